```python
import math
import jax, jax.numpy as jnp
from jax import lax
import numpy as np

D_MODEL = 1024
BATCH = 16
SEQ = 2048
DEPTH = 1
DEC_BATCH = 16
DEC_SEQ = 64
PAST_LEN = 4096

CHUNK = 64
N_MEM = 256
H_A = 8
DH_A = 64
BAND_CHUNKS = 8
WIN_A = BAND_CHUNKS * CHUNK
REL_CLIP = 128
H_B = 4
DH_B = 64
H_M = 4
DH_M = 128
D_A = H_A * DH_A
D_B = H_B * 2 * DH_B
D_M = H_M * DH_M
N_BRANCH = 3
D_IN = 3 * D_A + 3 * D_B + D_M + N_BRANCH * D_MODEL
D_FF = 2816
CONV_W = 3
Q_BLOCK = 128
EPS = 1e-6
NEG_INF = -1e30

kernel_name = "hybrid_streaming_encoder_step"


def _rmsnorm(x, g):
    xf = x.astype(jnp.float32)
    y = xf * lax.rsqrt(jnp.mean(xf * xf, axis=-1, keepdims=True) + EPS)
    return (y * g.astype(jnp.float32)).astype(x.dtype)


def _lambda_init(layer):
    return 0.8 - 0.6 * math.exp(-0.3 * layer)


def _in_proj(h, w_in, b_gate):
    B, T, _ = h.shape
    z = h @ w_in
    bounds = [D_A, 2 * D_A, 3 * D_A, 3 * D_A + D_B, 3 * D_A + 2 * D_B, 3 * D_A + 3 * D_B, 3 * D_A + 3 * D_B + D_M]
    qa, ka, va, qb, kb, vb, qm, gl = jnp.split(z, bounds, axis=-1)
    qa = qa.reshape(B, T, H_A, DH_A)
    ka = ka.reshape(B, T, H_A, DH_A)
    va = va.reshape(B, T, H_A, DH_A)
    qb = qb.reshape(B, T, H_B, 2, DH_B)
    kb = kb.reshape(B, T, H_B, 2, DH_B)
    vb = vb.reshape(B, T, H_B, 2 * DH_B)
    qm = qm.reshape(B, T, H_M, DH_M)
    gates = jax.nn.sigmoid((gl + b_gate).astype(jnp.float32)).astype(h.dtype)
    return qa, ka, va, qb, kb, vb, qm, gates.reshape(B, T, N_BRANCH, D_MODEL)


def _rel_bias(rel_bias, dist):
    idx = jnp.clip(dist, -REL_CLIP, REL_CLIP) + REL_CLIP
    return rel_bias.astype(jnp.float32)[:, idx]


def _band_attn_prompt(q, k, v, rel_bias):
    B, S = q.shape[:2]
    nc = S // CHUNK
    nk = (BAND_CHUNKS + 1) * CHUNK
    pad = ((0, 0), (WIN_A, 0), (0, 0), (0, 0))
    kp = jnp.pad(k, pad).reshape(B, nc + BAND_CHUNKS, CHUNK, H_A, DH_A)
    vp = jnp.pad(v, pad).reshape(B, nc + BAND_CHUNKS, CHUNK, H_A, DH_A)
    kband = jnp.concatenate([kp[:, j:j + nc] for j in range(BAND_CHUNKS + 1)], axis=2)
    vband = jnp.concatenate([vp[:, j:j + nc] for j in range(BAND_CHUNKS + 1)], axis=2)
    qc = q.reshape(B, nc, CHUNK, H_A, DH_A)
    qi = jnp.arange(CHUNK)
    kj = jnp.arange(nk)
    bias = _rel_bias(rel_bias, qi[:, None] + WIN_A - kj[None, :])
    key_pos = (jnp.arange(nc)[:, None] - BAND_CHUNKS) * CHUNK + kj[None, :]
    valid = key_pos >= 0
    s = jnp.einsum('bnqhd,bnkhd->bnhqk', qc, kband).astype(jnp.float32) * (DH_A ** -0.5) + bias[None, None]
    s = jnp.where(valid[None, :, None, None, :], s, NEG_INF)
    p = jax.nn.softmax(s, axis=-1).astype(v.dtype)
    o = jnp.einsum('bnhqk,bnkhd->bnqhd', p, vband)
    return o.reshape(B, S, D_A)


def _band_attn_sample(q, k_new, v_new, cache_k, cache_v, rel_bias):
    B, T = q.shape[:2]
    L = cache_k.shape[1]
    kk = jnp.concatenate([cache_k, k_new], axis=1)
    vv = jnp.concatenate([cache_v, v_new], axis=1)
    bias = _rel_bias(rel_bias, jnp.arange(T)[:, None] + L - jnp.arange(L + T)[None, :])
    s = jnp.einsum('bqhd,bkhd->bhqk', q, kk).astype(jnp.float32) * (DH_A ** -0.5) + bias[None]
    p = jax.nn.softmax(s, axis=-1).astype(vv.dtype)
    o = jnp.einsum('bhqk,bkhd->bqhd', p, vv)
    return o.reshape(B, T, D_A), kk[:, T:], vv[:, T:]


def _diff_attend(q, k, v, pos_q, pos_k, lam, slopes):
    s = jnp.einsum('bqhmd,bkhmd->bhmqk', q, k).astype(jnp.float32) * (DH_B ** -0.5)
    dist = jnp.abs(pos_q[:, None] - pos_k[None, :]).astype(jnp.float32)
    allowed = (pos_k[None, :] // CHUNK) <= (pos_q[:, None] // CHUNK)
    s = jnp.where(allowed, s - slopes[:, None, None, None] * dist, NEG_INF)
    p = jax.nn.softmax(s, axis=-1)
    a = (p[:, :, 0] - lam * p[:, :, 1]).astype(v.dtype)
    return jnp.einsum('bhqk,bkhe->bqhe', a, v)


def _diff_attn_prompt(q, k, v, lam, slopes):
    B, S = q.shape[:2]
    nb = S // Q_BLOCK
    qblk = jnp.moveaxis(q.reshape(B, nb, Q_BLOCK, H_B, 2, DH_B), 1, 0)
    pos_k = jnp.arange(S)

    def one_block(args):
        qb, bi = args
        return _diff_attend(qb, k, v, bi * Q_BLOCK + jnp.arange(Q_BLOCK), pos_k, lam, slopes)

    o = lax.map(one_block, (qblk, jnp.arange(nb)))
    return jnp.moveaxis(o, 0, 1).reshape(B, S, H_B, 2 * DH_B)


def _diff_out(o, g_sub, lam_init):
    B, T = o.shape[:2]
    return (_rmsnorm(o, g_sub) * (1.0 - lam_init)).reshape(B, T, D_B)


def _mem_kv(mem, g_mem, w_mem_kv):
    B, M, _ = mem.shape
    kv = _rmsnorm(mem, g_mem) @ w_mem_kv
    mk, mv = jnp.split(kv, 2, axis=-1)
    return mk.reshape(B, M, H_M, DH_M), mv.reshape(B, M, H_M, DH_M)


def _mem_attend(q, mk, mv):
    B, T = q.shape[:2]
    s = jnp.einsum('bqhd,bmhd->bhqm', q, mk).astype(jnp.float32) * (DH_M ** -0.5)
    p = jax.nn.softmax(s, axis=-1).astype(mv.dtype)
    return jnp.einsum('bhqm,bmhd->bqhd', p, mv).reshape(B, T, D_M)


def _merge(oa, ob, om, gates, w_oa, w_ob, w_om, w_out):
    m = gates[:, :, 0] * (oa @ w_oa) + gates[:, :, 1] * (ob @ w_ob) + gates[:, :, 2] * (om @ w_om)
    return m @ w_out


def _conv_ffn(h, prev, w_up, w_conv, b_conv, w_down):
    T = h.shape[1]
    a, g = jnp.split(h @ w_up, 2, axis=-1)
    ext = jnp.concatenate([prev.astype(a.dtype), a], axis=1)
    c = b_conv
    for j in range(CONV_W):
        c = c + ext[:, j:j + T] * w_conv[j]
    y = (jax.nn.gelu(c) * g) @ w_down
    return y, ext[:, -(CONV_W - 1):]


def setup_inputs(seed: int = 0) -> dict:
    key = jax.random.key(seed)
    ks = iter(jax.random.split(key, 40))

    def nrm(shape, scale=1.0):
        return jax.random.normal(next(ks), shape, jnp.float32) * scale

    a_buf = min(WIN_A, PAST_LEN)
    return {
        "x_prompt": nrm((BATCH, SEQ, D_MODEL)),
        "x_sample": nrm((DEC_BATCH, DEC_SEQ, D_MODEL)),
        "mem_prompt": nrm((BATCH, N_MEM, D_MODEL)),
        "cache_a_k": nrm((DEPTH, DEC_BATCH, a_buf, H_A, DH_A)),
        "cache_a_v": nrm((DEPTH, DEC_BATCH, a_buf, H_A, DH_A)),
        "cache_b_k": nrm((DEPTH, DEC_BATCH, PAST_LEN, H_B, 2 * DH_B)),
        "cache_b_v": nrm((DEPTH, DEC_BATCH, PAST_LEN, H_B, 2 * DH_B)),
        "cache_mem_k": nrm((DEPTH, DEC_BATCH, N_MEM, H_M, DH_M)),
        "cache_mem_v": nrm((DEPTH, DEC_BATCH, N_MEM, H_M, DH_M)),
        "state_ffn_conv": nrm((DEPTH, DEC_BATCH, CONV_W - 1, D_FF)),
        "g_mix": 1.0 + nrm((DEPTH, D_MODEL), 0.02),
        "w_in": nrm((DEPTH, D_MODEL, D_IN), D_MODEL ** -0.5),
        "b_gate": nrm((DEPTH, N_BRANCH * D_MODEL), 0.02),
        "rel_bias": nrm((DEPTH, H_A, 2 * REL_CLIP + 1), 0.1),
        "lam_q": nrm((DEPTH, 2, DH_B), 0.1),
        "lam_k": nrm((DEPTH, 2, DH_B), 0.1),
        "g_sub": 1.0 + nrm((DEPTH, 2 * DH_B), 0.02),
        "g_mem": 1.0 + nrm((DEPTH, D_MODEL), 0.02),
        "w_mem_kv": nrm((DEPTH, D_MODEL, 2 * D_M), D_MODEL ** -0.5),
        "w_oa": nrm((DEPTH, D_A, D_MODEL), D_A ** -0.5),
        "w_ob": nrm((DEPTH, D_B, D_MODEL), D_B ** -0.5),
        "w_om": nrm((DEPTH, D_M, D_MODEL), D_M ** -0.5),
        "w_out": nrm((DEPTH, D_MODEL, D_MODEL), D_MODEL ** -0.5),
        "g_ffn": 1.0 + nrm((DEPTH, D_MODEL), 0.02),
        "w_up": nrm((DEPTH, D_MODEL, 2 * D_FF), D_MODEL ** -0.5),
        "w_conv": nrm((DEPTH, CONV_W, D_FF), CONV_W ** -0.5),
        "b_conv": nrm((DEPTH, D_FF), 0.02),
        "w_down": nrm((DEPTH, D_FF, D_MODEL), D_FF ** -0.5),
        "g_final": 1.0 + nrm((D_MODEL,), 0.02),
    }


def reference(x_prompt, x_sample, mem_prompt, cache_a_k, cache_a_v, cache_b_k, cache_b_v,
              cache_mem_k, cache_mem_v, state_ffn_conv, g_mix, w_in, b_gate, rel_bias,
              lam_q, lam_k, g_sub, g_mem, w_mem_kv, w_oa, w_ob, w_om, w_out, g_ffn,
              w_up, w_conv, b_conv, w_down, g_final):
    slopes = jnp.exp2(-8.0 * jnp.arange(1, H_B + 1, dtype=jnp.float32) / H_B)
    xp, xs = x_prompt, x_sample
    B, S = xp.shape[:2]
    Bd, T = xs.shape[:2]
    L = cache_b_k.shape[2]
    a_keep = min(WIN_A, S)
    pa_k, pa_v, pb_k, pb_v, pm_k, pm_v, pconv = [], [], [], [], [], [], []
    sa_k, sa_v, sb_k, sb_v, sconv = [], [], [], [], []
    for l in range(DEPTH):
        lam_init = _lambda_init(l)
        lq = lam_q[l].astype(jnp.float32)
        lk = lam_k[l].astype(jnp.float32)
        lam = jnp.exp(jnp.sum(lq[0] * lk[0])) - jnp.exp(jnp.sum(lq[1] * lk[1])) + lam_init

        h = _rmsnorm(xp, g_mix[l])
        qa, ka, va, qb, kb, vb, qm, gates = _in_proj(h, w_in[l], b_gate[l])
        oa = _band_attn_prompt(qa, ka, va, rel_bias[l])
        ob = _diff_out(_diff_attn_prompt(qb, kb, vb, lam, slopes), g_sub[l], lam_init)
        mk, mv = _mem_kv(mem_prompt, g_mem[l], w_mem_kv[l])
        om = _mem_attend(qm, mk, mv)
        xp = xp + _merge(oa, ob, om, gates, w_oa[l], w_ob[l], w_om[l], w_out[l])
        f, conv_buf = _conv_ffn(_rmsnorm(xp, g_ffn[l]), jnp.zeros((B, CONV_W - 1, D_FF), xp.dtype),
                                w_up[l], w_conv[l], b_conv[l], w_down[l])
        xp = xp + f
        pa_k.append(ka[:, S - a_keep:])
        pa_v.append(va[:, S - a_keep:])
        pb_k.append(kb.reshape(B, S, H_B, 2 * DH_B))
        pb_v.append(vb)
        pm_k.append(mk)
        pm_v.append(mv)
        pconv.append(conv_buf)

        h = _rmsnorm(xs, g_mix[l])
        qa, ka, va, qb, kb, vb, qm, gates = _in_proj(h, w_in[l], b_gate[l])
        oa, new_ak, new_av = _band_attn_sample(qa, ka, va, cache_a_k[l], cache_a_v[l], rel_bias[l])
        kk = jnp.concatenate([cache_b_k[l].reshape(Bd, L, H_B, 2, DH_B), kb], axis=1)
        vv = jnp.concatenate([cache_b_v[l], vb], axis=1)
        ob = _diff_attend(qb, kk, vv, L + jnp.arange(T), jnp.arange(L + T), lam, slopes)
        ob = _diff_out(ob, g_sub[l], lam_init)
        om = _mem_attend(qm, cache_mem_k[l], cache_mem_v[l])
        xs = xs + _merge(oa, ob, om, gates, w_oa[l], w_ob[l], w_om[l], w_out[l])
        f, conv_buf = _conv_ffn(_rmsnorm(xs, g_ffn[l]), state_ffn_conv[l],
                                w_up[l], w_conv[l], b_conv[l], w_down[l])
        xs = xs + f
        sa_k.append(new_ak)
        sa_v.append(new_av)
        sb_k.append(kb.reshape(Bd, T, H_B, 2 * DH_B))
        sb_v.append(vb)
        sconv.append(conv_buf)

    y_prompt = _rmsnorm(xp, g_final)
    y_sample = _rmsnorm(xs, g_final)
    return (y_prompt, y_sample,
            jnp.stack(pa_k), jnp.stack(pa_v), jnp.stack(pb_k), jnp.stack(pb_v),
            jnp.stack(pm_k), jnp.stack(pm_v), jnp.stack(pconv),
            jnp.stack(sa_k), jnp.stack(sa_v), jnp.stack(sb_k), jnp.stack(sb_v), jnp.stack(sconv))
```

```python
import functools
import math

import jax
import jax.numpy as jnp
from jax import lax
from jax.experimental import pallas as pl
from jax.experimental.pallas import tpu as pltpu

CHUNK = 64
BAND_CHUNKS = 8
WIN_A = BAND_CHUNKS * CHUNK
REL_CLIP = 128
H_A = 8
DH_A = 64
H_B = 4
DH_B = 64
H_M = 4
DH_M = 128
N_BRANCH = 3
CONV_W = 3
EPS = 1e-6
NEG_INF = -1e30

LANES = 128
FF_CHUNK = 256
VMEM_LIMIT = 56 * 1024 * 1024

BF16 = jnp.bfloat16
F32 = jnp.float32


def _rms(x, g):
    y = x * lax.rsqrt(jnp.mean(x * x, axis=-1, keepdims=True) + EPS)
    return y * g


def _dot(a, b):
    return jnp.dot(a, b, preferred_element_type=F32)


def _dot_nt(a, b):
    return lax.dot_general(a, b, (((1,), (1,)), ((), ())), preferred_element_type=F32)


def _const_spec(shape):
    nd = len(shape)
    return pl.BlockSpec(shape, lambda *_: (0,) * nd, pipeline_mode=pl.Buffered(1))


def _params(sem):
    return pltpu.CompilerParams(dimension_semantics=sem, vmem_limit_bytes=VMEM_LIMIT)


def _norm_proj_kernel(x_ref, g_ref, w_ref, *o_refs, cols, scales):
    h = _rms(x_ref[...], g_ref[...]).astype(BF16)
    for o_ref, (c0, wd), sc in zip(o_refs, cols, scales):
        z = _dot(h, w_ref[:, c0:c0 + wd])
        if sc != 1.0:
            z = z * sc
        o_ref[...] = z.astype(o_ref.dtype)


def _norm_proj(x, g, w, cols, scales, dtypes, tm):
    n, d = x.shape
    kern = functools.partial(_norm_proj_kernel, cols=cols, scales=scales)
    return pl.pallas_call(
        kern,
        grid=(n // tm,),
        in_specs=[
            pl.BlockSpec((tm, d), lambda i: (i, 0)),
            _const_spec((1, d)),
            _const_spec(w.shape),
        ],
        out_specs=[pl.BlockSpec((tm, wd), lambda i: (i, 0)) for _, wd in cols],
        out_shape=[jax.ShapeDtypeStruct((n, wd), dt) for (_, wd), dt in zip(cols, dtypes)],
        compiler_params=_params(("parallel",)),
        name="norm_proj",
    )(x, g, w)


def _band_kernel(*refs, tq, seq, use_hist):
    if use_hist:
        q_ref, k_ref, v_ref, hk_ref, hv_ref, bias_ref, o_ref, kp, vp = refs
    else:
        q_ref, k_ref, v_ref, bias_ref, o_ref, kp, vp = refs
    m = pl.program_id(1)
    win = WIN_A + tq

    @pl.when(m == 0)
    def _():
        if use_hist:
            kp[0:WIN_A, :] = hk_ref[0].astype(BF16)
            vp[0:WIN_A, :] = hv_ref[0].astype(BF16)
        else:
            kp[0:WIN_A, :] = jnp.zeros((WIN_A, kp.shape[1]), BF16)
            vp[0:WIN_A, :] = jnp.zeros((WIN_A, vp.shape[1]), BF16)
        kp[WIN_A:WIN_A + seq, :] = k_ref[0].astype(BF16)
        vp[WIN_A:WIN_A + seq, :] = v_ref[0].astype(BF16)

    off = pl.multiple_of(m * tq, tq)
    lane = lax.broadcasted_iota(jnp.int32, (tq, LANES), 1)
    lo = lane < DH_A
    if not use_hist:
        col = lax.broadcasted_iota(jnp.int32, (2 * tq, win), 1)
        in_seq = col >= (WIN_A - m * tq)
    zero = jnp.zeros((tq, LANES), BF16)
    for hp in range(H_A // 2):
        cs = slice(hp * LANES, (hp + 1) * LANES)
        qs = q_ref[0, :, cs]
        qq = jnp.concatenate([jnp.where(lo, qs, zero), jnp.where(lo, zero, qs)], axis=0)
        ks = kp[pl.ds(off, win), cs]
        vs = vp[pl.ds(off, win), cs]
        s = _dot_nt(qq, ks) + bias_ref[hp]
        if not use_hist:
            s = jnp.where(in_seq, s, NEG_INF)
        mx = jnp.max(s, axis=-1, keepdims=True)
        p = jnp.exp(s - mx)
        l = jnp.sum(p, axis=-1, keepdims=True)
        o = _dot(p.astype(BF16), vs) / l
        o_ref[0, :, cs] = jnp.where(lo, o[:tq], o[tq:]).astype(o_ref.dtype)


def _band_attn(q, k, v, bias, tq, hist=None):
    b, seq, da = q.shape
    win = WIN_A + tq
    use_hist = hist is not None
    kern = functools.partial(_band_kernel, tq=tq, seq=seq, use_hist=use_hist)
    full = pl.BlockSpec((1, seq, da), lambda i, m: (i, 0, 0))
    in_specs = [pl.BlockSpec((1, tq, da), lambda i, m: (i, m, 0)), full, full]
    args = [q, k, v]
    if use_hist:
        hspec = pl.BlockSpec((1, WIN_A, da), lambda i, m: (i, 0, 0))
        in_specs += [hspec, hspec]
        args += list(hist)
    in_specs.append(_const_spec(bias.shape))
    args.append(bias)
    return pl.pallas_call(
        kern,
        grid=(b, seq // tq),
        in_specs=in_specs,
        out_specs=pl.BlockSpec((1, tq, da), lambda i, m: (i, m, 0)),
        out_shape=jax.ShapeDtypeStruct((b, seq, da), BF16),
        scratch_shapes=[pltpu.VMEM((WIN_A + seq, da), BF16), pltpu.VMEM((WIN_A + seq, da), BF16)],
        compiler_params=_params(("parallel", "arbitrary")),
        name="band_attn",
    )(*args)


def _band_bias_table(rel_bias, tq):
    win = WIN_A + tq
    r = jnp.arange(tq)[:, None]
    j = jnp.arange(win)[None, :]
    idx = jnp.clip(r + WIN_A - j, -REL_CLIP, REL_CLIP) + REL_CLIP
    bias = rel_bias.astype(F32)[:, idx]
    qc = r // CHUNK
    kc = j // CHUNK
    band = (kc >= qc) & (kc <= qc + BAND_CHUNKS)
    tbl = jnp.where(band[None], bias, NEG_INF)
    return tbl.reshape(H_A // 2, 2 * tq, win)


def _split_maps(q):
    tq = q.shape[0]
    lane = lax.broadcasted_iota(jnp.int32, (tq, LANES), 1)
    lo = lane < DH_B
    zero = jnp.zeros_like(q)
    return jnp.concatenate([jnp.where(lo, q, zero), jnp.where(lo, zero, q)], axis=0)


def _row_col_diff(tq, tk):
    r = lax.broadcasted_iota(jnp.int32, (2 * tq, tk), 0)
    c = lax.broadcasted_iota(jnp.int32, (2 * tq, tk), 1)
    rr = jnp.where(r >= tq, r - tq, r)
    return rr, c


def _online_step(qq, kb, vb, bias, carry):
    m, l, acc = carry
    s = _dot_nt(qq, kb) + bias
    m_new = jnp.maximum(m, jnp.max(s, axis=-1, keepdims=True))
    alpha = jnp.exp(m - m_new)
    p = jnp.exp(s - m_new)
    l = alpha * l + jnp.sum(p, axis=-1, keepdims=True)
    acc = alpha * acc + _dot(p.astype(BF16), vb)
    return m_new, l, acc


def _diff_finish(carry, tq, lq_ref, lk_ref, gsub_ref, lam_init):
    _, l, acc = carry
    o = acc / l
    e = jnp.exp(jnp.sum(lq_ref[...] * lk_ref[...], axis=-1, keepdims=True))
    lam = e[0:1] - e[1:2] + lam_init
    od = o[:tq] - lam * o[tq:]
    return _rms(od, gsub_ref[...]) * (1.0 - lam_init)


def _diff_init(tq):
    return (jnp.full((2 * tq, 1), NEG_INF, F32), jnp.zeros((2 * tq, 1), F32),
            jnp.zeros((2 * tq, LANES), F32))


def _diff_prompt_kernel(slopes_ref, q_ref, k_ref, v_ref, lq_ref, lk_ref, gsub_ref, o_ref,
                        kbf, vbf, *, tq, lam_init):
    h = pl.program_id(1)
    qi = pl.program_id(2)

    @pl.when(qi == 0)
    def _():
        kbf[...] = k_ref[0].astype(BF16)
        vbf[...] = v_ref[0].astype(BF16)

    slope = slopes_ref[h]
    qq = _split_maps(q_ref[0])
    rr, c = _row_col_diff(tq, tq)
    d = (rr - c).astype(F32)
    nsd = -slope * d

    def body(j, carry):
        off = pl.multiple_of(j * tq, tq)
        bias = nsd - slope * ((qi - j) * tq).astype(F32)
        return _online_step(qq, kbf[pl.ds(off, tq), :], vbf[pl.ds(off, tq), :], bias, carry)

    carry = lax.fori_loop(0, qi, body, _diff_init(tq))
    off = pl.multiple_of(qi * tq, tq)
    allowed = (c // CHUNK) <= (rr // CHUNK)
    bias = jnp.where(allowed, -slope * jnp.abs(d), NEG_INF)
    carry = _online_step(qq, kbf[pl.ds(off, tq), :], vbf[pl.ds(off, tq), :], bias, carry)
    o_ref[0] = _diff_finish(carry, tq, lq_ref, lk_ref, gsub_ref, lam_init).astype(o_ref.dtype)


def _smem_spec():
    return pl.BlockSpec(memory_space=pltpu.SMEM)


def _diff_attn_prompt(slopes, q, k, v, lam_q, lam_k, g_sub, tq, lam_init):
    b, seq, db = q.shape
    kern = functools.partial(_diff_prompt_kernel, tq=tq, lam_init=lam_init)
    kv_spec = pl.BlockSpec((1, seq, LANES), lambda i, h, j: (i, 0, h))
    q_spec = pl.BlockSpec((1, tq, LANES), lambda i, h, j: (i, j, h))
    return pl.pallas_call(
        kern,
        grid=(b, H_B, seq // tq),
        in_specs=[_smem_spec(), q_spec, kv_spec, kv_spec,
                  _const_spec(lam_q.shape), _const_spec(lam_k.shape), _const_spec(g_sub.shape)],
        out_specs=q_spec,
        out_shape=jax.ShapeDtypeStruct((b, seq, db), BF16),
        scratch_shapes=[pltpu.VMEM((seq, LANES), BF16), pltpu.VMEM((seq, LANES), BF16)],
        compiler_params=_params(("parallel", "parallel", "arbitrary")),
        name="diff_attn_prompt",
    )(slopes, q, k, v, lam_q, lam_k, g_sub)


def _diff_sample_kernel(slopes_ref, q_ref, kc_ref, vc_ref, kn_ref, vn_ref, lq_ref, lk_ref, gsub_ref,
                        o_ref, *, t, past, tk, lam_init):
    h = pl.program_id(1)
    slope = slopes_ref[h]
    qq = _split_maps(q_ref[0])
    rr, c = _row_col_diff(t, tk)
    nsd = -slope * (rr - c).astype(F32)

    def body(j, carry):
        off = pl.multiple_of(j * tk, tk)
        bias = nsd - slope * (past - j * tk).astype(F32)
        kb = kc_ref[0, pl.ds(off, tk), :].astype(BF16)
        vb = vc_ref[0, pl.ds(off, tk), :].astype(BF16)
        return _online_step(qq, kb, vb, bias, carry)

    carry = lax.fori_loop(0, past // tk, body, _diff_init(t))
    rr, c = _row_col_diff(t, t)
    bias = -slope * jnp.abs((rr - c).astype(F32))
    carry = _online_step(qq, kn_ref[0].astype(BF16), vn_ref[0].astype(BF16), bias, carry)
    o_ref[0] = _diff_finish(carry, t, lq_ref, lk_ref, gsub_ref, lam_init).astype(o_ref.dtype)


def _diff_attn_sample(slopes, q, kc, vc, kn, vn, lam_q, lam_k, g_sub, tk, lam_init):
    b, t, db = q.shape
    past = kc.shape[1]
    kern = functools.partial(_diff_sample_kernel, t=t, past=past, tk=tk, lam_init=lam_init)
    new_spec = pl.BlockSpec((1, t, LANES), lambda i, h: (i, 0, h))
    cache_spec = pl.BlockSpec((1, past, LANES), lambda i, h: (i, 0, h))
    return pl.pallas_call(
        kern,
        grid=(b, H_B),
        in_specs=[_smem_spec(), new_spec, cache_spec, cache_spec, new_spec, new_spec,
                  _const_spec(lam_q.shape), _const_spec(lam_k.shape), _const_spec(g_sub.shape)],
        out_specs=new_spec,
        out_shape=jax.ShapeDtypeStruct((b, t, db), BF16),
        compiler_params=_params(("parallel", "parallel")),
        name="diff_attn_sample",
    )(slopes, q, kc, vc, kn, vn, lam_q, lam_k, g_sub)


def _mem_kernel(q_ref, mk_ref, mv_ref, o_ref):
    scale = DH_M ** -0.5
    for h in range(H_M):
        cs = slice(h * LANES, (h + 1) * LANES)
        kh = mk_ref[0, :, cs].astype(BF16)
        vh = mv_ref[0, :, cs].astype(BF16)
        s = _dot_nt(q_ref[0, :, cs], kh) * scale
        mx = jnp.max(s, axis=-1, keepdims=True)
        p = jnp.exp(s - mx)
        l = jnp.sum(p, axis=-1, keepdims=True)
        o_ref[0, :, cs] = (_dot(p.astype(BF16), vh) / l).astype(o_ref.dtype)


def _mem_attn(q, mk, mv, tq):
    b, seq, dm = q.shape
    nm = mk.shape[1]
    kv_spec = pl.BlockSpec((1, nm, dm), lambda i, j: (i, 0, 0))
    q_spec = pl.BlockSpec((1, tq, dm), lambda i, j: (i, j, 0))
    return pl.pallas_call(
        _mem_kernel,
        grid=(b, seq // tq),
        in_specs=[q_spec, kv_spec, kv_spec],
        out_specs=q_spec,
        out_shape=jax.ShapeDtypeStruct((b, seq, dm), BF16),
        compiler_params=_params(("parallel", "parallel")),
        name="mem_attn",
    )(q, mk, mv)


def _merge_kernel(x_ref, oa_ref, ob_ref, om_ref, g_ref, wg_ref, bg_ref, wo_ref, wout_ref, o_ref, m_scr,
                  *, tn):
    x = x_ref[...]
    h = _rms(x, g_ref[...]).astype(BF16)
    o_refs = (oa_ref, ob_ref, om_ref)
    d = x.shape[1]
    for c0 in range(0, d, tn):
        cs = slice(c0, c0 + tn)
        m = None
        for br in range(N_BRANCH):
            gate = jax.nn.sigmoid(_dot(h, wg_ref[br, :, cs]) + bg_ref[br, :, cs])
            term = gate * _dot(o_refs[br][...], wo_ref[br, :, cs])
            m = term if m is None else m + term
        m_scr[:, cs] = m.astype(BF16)
    o_ref[...] = x + _dot(m_scr[...], wout_ref[...])


def _merge(x, oa, ob, om, g, wg, bg, wo, wout, tm, tn=256):
    n, d = x.shape
    kern = functools.partial(_merge_kernel, tn=tn)
    tok = lambda wd: pl.BlockSpec((tm, wd), lambda i: (i, 0))
    return pl.pallas_call(
        kern,
        grid=(n // tm,),
        in_specs=[tok(d), tok(oa.shape[1]), tok(ob.shape[1]), tok(om.shape[1]), _const_spec(g.shape),
                  _const_spec(wg.shape), _const_spec(bg.shape), _const_spec(wo.shape),
                  _const_spec(wout.shape)],
        out_specs=tok(d),
        out_shape=jax.ShapeDtypeStruct((n, d), F32),
        scratch_shapes=[pltpu.VMEM((tm, d), BF16)],
        compiler_params=_params(("parallel",)),
        name="merge",
    )(x, oa, ob, om, g, wg, bg, wo, wout)


def _ffn_kernel(x_ref, g_ref, prev_ref, wup_ref, wconv_ref, bconv_ref, wdown_ref, gfin_ref,
                y_ref, conv_ref, carry, *, tm, nch, final_norm):
    @pl.when(pl.program_id(1) == 0)
    def _():
        carry[...] = prev_ref[0]

    x = x_ref[0]
    h = _rms(x, g_ref[...]).astype(BF16)
    row = lax.broadcasted_iota(jnp.int32, (tm, FF_CHUNK), 0)
    acc = jnp.zeros(x.shape, F32)
    for ch in range(nch):
        cs = slice(ch * FF_CHUNK, (ch + 1) * FF_CHUNK)
        a = _dot(h, wup_ref[ch])
        gate = _dot(h, wup_ref[nch + ch])
        p2 = carry[0:1, cs]
        p1 = carry[1:2, cs]
        a1 = jnp.where(row == 0, p1, pltpu.roll(a, 1, axis=0))
        a2 = jnp.where(row == 0, p2, jnp.where(row == 1, p1, pltpu.roll(a, 2, axis=0)))
        conv = bconv_ref[:, cs] + a2 * wconv_ref[0:1, cs]
        conv = conv + a1 * wconv_ref[1:2, cs]
        conv = conv + a * wconv_ref[2:3, cs]
        carry[:, cs] = a[tm - (CONV_W - 1):, :]
        u = (jax.nn.gelu(conv) * gate).astype(BF16)
        acc = acc + _dot(u, wdown_ref[ch])
    y = x + acc
    if final_norm:
        y = _rms(y, gfin_ref[...])
    y_ref[0] = y
    conv_ref[0] = carry[...]


def _conv_ffn(x, g, prev, wup, wconv, bconv, wdown, gfin, tm, final_norm):
    b, seq, d = x.shape
    nch = wdown.shape[0]
    ff = nch * FF_CHUNK
    kern = functools.partial(_ffn_kernel, tm=tm, nch=nch, final_norm=final_norm)
    tok = pl.BlockSpec((1, tm, d), lambda i, j: (i, j, 0))
    state = pl.BlockSpec((1, CONV_W - 1, ff), lambda i, j: (i, 0, 0))
    return pl.pallas_call(
        kern,
        grid=(b, seq // tm),
        in_specs=[tok, _const_spec(g.shape), state, _const_spec(wup.shape), _const_spec(wconv.shape),
                  _const_spec(bconv.shape), _const_spec(wdown.shape), _const_spec(gfin.shape)],
        out_specs=[tok, state],
        out_shape=[jax.ShapeDtypeStruct((b, seq, d), F32),
                   jax.ShapeDtypeStruct((b, CONV_W - 1, ff), F32)],
        scratch_shapes=[pltpu.VMEM((CONV_W - 1, ff), F32)],
        compiler_params=_params(("parallel", "arbitrary")),
        name="conv_ffn",
    )(x, g, prev, wup, wconv, bconv, wdown, gfin)


def _token_tile(n, pref):
    return pref if n % pref == 0 else n


def _layer(x, w, *, lam_init, final_norm, conv_prev, mem=None, cache=None):
    b, t, d = x.shape
    n = b * t
    tm = _token_tile(n, 512)
    d_a, d_b, d_m = H_A * DH_A, H_B * 2 * DH_B, H_M * DH_M
    cols = [(0, d_a), (d_a, d_a), (2 * d_a, d_a), (3 * d_a, d_b), (3 * d_a + d_b, d_b),
            (3 * d_a + 2 * d_b, d_b), (3 * d_a + 3 * d_b, d_m)]
    scales = [DH_A ** -0.5, 1.0, 1.0, DH_B ** -0.5, 1.0, 1.0, 1.0]
    dtypes = [BF16, F32, F32, BF16, F32, F32, BF16]
    qa, ka, va, qb, kb, vb, qm = _norm_proj(x.reshape(n, d), w["g_mix"], w["w_qkv"], cols, scales,
                                            dtypes, tm)
    seq3 = lambda a: a.reshape(b, t, a.shape[-1])
    qa, ka, va, qb, kb, vb, qm = map(seq3, (qa, ka, va, qb, kb, vb, qm))

    if cache is None:
        tq_a = 128
        oa = _band_attn(qa, ka, va, _band_bias_table(w["rel_bias"], tq_a), tq_a)
        ob = _diff_attn_prompt(w["slopes"], qb, kb, vb, w["lam_q"], w["lam_k"], w["g_sub"], 256, lam_init)
        nm = mem.shape[1]
        mk, mv = _norm_proj(mem.reshape(b * nm, d), w["g_mem"], w["w_mem_kv"], [(0, d_m), (d_m, d_m)],
                            [1.0, 1.0], [F32, F32], _token_tile(b * nm, 512))
        mk = mk.reshape(b, nm, d_m)
        mv = mv.reshape(b, nm, d_m)
        om = _mem_attn(qm, mk, mv, _token_tile(t, 512))
    else:
        oa = _band_attn(qa, ka, va, _band_bias_table(w["rel_bias"], t), t,
                        hist=(cache["a_k"], cache["a_v"]))
        ob = _diff_attn_sample(w["slopes"], qb, cache["b_k"], cache["b_v"], kb, vb,
                               w["lam_q"], w["lam_k"], w["g_sub"], 512, lam_init)
        mk, mv = cache["m_k"], cache["m_v"]
        om = _mem_attn(qm, mk, mv, t)

    flat = lambda a: a.reshape(n, a.shape[-1])
    x1 = _merge(x.reshape(n, d), flat(oa), flat(ob), flat(om), w["g_mix"], w["w_gate"], w["b_gate"],
                w["w_o"], w["w_out"], tm)
    y, conv = _conv_ffn(x1.reshape(b, t, d), w["g_ffn"], conv_prev, w["w_up"], w["w_conv"], w["b_conv"],
                        w["w_down"], w["g_final"], _token_tile(t, 512), final_norm)
    return y, dict(ka=ka, va=va, kb=kb, vb=vb, mk=mk, mv=mv, conv=conv)


def kernel(x_prompt, x_sample, mem_prompt, cache_a_k, cache_a_v, cache_b_k, cache_b_v, cache_mem_k, cache_mem_v, state_ffn_conv, g_mix, w_in, b_gate, rel_bias, lam_q, lam_k, g_sub, g_mem, w_mem_kv, w_oa, w_ob, w_om, w_out, g_ffn, w_up, w_conv, b_conv, w_down, g_final):
    depth, d, _ = w_in.shape
    b, s, _ = x_prompt.shape
    bd, t, _ = x_sample.shape
    past = cache_b_k.shape[2]
    a_buf = cache_a_k.shape[2]
    ff = w_down.shape[1]
    d_a, d_b, d_m = H_A * DH_A, H_B * 2 * DH_B, H_M * DH_M
    n_qkv = 3 * d_a + 3 * d_b + d_m
    a_keep = min(WIN_A, s)
    assert a_buf == WIN_A and a_keep == WIN_A and t == CHUNK and past % CHUNK == 0 and ff % FF_CHUNK == 0
    nch = ff // FF_CHUNK
    slopes = jnp.exp2(-8.0 * jnp.arange(1, H_B + 1, dtype=F32) / H_B)

    xp, xs = x_prompt, x_sample
    outs = {k: [] for k in ("pa_k", "pa_v", "pb_k", "pb_v", "pm_k", "pm_v", "pconv",
                            "sa_k", "sa_v", "sb_k", "sb_v", "sconv")}
    for l in range(depth):
        lam_init = 0.8 - 0.6 * math.exp(-0.3 * l)
        w = dict(
            g_mix=g_mix[l][None], g_mem=g_mem[l][None], g_ffn=g_ffn[l][None], g_final=g_final[None],
            g_sub=g_sub[l][None], lam_q=lam_q[l], lam_k=lam_k[l], rel_bias=rel_bias[l], slopes=slopes,
            w_qkv=w_in[l][:, :n_qkv].astype(BF16),
            w_gate=w_in[l][:, n_qkv:].reshape(d, N_BRANCH, d).transpose(1, 0, 2).astype(BF16),
            b_gate=b_gate[l].reshape(N_BRANCH, 1, d),
            w_mem_kv=w_mem_kv[l].astype(BF16),
            w_o=jnp.stack([w_oa[l], w_ob[l], w_om[l]]).astype(BF16),
            w_out=w_out[l].astype(BF16),
            w_up=w_up[l].reshape(d, 2 * nch, FF_CHUNK).transpose(1, 0, 2).astype(BF16),
            w_conv=w_conv[l], b_conv=b_conv[l][None],
            w_down=w_down[l].reshape(nch, FF_CHUNK, d).astype(BF16),
        )
        last = l == depth - 1
        xp, po = _layer(xp, w, lam_init=lam_init, final_norm=last,
                        conv_prev=jnp.zeros((b, CONV_W - 1, ff), F32), mem=mem_prompt)
        cache = dict(a_k=cache_a_k[l].reshape(bd, a_buf, d_a), a_v=cache_a_v[l].reshape(bd, a_buf, d_a),
                     b_k=cache_b_k[l].reshape(bd, past, d_b), b_v=cache_b_v[l].reshape(bd, past, d_b),
                     m_k=cache_mem_k[l].reshape(bd, -1, d_m), m_v=cache_mem_v[l].reshape(bd, -1, d_m))
        xs, so = _layer(xs, w, lam_init=lam_init, final_norm=last, conv_prev=state_ffn_conv[l], cache=cache)

        outs["pa_k"].append(po["ka"][:, s - a_keep:].reshape(b, a_keep, H_A, DH_A))
        outs["pa_v"].append(po["va"][:, s - a_keep:].reshape(b, a_keep, H_A, DH_A))
        outs["pb_k"].append(po["kb"].reshape(b, s, H_B, 2 * DH_B))
        outs["pb_v"].append(po["vb"].reshape(b, s, H_B, 2 * DH_B))
        outs["pm_k"].append(po["mk"].reshape(b, -1, H_M, DH_M))
        outs["pm_v"].append(po["mv"].reshape(b, -1, H_M, DH_M))
        outs["pconv"].append(po["conv"])
        new_k = jnp.concatenate([cache["a_k"], so["ka"]], axis=1)[:, t:]
        new_v = jnp.concatenate([cache["a_v"], so["va"]], axis=1)[:, t:]
        outs["sa_k"].append(new_k.reshape(bd, a_buf, H_A, DH_A))
        outs["sa_v"].append(new_v.reshape(bd, a_buf, H_A, DH_A))
        outs["sb_k"].append(so["kb"].reshape(bd, t, H_B, 2 * DH_B))
        outs["sb_v"].append(so["vb"].reshape(bd, t, H_B, 2 * DH_B))
        outs["sconv"].append(so["conv"])

    st = {k: jnp.stack(v) for k, v in outs.items()}
    return (xp, xs, st["pa_k"], st["pa_v"], st["pb_k"], st["pb_v"], st["pm_k"], st["pm_v"], st["pconv"],
            st["sa_k"], st["sa_v"], st["sb_k"], st["sb_v"], st["sconv"])
```

```python
import functools
import math

import jax
import jax.numpy as jnp
from jax import lax
from jax.experimental import pallas as pl
from jax.experimental.pallas import tpu as pltpu

CHUNK = 64
BAND_CHUNKS = 8
WIN_A = BAND_CHUNKS * CHUNK
REL_CLIP = 128
H_A = 8
DH_A = 64
H_B = 4
DH_B = 64
H_M = 4
DH_M = 128
N_BRANCH = 3
CONV_W = 3
EPS = 1e-6
NEG_INF = -1e30

LANES = 128
FF_CHUNK = 256
VMEM_LIMIT = 56 * 1024 * 1024

LOG2E = math.log2(math.e)

BF16 = jnp.bfloat16
F32 = jnp.float32


def _rms(x, g):
    y = x * lax.rsqrt(jnp.mean(x * x, axis=-1, keepdims=True) + EPS)
    return y * g


def _dot(a, b):
    return jnp.dot(a, b, preferred_element_type=F32)


def _dot_nt(a, b):
    return lax.dot_general(a, b, (((1,), (1,)), ((), ())), preferred_element_type=F32)


def _dot_tn(a, b):
    return lax.dot_general(a, b, (((0,), (0,)), ((), ())), preferred_element_type=F32)


def _const_spec(shape):
    nd = len(shape)
    return pl.BlockSpec(shape, lambda *_: (0,) * nd, pipeline_mode=pl.Buffered(1))


def _params(sem):
    return pltpu.CompilerParams(dimension_semantics=sem, vmem_limit_bytes=VMEM_LIMIT)


def _norm_proj_kernel(x_ref, g_ref, w_ref, *o_refs, cols, scales):
    h = _rms(x_ref[...], g_ref[...]).astype(BF16)
    for o_ref, (c0, wd), sc in zip(o_refs, cols, scales):
        z = _dot(h, w_ref[:, c0:c0 + wd])
        if sc != 1.0:
            z = z * sc
        o_ref[...] = z.astype(o_ref.dtype)


def _norm_proj(x, g, w, cols, scales, dtypes, tm):
    n, d = x.shape
    kern = functools.partial(_norm_proj_kernel, cols=cols, scales=scales)
    return pl.pallas_call(
        kern,
        grid=(n // tm,),
        in_specs=[
            pl.BlockSpec((tm, d), lambda i: (i, 0)),
            _const_spec((1, d)),
            _const_spec(w.shape),
        ],
        out_specs=[pl.BlockSpec((tm, wd), lambda i: (i, 0)) for _, wd in cols],
        out_shape=[jax.ShapeDtypeStruct((n, wd), dt) for (_, wd), dt in zip(cols, dtypes)],
        compiler_params=_params(("parallel",)),
        name="norm_proj",
    )(x, g, w)


def _band_kernel(*refs, tq, seq, use_hist):
    if use_hist:
        q_ref, k_ref, v_ref, hk_ref, hv_ref, bias_ref, o_ref, kp, vp = refs
    else:
        q_ref, k_ref, v_ref, bias_ref, o_ref, kp, vp = refs
    m = pl.program_id(1)
    win = WIN_A + tq

    @pl.when(m == 0)
    def _():
        if use_hist:
            kp[0:WIN_A, :] = hk_ref[0].astype(BF16)
            vp[0:WIN_A, :] = hv_ref[0].astype(BF16)
        else:
            kp[0:WIN_A, :] = jnp.zeros((WIN_A, kp.shape[1]), BF16)
            vp[0:WIN_A, :] = jnp.zeros((WIN_A, vp.shape[1]), BF16)
        kp[WIN_A:WIN_A + seq, :] = k_ref[0].astype(BF16)
        vp[WIN_A:WIN_A + seq, :] = v_ref[0].astype(BF16)

    off = pl.multiple_of(m * tq, tq)
    feat = lax.broadcasted_iota(jnp.int32, (LANES, tq), 0)
    lo = feat < DH_A
    if not use_hist:
        key = lax.broadcasted_iota(jnp.int32, (win, 2 * tq), 0)
        in_seq = key >= (WIN_A - m * tq)
    slabs = [slice(hp * LANES, (hp + 1) * LANES) for hp in range(H_A // 2)]
    scores = [_dot(kp[pl.ds(off, win), cs], _split_maps_t(q_ref[0, :, cs])) for cs in slabs]
    for hp, cs in enumerate(slabs):
        s = scores[hp] + bias_ref[hp]
        if not use_hist:
            s = jnp.where(in_seq, s, NEG_INF)
        mx = jnp.max(s, axis=0, keepdims=True)
        p = jnp.exp2(s - mx)
        l = jnp.sum(p, axis=0, keepdims=True)
        o = _dot_tn(vp[pl.ds(off, win), cs], p.astype(BF16)) / l
        o_ref[0, :, cs] = jnp.where(lo, o[:, :tq], o[:, tq:]).T.astype(o_ref.dtype)


def _band_attn(q, k, v, bias, tq, hist=None):
    b, seq, da = q.shape
    win = WIN_A + tq
    use_hist = hist is not None
    kern = functools.partial(_band_kernel, tq=tq, seq=seq, use_hist=use_hist)
    full = pl.BlockSpec((1, seq, da), lambda i, m: (i, 0, 0))
    in_specs = [pl.BlockSpec((1, tq, da), lambda i, m: (i, m, 0)), full, full]
    args = [q, k, v]
    if use_hist:
        hspec = pl.BlockSpec((1, WIN_A, da), lambda i, m: (i, 0, 0))
        in_specs += [hspec, hspec]
        args += list(hist)
    in_specs.append(_const_spec(bias.shape))
    args.append(bias)
    return pl.pallas_call(
        kern,
        grid=(b, seq // tq),
        in_specs=in_specs,
        out_specs=pl.BlockSpec((1, tq, da), lambda i, m: (i, m, 0)),
        out_shape=jax.ShapeDtypeStruct((b, seq, da), BF16),
        scratch_shapes=[pltpu.VMEM((WIN_A + seq, da), BF16), pltpu.VMEM((WIN_A + seq, da), BF16)],
        compiler_params=_params(("parallel", "arbitrary")),
        name="band_attn",
    )(*args)


BIAS_LANE0 = WIN_A + REL_CLIP
BIAS_WIDTH = 768


def _band_bias_kernel(rb_ref, o_ref, *, tq, win):
    z = lax.broadcasted_iota(jnp.int32, (1, BIAS_WIDTH), 1)
    j = lax.broadcasted_iota(jnp.int32, (win, tq), 0)
    r = lax.broadcasted_iota(jnp.int32, (win, tq), 1)
    kc = j // CHUNK
    qc = r // CHUNK
    band = (kc >= qc) & (kc <= qc + BAND_CHUNKS)
    for h in range(H_A):
        row = jnp.where(z >= 2 * REL_CLIP, rb_ref[h:h + 1, 2 * REL_CLIP:2 * REL_CLIP + 1], rb_ref[h:h + 1, :])
        skew = pltpu.roll(jnp.broadcast_to(row, (win, BIAS_WIDTH)), 0, axis=1, stride=1, stride_axis=0)
        tbl = jnp.where(band, skew[:, BIAS_LANE0:BIAS_LANE0 + tq] * LOG2E, NEG_INF)
        o_ref[h // 2, :, (h % 2) * tq:(h % 2 + 1) * tq] = tbl


def _band_bias_table(rel_bias, tq):
    win = WIN_A + tq
    assert BIAS_LANE0 + tq <= BIAS_WIDTH and win <= BIAS_WIDTH
    rb = jnp.pad(rel_bias.astype(F32), ((0, 0), (0, BIAS_WIDTH - rel_bias.shape[1])))
    return pl.pallas_call(
        functools.partial(_band_bias_kernel, tq=tq, win=win),
        out_shape=jax.ShapeDtypeStruct((H_A // 2, win, 2 * tq), F32),
        compiler_params=pltpu.CompilerParams(vmem_limit_bytes=VMEM_LIMIT),
        name="band_bias",
    )(rb)


def _split_maps_t(q):
    qt = q.astype(F32).T
    feat = lax.broadcasted_iota(jnp.int32, qt.shape, 0)
    lo = feat < DH_B
    return jnp.concatenate([jnp.where(lo, qt, 0.0), jnp.where(lo, 0.0, qt)], axis=1).astype(BF16)


def _key_query_diff(tk, tq):
    c = lax.broadcasted_iota(jnp.int32, (tk, 2 * tq), 0)
    r = lax.broadcasted_iota(jnp.int32, (tk, 2 * tq), 1)
    rr = jnp.where(r >= tq, r - tq, r)
    return rr, c


def _alibi_slope(h):
    return 2.0 ** (-8.0 * (h + 1) / H_B) * LOG2E


def _online_update(s, vb, shift, carry):
    m, l, acc = carry
    m_new = jnp.maximum(m, jnp.max(s, axis=0, keepdims=True) + shift)
    alpha = jnp.exp2(m - m_new)
    p = jnp.exp2(s - (m_new - shift))
    l = alpha * l + jnp.sum(p, axis=0, keepdims=True)
    acc = alpha * acc + _dot_tn(vb, p.astype(BF16))
    return m_new, l, acc


def _diff_finish_t(carry, tq, lam, gsub, lam_init):
    _, l, acc = carry
    o = acc / l
    od = (o[:, :tq] - lam * o[:, tq:]).T
    return _rms(od, gsub) * (1.0 - lam_init)


def _diff_lambda(lq_ref, lk_ref, lam_init):
    e = jnp.exp(jnp.sum(lq_ref[...] * lk_ref[...], axis=-1, keepdims=True))
    return e[0:1] - e[1:2] + lam_init


def _diff_init_t(tq):
    return (jnp.full((1, 2 * tq), NEG_INF, F32), jnp.zeros((1, 2 * tq), F32),
            jnp.zeros((LANES, 2 * tq), F32))


def _diff_prompt_kernel(q_ref, k_ref, v_ref, lq_ref, lk_ref, gsub_ref, o_ref, kbf, vbf, *, tq, lam_init):
    qi = pl.program_id(1)

    @pl.when(qi == 0)
    def _():
        kbf[...] = k_ref[0].astype(BF16)
        vbf[...] = v_ref[0].astype(BF16)

    slabs = [slice(h * LANES, (h + 1) * LANES) for h in range(H_B)]
    qqts = [_split_maps_t(q_ref[0, :, cs]) for cs in slabs]
    rr, c = _key_query_diff(tq, tq)
    d = (rr - c).astype(F32)
    nsds = [-_alibi_slope(h) * d for h in range(H_B)]

    def block(j, biases, shifts, carries):
        off = pl.multiple_of(j * tq, tq)
        scores = [_dot(kbf[pl.ds(off, tq), cs], qqts[h]) for h, cs in enumerate(slabs)]
        return tuple(_online_update(scores[h] + biases[h], vbf[pl.ds(off, tq), cs], shifts[h], carries[h])
                     for h, cs in enumerate(slabs))

    def body(j, carries):
        ahead = ((qi - j) * tq).astype(F32)
        return block(j, nsds, [-_alibi_slope(h) * ahead for h in range(H_B)], carries)

    carries = lax.fori_loop(0, qi, body, tuple(_diff_init_t(tq) for _ in range(H_B)))
    allowed = (c // CHUNK) <= (rr // CHUNK)
    biases = [jnp.where(allowed, -_alibi_slope(h) * jnp.abs(d), NEG_INF) for h in range(H_B)]
    carries = block(qi, biases, [0.0] * H_B, carries)
    lam = _diff_lambda(lq_ref, lk_ref, lam_init)
    for h, cs in enumerate(slabs):
        o_ref[0, :, cs] = _diff_finish_t(carries[h], tq, lam, gsub_ref[...], lam_init).astype(o_ref.dtype)


def _diff_attn_prompt(q, k, v, lam_q, lam_k, g_sub, tq, lam_init):
    b, seq, db = q.shape
    kern = functools.partial(_diff_prompt_kernel, tq=tq, lam_init=lam_init)
    kv_spec = pl.BlockSpec((1, seq, db), lambda i, j: (i, 0, 0))
    q_spec = pl.BlockSpec((1, tq, db), lambda i, j: (i, j, 0))
    return pl.pallas_call(
        kern,
        grid=(b, seq // tq),
        in_specs=[q_spec, kv_spec, kv_spec,
                  _const_spec(lam_q.shape), _const_spec(lam_k.shape), _const_spec(g_sub.shape)],
        out_specs=q_spec,
        out_shape=jax.ShapeDtypeStruct((b, seq, db), BF16),
        scratch_shapes=[pltpu.VMEM((seq, db), BF16), pltpu.VMEM((seq, db), BF16)],
        compiler_params=_params(("parallel", "arbitrary")),
        name="diff_attn_prompt",
    )(q, k, v, lam_q, lam_k, g_sub)


def _diff_sample_kernel(q_ref, kc_ref, vc_ref, kn_ref, vn_ref, lq_ref, lk_ref, gsub_ref, o_ref,
                        m_s, l_s, acc_s, *, t, past, tk, lam_init):
    ci = pl.program_id(1)

    @pl.when(ci == 0)
    def _():
        for h in range(H_B):
            m_s[h], l_s[h], acc_s[h] = _diff_init_t(t)

    slabs = [slice(h * LANES, (h + 1) * LANES) for h in range(H_B)]
    qqts = [_split_maps_t(q_ref[0, :, cs]) for cs in slabs]
    rr, c = _key_query_diff(tk, t)
    d = (rr - c).astype(F32)
    ahead = (past - ci * tk).astype(F32)
    scores = [_dot(kc_ref[0, :, cs].astype(BF16), qqts[h]) for h, cs in enumerate(slabs)]
    for h, cs in enumerate(slabs):
        slope = _alibi_slope(h)
        m_s[h], l_s[h], acc_s[h] = _online_update(scores[h] - slope * d, vc_ref[0, :, cs].astype(BF16),
                                                  -slope * ahead, (m_s[h], l_s[h], acc_s[h]))

    @pl.when(ci == past // tk - 1)
    def _():
        rn, cn = _key_query_diff(t, t)
        dn = jnp.abs((rn - cn).astype(F32))
        lam = _diff_lambda(lq_ref, lk_ref, lam_init)
        for h, cs in enumerate(slabs):
            s = _dot(kn_ref[0, :, cs].astype(BF16), qqts[h]) - _alibi_slope(h) * dn
            carry = _online_update(s, vn_ref[0, :, cs].astype(BF16), 0.0, (m_s[h], l_s[h], acc_s[h]))
            o_ref[0, :, cs] = _diff_finish_t(carry, t, lam, gsub_ref[...], lam_init).astype(o_ref.dtype)


def _diff_attn_sample(q, kc, vc, kn, vn, lam_q, lam_k, g_sub, tk, lam_init):
    b, t, db = q.shape
    past = kc.shape[1]
    kern = functools.partial(_diff_sample_kernel, t=t, past=past, tk=tk, lam_init=lam_init)
    new_spec = pl.BlockSpec((1, t, db), lambda i, j: (i, 0, 0))
    cache_spec = pl.BlockSpec((1, tk, db), lambda i, j: (i, j, 0))
    return pl.pallas_call(
        kern,
        grid=(b, past // tk),
        in_specs=[new_spec, cache_spec, cache_spec, new_spec, new_spec,
                  _const_spec(lam_q.shape), _const_spec(lam_k.shape), _const_spec(g_sub.shape)],
        out_specs=new_spec,
        out_shape=jax.ShapeDtypeStruct((b, t, db), BF16),
        scratch_shapes=[pltpu.VMEM((H_B, 1, 2 * t), F32), pltpu.VMEM((H_B, 1, 2 * t), F32),
                        pltpu.VMEM((H_B, LANES, 2 * t), F32)],
        compiler_params=_params(("parallel", "arbitrary")),
        name="diff_attn_sample",
    )(q, kc, vc, kn, vn, lam_q, lam_k, g_sub)


def _mem_kernel(q_ref, mk_ref, mv_ref, o_ref):
    scale = DH_M ** -0.5
    slabs = [slice(h * LANES, (h + 1) * LANES) for h in range(H_M)]
    scores = [_dot_nt(q_ref[0, :, cs], mk_ref[0, :, cs].astype(BF16)) for cs in slabs]
    for h, cs in enumerate(slabs):
        s = scores[h] * scale
        mx = jnp.max(s, axis=-1, keepdims=True)
        p = jnp.exp(s - mx)
        l = jnp.sum(p, axis=-1, keepdims=True)
        o_ref[0, :, cs] = (_dot(p.astype(BF16), mv_ref[0, :, cs].astype(BF16)) / l).astype(o_ref.dtype)


def _mem_attn(q, mk, mv, tq):
    b, seq, dm = q.shape
    nm = mk.shape[1]
    kv_spec = pl.BlockSpec((1, nm, dm), lambda i, j: (i, 0, 0))
    q_spec = pl.BlockSpec((1, tq, dm), lambda i, j: (i, j, 0))
    return pl.pallas_call(
        _mem_kernel,
        grid=(b, seq // tq),
        in_specs=[q_spec, kv_spec, kv_spec],
        out_specs=q_spec,
        out_shape=jax.ShapeDtypeStruct((b, seq, dm), BF16),
        compiler_params=_params(("parallel", "parallel")),
        name="mem_attn",
    )(q, mk, mv)


def _merge_kernel(x_ref, oa_ref, ob_ref, om_ref, g_ref, wg_ref, bg_ref, wo_ref, wout_ref, o_ref, m_scr,
                  *, tn):
    x = x_ref[...]
    h = _rms(x, g_ref[...]).astype(BF16)
    o_refs = (oa_ref, ob_ref, om_ref)
    d = x.shape[1]
    for c0 in range(0, d, tn):
        cs = slice(c0, c0 + tn)
        m = None
        for br in range(N_BRANCH):
            gate = jax.nn.sigmoid(_dot(h, wg_ref[br, :, cs]) + bg_ref[br, :, cs])
            term = gate * _dot(o_refs[br][...], wo_ref[br, :, cs])
            m = term if m is None else m + term
        m_scr[:, cs] = m.astype(BF16)
    o_ref[...] = x + _dot(m_scr[...], wout_ref[...])


def _merge(x, oa, ob, om, g, wg, bg, wo, wout, tm, tn=256):
    n, d = x.shape
    kern = functools.partial(_merge_kernel, tn=tn)
    tok = lambda wd: pl.BlockSpec((tm, wd), lambda i: (i, 0))
    return pl.pallas_call(
        kern,
        grid=(n // tm,),
        in_specs=[tok(d), tok(oa.shape[1]), tok(ob.shape[1]), tok(om.shape[1]), _const_spec(g.shape),
                  _const_spec(wg.shape), _const_spec(bg.shape), _const_spec(wo.shape),
                  _const_spec(wout.shape)],
        out_specs=tok(d),
        out_shape=jax.ShapeDtypeStruct((n, d), F32),
        scratch_shapes=[pltpu.VMEM((tm, d), BF16)],
        compiler_params=_params(("parallel",)),
        name="merge",
    )(x, oa, ob, om, g, wg, bg, wo, wout)


def _ffn_kernel(x_ref, g_ref, prev_ref, wup_ref, wconv_ref, bconv_ref, wdown_ref, gfin_ref,
                y_ref, conv_ref, carry, u_scr, *, tm, nch, final_norm):
    @pl.when(pl.program_id(1) == 0)
    def _():
        carry[...] = prev_ref[0]

    x = x_ref[0]
    h = _rms(x, g_ref[...]).astype(BF16)
    row = lax.broadcasted_iota(jnp.int32, (tm, FF_CHUNK), 0)
    for ch in range(nch):
        cs = slice(ch * FF_CHUNK, (ch + 1) * FF_CHUNK)
        a = _dot(h, wup_ref[ch])
        gate = _dot(h, wup_ref[nch + ch])
        p2 = carry[0:1, cs]
        p1 = carry[1:2, cs]
        a1 = jnp.where(row == 0, p1, pltpu.roll(a, 1, axis=0))
        a2 = jnp.where(row == 0, p2, jnp.where(row == 1, p1, pltpu.roll(a, 2, axis=0)))
        conv = bconv_ref[:, cs] + a2 * wconv_ref[0:1, cs]
        conv = conv + a1 * wconv_ref[1:2, cs]
        conv = conv + a * wconv_ref[2:3, cs]
        carry[:, cs] = a[tm - (CONV_W - 1):, :]
        u_scr[:, cs] = (jax.nn.gelu(conv) * gate).astype(BF16)
    y = x + _dot(u_scr[...], wdown_ref[...])
    if final_norm:
        y = _rms(y, gfin_ref[...])
    y_ref[0] = y
    conv_ref[0] = carry[...]


def _conv_ffn(x, g, prev, wup, wconv, bconv, wdown, gfin, tm, final_norm):
    b, seq, d = x.shape
    ff = wdown.shape[0]
    nch = ff // FF_CHUNK
    kern = functools.partial(_ffn_kernel, tm=tm, nch=nch, final_norm=final_norm)
    tok = pl.BlockSpec((1, tm, d), lambda i, j: (i, j, 0))
    state = pl.BlockSpec((1, CONV_W - 1, ff), lambda i, j: (i, 0, 0))
    return pl.pallas_call(
        kern,
        grid=(b, seq // tm),
        in_specs=[tok, _const_spec(g.shape), state, _const_spec(wup.shape), _const_spec(wconv.shape),
                  _const_spec(bconv.shape), _const_spec(wdown.shape), _const_spec(gfin.shape)],
        out_specs=[tok, state],
        out_shape=[jax.ShapeDtypeStruct((b, seq, d), F32),
                   jax.ShapeDtypeStruct((b, CONV_W - 1, ff), F32)],
        scratch_shapes=[pltpu.VMEM((CONV_W - 1, ff), F32), pltpu.VMEM((tm, ff), BF16)],
        compiler_params=_params(("parallel", "arbitrary")),
        name="conv_ffn",
    )(x, g, prev, wup, wconv, bconv, wdown, gfin)


def _token_tile(n, pref):
    return pref if n % pref == 0 else n


def _layer(x, w, *, lam_init, final_norm, conv_prev, mem=None, cache=None):
    b, t, d = x.shape
    n = b * t
    tm = _token_tile(n, 512)
    d_a, d_b, d_m = H_A * DH_A, H_B * 2 * DH_B, H_M * DH_M
    cols = [(0, d_a), (d_a, d_a), (2 * d_a, d_a), (3 * d_a, d_b), (3 * d_a + d_b, d_b),
            (3 * d_a + 2 * d_b, d_b), (3 * d_a + 3 * d_b, d_m)]
    scales = [DH_A ** -0.5 * LOG2E, 1.0, 1.0, DH_B ** -0.5 * LOG2E, 1.0, 1.0, 1.0]
    dtypes = [BF16, F32, F32, BF16, F32, F32, BF16]
    qa, ka, va, qb, kb, vb, qm = _norm_proj(x.reshape(n, d), w["g_mix"], w["w_qkv"], cols, scales,
                                            dtypes, tm)
    seq3 = lambda a: a.reshape(b, t, a.shape[-1])
    qa, ka, va, qb, kb, vb, qm = map(seq3, (qa, ka, va, qb, kb, vb, qm))

    if cache is None:
        tq_a = 128
        oa = _band_attn(qa, ka, va, _band_bias_table(w["rel_bias"], tq_a), tq_a)
        ob = _diff_attn_prompt(qb, kb, vb, w["lam_q"], w["lam_k"], w["g_sub"], 256, lam_init)
        nm = mem.shape[1]
        mk, mv = _norm_proj(mem.reshape(b * nm, d), w["g_mem"], w["w_mem_kv"], [(0, d_m), (d_m, d_m)],
                            [1.0, 1.0], [F32, F32], _token_tile(b * nm, 512))
        mk = mk.reshape(b, nm, d_m)
        mv = mv.reshape(b, nm, d_m)
        om = _mem_attn(qm, mk, mv, _token_tile(t, 512))
    else:
        oa = _band_attn(qa, ka, va, _band_bias_table(w["rel_bias"], t), t,
                        hist=(cache["a_k"], cache["a_v"]))
        ob = _diff_attn_sample(qb, cache["b_k"], cache["b_v"], kb, vb,
                               w["lam_q"], w["lam_k"], w["g_sub"], 1024, lam_init)
        mk, mv = cache["m_k"], cache["m_v"]
        om = _mem_attn(qm, mk, mv, t)

    flat = lambda a: a.reshape(n, a.shape[-1])
    x1 = _merge(x.reshape(n, d), flat(oa), flat(ob), flat(om), w["g_mix"], w["w_gate"], w["b_gate"],
                w["w_o"], w["w_out"], tm)
    y, conv = _conv_ffn(x1.reshape(b, t, d), w["g_ffn"], conv_prev, w["w_up"], w["w_conv"], w["b_conv"],
                        w["w_down"], w["g_final"], _token_tile(t, 512), final_norm)
    return y, dict(ka=ka, va=va, kb=kb, vb=vb, mk=mk, mv=mv, conv=conv)


def kernel(x_prompt, x_sample, mem_prompt, cache_a_k, cache_a_v, cache_b_k, cache_b_v, cache_mem_k, cache_mem_v, state_ffn_conv, g_mix, w_in, b_gate, rel_bias, lam_q, lam_k, g_sub, g_mem, w_mem_kv, w_oa, w_ob, w_om, w_out, g_ffn, w_up, w_conv, b_conv, w_down, g_final):
    depth, d, _ = w_in.shape
    b, s, _ = x_prompt.shape
    bd, t, _ = x_sample.shape
    past = cache_b_k.shape[2]
    a_buf = cache_a_k.shape[2]
    ff = w_down.shape[1]
    d_a, d_b, d_m = H_A * DH_A, H_B * 2 * DH_B, H_M * DH_M
    n_qkv = 3 * d_a + 3 * d_b + d_m
    a_keep = min(WIN_A, s)
    assert a_buf == WIN_A and a_keep == WIN_A and t == CHUNK and past % CHUNK == 0 and ff % FF_CHUNK == 0
    nch = ff // FF_CHUNK

    xp, xs = x_prompt, x_sample
    outs = {k: [] for k in ("pa_k", "pa_v", "pb_k", "pb_v", "pm_k", "pm_v", "pconv",
                            "sa_k", "sa_v", "sb_k", "sb_v", "sconv")}
    for l in range(depth):
        lam_init = 0.8 - 0.6 * math.exp(-0.3 * l)
        w = dict(
            g_mix=g_mix[l][None], g_mem=g_mem[l][None], g_ffn=g_ffn[l][None], g_final=g_final[None],
            g_sub=g_sub[l][None], lam_q=lam_q[l], lam_k=lam_k[l], rel_bias=rel_bias[l],
            w_qkv=w_in[l][:, :n_qkv].astype(BF16),
            w_gate=w_in[l][:, n_qkv:].reshape(d, N_BRANCH, d).transpose(1, 0, 2).astype(BF16),
            b_gate=b_gate[l].reshape(N_BRANCH, 1, d),
            w_mem_kv=w_mem_kv[l].astype(BF16),
            w_o=jnp.stack([w_oa[l], w_ob[l], w_om[l]]).astype(BF16),
            w_out=w_out[l].astype(BF16),
            w_up=w_up[l].reshape(d, 2 * nch, FF_CHUNK).transpose(1, 0, 2).astype(BF16),
            w_conv=w_conv[l], b_conv=b_conv[l][None],
            w_down=w_down[l].astype(BF16),
        )
        last = l == depth - 1
        xp, po = _layer(xp, w, lam_init=lam_init, final_norm=last,
                        conv_prev=jnp.zeros((b, CONV_W - 1, ff), F32), mem=mem_prompt)
        cache = dict(a_k=cache_a_k[l].reshape(bd, a_buf, d_a), a_v=cache_a_v[l].reshape(bd, a_buf, d_a),
                     b_k=cache_b_k[l].reshape(bd, past, d_b), b_v=cache_b_v[l].reshape(bd, past, d_b),
                     m_k=cache_mem_k[l].reshape(bd, -1, d_m), m_v=cache_mem_v[l].reshape(bd, -1, d_m))
        xs, so = _layer(xs, w, lam_init=lam_init, final_norm=last, conv_prev=state_ffn_conv[l], cache=cache)

        outs["pa_k"].append(po["ka"][:, s - a_keep:].reshape(b, a_keep, H_A, DH_A))
        outs["pa_v"].append(po["va"][:, s - a_keep:].reshape(b, a_keep, H_A, DH_A))
        outs["pb_k"].append(po["kb"].reshape(b, s, H_B, 2 * DH_B))
        outs["pb_v"].append(po["vb"].reshape(b, s, H_B, 2 * DH_B))
        outs["pm_k"].append(po["mk"].reshape(b, -1, H_M, DH_M))
        outs["pm_v"].append(po["mv"].reshape(b, -1, H_M, DH_M))
        outs["pconv"].append(po["conv"])
        new_k = jnp.concatenate([cache["a_k"], so["ka"]], axis=1)[:, t:]
        new_v = jnp.concatenate([cache["a_v"], so["va"]], axis=1)[:, t:]
        outs["sa_k"].append(new_k.reshape(bd, a_buf, H_A, DH_A))
        outs["sa_v"].append(new_v.reshape(bd, a_buf, H_A, DH_A))
        outs["sb_k"].append(so["kb"].reshape(bd, t, H_B, 2 * DH_B))
        outs["sb_v"].append(so["vb"].reshape(bd, t, H_B, 2 * DH_B))
        outs["sconv"].append(so["conv"])

    st = {k: jnp.stack(v) for k, v in outs.items()}
    return (xp, xs, st["pa_k"], st["pa_v"], st["pb_k"], st["pb_v"], st["pm_k"], st["pm_v"], st["pconv"],
            st["sa_k"], st["sa_v"], st["sb_k"], st["sb_v"], st["sconv"])
```

```python
import functools
import math

import jax
import jax.numpy as jnp
from jax import lax
from jax.experimental import pallas as pl
from jax.experimental.pallas import tpu as pltpu

CHUNK = 64
BAND_CHUNKS = 8
WIN_A = BAND_CHUNKS * CHUNK
REL_CLIP = 128
H_A = 8
DH_A = 64
H_B = 4
DH_B = 64
H_M = 4
DH_M = 128
N_BRANCH = 3
CONV_W = 3
EPS = 1e-6
NEG_INF = -1e30

LANES = 128
FF_CHUNK = 256
VMEM_LIMIT = 56 * 1024 * 1024

LOG2E = math.log2(math.e)

BF16 = jnp.bfloat16
F32 = jnp.float32


def _rms(x, g):
    y = x * lax.rsqrt(jnp.mean(x * x, axis=-1, keepdims=True) + EPS)
    return y * g


def _dot(a, b):
    return jnp.dot(a, b, preferred_element_type=F32)


def _dot_nt(a, b):
    return lax.dot_general(a, b, (((1,), (1,)), ((), ())), preferred_element_type=F32)


def _dot_tn(a, b):
    return lax.dot_general(a, b, (((0,), (0,)), ((), ())), preferred_element_type=F32)


def _const_spec(shape):
    nd = len(shape)
    return pl.BlockSpec(shape, lambda *_: (0,) * nd, pipeline_mode=pl.Buffered(1))


def _params(sem):
    return pltpu.CompilerParams(dimension_semantics=sem, vmem_limit_bytes=VMEM_LIMIT)


def _head_rows(ref, h, n, heads, lead=()):
    return ref.at[lead + (pl.ds(h, n, stride=heads), slice(None))]


def _norm_proj_kernel(x_ref, g_ref, w_ref, *o_refs, cols, scales, by_head):
    tm = x_ref.shape[0]
    h = _rms(x_ref[...], g_ref[...]).astype(BF16)
    for o_ref, (c0, wd), sc, split in zip(o_refs, cols, scales, by_head):
        z = _dot(h, w_ref[:, c0:c0 + wd])
        if sc != 1.0:
            z = z * sc
        if split:
            heads = wd // LANES
            for hd in range(heads):
                _head_rows(o_ref, hd, tm, heads)[...] = z[:, hd * LANES:(hd + 1) * LANES].astype(o_ref.dtype)
        else:
            o_ref[...] = z.astype(o_ref.dtype)


def _norm_proj(x, g, w, cols, scales, dtypes, tm, by_head):
    n, d = x.shape
    kern = functools.partial(_norm_proj_kernel, cols=cols, scales=scales, by_head=by_head)
    shapes = [((wd // LANES) * tm, LANES) if split else (tm, wd) for (_, wd), split in zip(cols, by_head)]
    return pl.pallas_call(
        kern,
        grid=(n // tm,),
        in_specs=[
            pl.BlockSpec((tm, d), lambda i: (i, 0)),
            _const_spec((1, d)),
            _const_spec(w.shape),
        ],
        out_specs=[pl.BlockSpec(shp, lambda i: (i, 0)) for shp in shapes],
        out_shape=[jax.ShapeDtypeStruct((shp[0] * (n // tm), shp[1]), dt) for shp, dt in zip(shapes, dtypes)],
        compiler_params=_params(("parallel",)),
        name="norm_proj",
    )(x, g, w)


def _band_kernel(*refs, tq, seq, use_hist):
    if use_hist:
        q_ref, k_ref, v_ref, hk_ref, hv_ref, bias_ref, o_ref, kp, vp, s_scr = refs
    else:
        q_ref, k_ref, v_ref, bias_ref, o_ref, kp, vp, s_scr = refs
    m = pl.program_id(1)
    win = WIN_A + tq

    @pl.when(m == 0)
    def _():
        if use_hist:
            kp[0:WIN_A, :] = hk_ref[0].astype(BF16)
            vp[0:WIN_A, :] = hv_ref[0].astype(BF16)
        else:
            kp[0:WIN_A, :] = jnp.zeros((WIN_A, kp.shape[1]), BF16)
            vp[0:WIN_A, :] = jnp.zeros((WIN_A, vp.shape[1]), BF16)
        kp[WIN_A:WIN_A + seq, :] = k_ref[0].astype(BF16)
        vp[WIN_A:WIN_A + seq, :] = v_ref[0].astype(BF16)

    off = pl.multiple_of(m * tq, tq)
    feat = lax.broadcasted_iota(jnp.int32, (LANES, tq), 0)
    lo = feat < DH_A
    if not use_hist:
        key = lax.broadcasted_iota(jnp.int32, (win, 2 * tq), 0)
        in_seq = key >= (WIN_A - m * tq)
    slabs = [slice(hp * LANES, (hp + 1) * LANES) for hp in range(H_A // 2)]

    def stage_a(hp):
        s = _dot(kp[pl.ds(off, win), slabs[hp]], _split_maps_t(q_ref[0, :, slabs[hp]])) + bias_ref[hp]
        if not use_hist:
            s = jnp.where(in_seq, s, NEG_INF)
        s_scr[hp] = s
        return jnp.max(s, axis=0, keepdims=True)

    def stage_b(hp, mx):
        p = jnp.exp2(s_scr[hp] - mx)
        l = jnp.sum(p, axis=0, keepdims=True)
        o = _dot_tn(vp[pl.ds(off, win), slabs[hp]], p.astype(BF16)) / l
        o_ref[0, :, slabs[hp]] = jnp.where(lo, o[:, :tq], o[:, tq:]).T.astype(o_ref.dtype)

    mx = stage_a(0)
    for hp in range(1, H_A // 2):
        mx_next = stage_a(hp)
        stage_b(hp - 1, mx)
        mx = mx_next
    stage_b(H_A // 2 - 1, mx)


def _band_attn(q, k, v, bias, tq, hist=None):
    b, seq, da = q.shape
    win = WIN_A + tq
    use_hist = hist is not None
    kern = functools.partial(_band_kernel, tq=tq, seq=seq, use_hist=use_hist)
    full = pl.BlockSpec((1, seq, da), lambda i, m: (i, 0, 0))
    in_specs = [pl.BlockSpec((1, tq, da), lambda i, m: (i, m, 0)), full, full]
    args = [q, k, v]
    if use_hist:
        hspec = pl.BlockSpec((1, WIN_A, da), lambda i, m: (i, 0, 0))
        in_specs += [hspec, hspec]
        args += list(hist)
    in_specs.append(_const_spec(bias.shape))
    args.append(bias)
    return pl.pallas_call(
        kern,
        grid=(b, seq // tq),
        in_specs=in_specs,
        out_specs=pl.BlockSpec((1, tq, da), lambda i, m: (i, m, 0)),
        out_shape=jax.ShapeDtypeStruct((b, seq, da), BF16),
        scratch_shapes=[pltpu.VMEM((WIN_A + seq, da), BF16), pltpu.VMEM((WIN_A + seq, da), BF16),
                        pltpu.VMEM((H_A // 2, win, 2 * tq), F32)],
        compiler_params=_params(("parallel", "arbitrary")),
        name="band_attn",
    )(*args)


BIAS_LANE0 = WIN_A + REL_CLIP
BIAS_WIDTH = 768


def _band_bias_kernel(rb_ref, o_ref, *, tq, win):
    z = lax.broadcasted_iota(jnp.int32, (1, BIAS_WIDTH), 1)
    j = lax.broadcasted_iota(jnp.int32, (win, tq), 0)
    r = lax.broadcasted_iota(jnp.int32, (win, tq), 1)
    kc = j // CHUNK
    qc = r // CHUNK
    band = (kc >= qc) & (kc <= qc + BAND_CHUNKS)
    for h in range(H_A):
        row = jnp.where(z >= 2 * REL_CLIP, rb_ref[h:h + 1, 2 * REL_CLIP:2 * REL_CLIP + 1], rb_ref[h:h + 1, :])
        skew = pltpu.roll(jnp.broadcast_to(row, (win, BIAS_WIDTH)), 0, axis=1, stride=1, stride_axis=0)
        tbl = jnp.where(band, skew[:, BIAS_LANE0:BIAS_LANE0 + tq] * LOG2E, NEG_INF)
        o_ref[h // 2, :, (h % 2) * tq:(h % 2 + 1) * tq] = tbl


def _band_bias_table(rel_bias, tq):
    win = WIN_A + tq
    assert BIAS_LANE0 + tq <= BIAS_WIDTH and win <= BIAS_WIDTH
    rb = jnp.pad(rel_bias.astype(F32), ((0, 0), (0, BIAS_WIDTH - rel_bias.shape[1])))
    return pl.pallas_call(
        functools.partial(_band_bias_kernel, tq=tq, win=win),
        out_shape=jax.ShapeDtypeStruct((H_A // 2, win, 2 * tq), F32),
        compiler_params=pltpu.CompilerParams(vmem_limit_bytes=VMEM_LIMIT),
        name="band_bias",
    )(rb)


def _split_maps_t(q):
    qt = q.astype(F32).T
    feat = lax.broadcasted_iota(jnp.int32, qt.shape, 0)
    lo = feat < DH_B
    return jnp.concatenate([jnp.where(lo, qt, 0.0), jnp.where(lo, 0.0, qt)], axis=1).astype(BF16)


def _key_query_diff(tk, tq):
    c = lax.broadcasted_iota(jnp.int32, (tk, 2 * tq), 0)
    r = lax.broadcasted_iota(jnp.int32, (tk, 2 * tq), 1)
    rr = jnp.where(r >= tq, r - tq, r)
    return rr, c


def _alibi_slope(h):
    return 2.0 ** (-8.0 * (h + 1) / H_B) * LOG2E


def _online_update(s, vb, shift, carry):
    m, l, acc = carry
    m_new = jnp.maximum(m, jnp.max(s, axis=0, keepdims=True) + shift)
    alpha = jnp.exp2(m - m_new)
    p = jnp.exp2(s - (m_new - shift))
    l = alpha * l + jnp.sum(p, axis=0, keepdims=True)
    acc = alpha * acc + _dot_tn(vb, p.astype(BF16))
    return m_new, l, acc


def _diff_finish_t(carry, tq, lam, gsub, lam_init):
    _, l, acc = carry
    o = acc / l
    od = (o[:, :tq] - lam * o[:, tq:]).T
    return _rms(od, gsub) * (1.0 - lam_init)


def _diff_lambda(lq_ref, lk_ref, lam_init):
    e = jnp.exp(jnp.sum(lq_ref[...] * lk_ref[...], axis=-1, keepdims=True))
    return e[0:1] - e[1:2] + lam_init


def _diff_init_t(tq):
    return (jnp.full((1, 2 * tq), NEG_INF, F32), jnp.zeros((1, 2 * tq), F32),
            jnp.zeros((LANES, 2 * tq), F32))


def _diff_prompt_kernel(q_ref, k_ref, v_ref, lq_ref, lk_ref, gsub_ref, o_ref,
                        kbf, vbf, bias_scr, qq_scr, s0_scr, s1_scr, acc_scr, *, tq, lam_init):
    qi = pl.program_id(1)
    slabs = [slice(h * LANES, (h + 1) * LANES) for h in range(H_B)]

    @pl.when(qi == 0)
    def _():
        for h, cs in enumerate(slabs):
            kbf[:, cs] = _head_rows(k_ref, h, kbf.shape[0], H_B, lead=(0,))[...].astype(BF16)
            vbf[:, cs] = _head_rows(v_ref, h, vbf.shape[0], H_B, lead=(0,))[...].astype(BF16)
        rr, c = _key_query_diff(tq, tq)
        d = (rr - c).astype(F32)
        allowed = (c // CHUNK) <= (rr // CHUNK)
        for h in range(H_B):
            bias_scr[0, h] = -_alibi_slope(h) * d
            bias_scr[1, h] = jnp.where(allowed, -_alibi_slope(h) * jnp.abs(d), NEG_INF)

    for h, cs in enumerate(slabs):
        qq_scr[h] = _split_maps_t(q_ref[0, :, cs])
    acc_scr[...] = jnp.zeros(acc_scr.shape, F32)

    def stage_a(j, s_ref, ms):
        off = j * tq if isinstance(j, int) else pl.multiple_of(j * tq, tq)
        diag = (j == qi).astype(jnp.int32)
        out = []
        for h in range(H_B):
            s = _dot(kbf[pl.ds(off, tq), slabs[h]], qq_scr[h]) + bias_scr[diag, h]
            s_ref[h] = s
            shift = -_alibi_slope(h) * ((qi - j) * tq).astype(F32)
            m_new = jnp.maximum(ms[h], jnp.max(s, axis=0, keepdims=True) + shift)
            out.append((m_new, jnp.exp2(ms[h] - m_new), m_new - shift))
        return tuple(o[0] for o in out), tuple(o[1] for o in out), tuple(o[2] for o in out)

    def stage_b(j, s_ref, ls, alphas, subs):
        off = j * tq if isinstance(j, int) else pl.multiple_of(j * tq, tq)
        out = []
        for h in range(H_B):
            p = jnp.exp2(s_ref[h] - subs[h])
            acc_scr[h] = alphas[h] * acc_scr[h] + _dot_tn(vbf[pl.ds(off, tq), slabs[h]], p.astype(BF16))
            out.append(alphas[h] * ls[h] + jnp.sum(p, axis=0, keepdims=True))
        return tuple(out)

    def finish(ls):
        lam = _diff_lambda(lq_ref, lk_ref, lam_init)
        for h, cs in enumerate(slabs):
            o_ref[0, :, cs] = _diff_finish_t((None, ls[h], acc_scr[h]), tq, lam, gsub_ref[...],
                                             lam_init).astype(o_ref.dtype)

    row = lambda val: tuple(jnp.full((1, 2 * tq), val, F32) for _ in range(H_B))
    ms, alphas, subs = stage_a(0, s0_scr, row(NEG_INF))

    def two_blocks(jj, carry):
        ms, ls, alphas, subs = carry
        j = 2 * jj
        ms, alphas1, subs1 = stage_a(j + 1, s1_scr, ms)
        ls = stage_b(j, s0_scr, ls, alphas, subs)
        ms, alphas, subs = stage_a(j + 2, s0_scr, ms)
        ls = stage_b(j + 1, s1_scr, ls, alphas1, subs1)
        return ms, ls, alphas, subs

    ms, ls, alphas, subs = lax.fori_loop(0, qi // 2, two_blocks, (ms, row(0.0), alphas, subs))

    @pl.when(qi % 2 == 1)
    def _():
        _, alphas1, subs1 = stage_a(qi, s1_scr, ms)
        finish(stage_b(qi, s1_scr, stage_b(qi - 1, s0_scr, ls, alphas, subs), alphas1, subs1))

    @pl.when(qi % 2 == 0)
    def _():
        finish(stage_b(qi, s0_scr, ls, alphas, subs))


def _diff_attn_prompt(q, k, v, lam_q, lam_k, g_sub, tq, lam_init):
    b, seq, db = q.shape
    kern = functools.partial(_diff_prompt_kernel, tq=tq, lam_init=lam_init)
    kv_spec = pl.BlockSpec((1, seq * H_B, LANES), lambda i, j: (i, 0, 0))
    q_spec = pl.BlockSpec((1, tq, db), lambda i, j: (i, j, 0))
    return pl.pallas_call(
        kern,
        grid=(b, seq // tq),
        in_specs=[q_spec, kv_spec, kv_spec,
                  _const_spec(lam_q.shape), _const_spec(lam_k.shape), _const_spec(g_sub.shape)],
        out_specs=q_spec,
        out_shape=jax.ShapeDtypeStruct((b, seq, db), BF16),
        scratch_shapes=[pltpu.VMEM((seq, db), BF16), pltpu.VMEM((seq, db), BF16),
                        pltpu.VMEM((2, H_B, tq, 2 * tq), F32), pltpu.VMEM((H_B, LANES, 2 * tq), BF16),
                        pltpu.VMEM((H_B, tq, 2 * tq), F32), pltpu.VMEM((H_B, tq, 2 * tq), F32),
                        pltpu.VMEM((H_B, LANES, 2 * tq), F32)],
        compiler_params=_params(("parallel", "arbitrary")),
        name="diff_attn_prompt",
    )(q, k, v, lam_q, lam_k, g_sub)


def _diff_sample_kernel(q_ref, kc_ref, vc_ref, kn_ref, vn_ref, lq_ref, lk_ref, gsub_ref, o_ref,
                        m_s, l_s, acc_s, *, t, past, tk, lam_init):
    ci = pl.program_id(1)

    @pl.when(ci == 0)
    def _():
        for h in range(H_B):
            m_s[h], l_s[h], acc_s[h] = _diff_init_t(t)

    slabs = [slice(h * LANES, (h + 1) * LANES) for h in range(H_B)]
    qqts = [_split_maps_t(q_ref[0, :, cs]) for cs in slabs]
    rr, c = _key_query_diff(tk, t)
    d = (rr - c).astype(F32)
    ahead = (past - ci * tk).astype(F32)
    rows = lambda ref, h, n: _head_rows(ref, h, n, H_B, lead=(0,))[...].astype(BF16)
    scores = [_dot(rows(kc_ref, h, tk), qqts[h]) for h in range(H_B)]
    for h in range(H_B):
        slope = _alibi_slope(h)
        m_s[h], l_s[h], acc_s[h] = _online_update(scores[h] - slope * d, rows(vc_ref, h, tk),
                                                  -slope * ahead, (m_s[h], l_s[h], acc_s[h]))

    @pl.when(ci == past // tk - 1)
    def _():
        rn, cn = _key_query_diff(t, t)
        dn = jnp.abs((rn - cn).astype(F32))
        lam = _diff_lambda(lq_ref, lk_ref, lam_init)
        for h, cs in enumerate(slabs):
            s = _dot(rows(kn_ref, h, t), qqts[h]) - _alibi_slope(h) * dn
            carry = _online_update(s, rows(vn_ref, h, t), 0.0, (m_s[h], l_s[h], acc_s[h]))
            o_ref[0, :, cs] = _diff_finish_t(carry, t, lam, gsub_ref[...], lam_init).astype(o_ref.dtype)


def _diff_attn_sample(q, kc, vc, kn, vn, lam_q, lam_k, g_sub, tk, lam_init):
    b, t, db = q.shape
    past = kc.shape[1] // H_B
    kern = functools.partial(_diff_sample_kernel, t=t, past=past, tk=tk, lam_init=lam_init)
    new_spec = pl.BlockSpec((1, t, db), lambda i, j: (i, 0, 0))
    new_kv_spec = pl.BlockSpec((1, t * H_B, LANES), lambda i, j: (i, 0, 0))
    cache_spec = pl.BlockSpec((1, tk * H_B, LANES), lambda i, j: (i, j, 0))
    return pl.pallas_call(
        kern,
        grid=(b, past // tk),
        in_specs=[new_spec, cache_spec, cache_spec, new_kv_spec, new_kv_spec,
                  _const_spec(lam_q.shape), _const_spec(lam_k.shape), _const_spec(g_sub.shape)],
        out_specs=new_spec,
        out_shape=jax.ShapeDtypeStruct((b, t, db), BF16),
        scratch_shapes=[pltpu.VMEM((H_B, 1, 2 * t), F32), pltpu.VMEM((H_B, 1, 2 * t), F32),
                        pltpu.VMEM((H_B, LANES, 2 * t), F32)],
        compiler_params=_params(("parallel", "arbitrary")),
        name="diff_attn_sample",
    )(q, kc, vc, kn, vn, lam_q, lam_k, g_sub)


def _mem_kernel(q_ref, mk_ref, mv_ref, o_ref):
    scale = DH_M ** -0.5
    slabs = [slice(h * LANES, (h + 1) * LANES) for h in range(H_M)]
    nm = mk_ref.shape[1] // H_M
    rows = lambda ref, h: _head_rows(ref, h, nm, H_M, lead=(0,))[...].astype(BF16)
    scores = [_dot_nt(q_ref[0, :, cs], rows(mk_ref, h)) for h, cs in enumerate(slabs)]
    for h, cs in enumerate(slabs):
        s = scores[h] * scale
        mx = jnp.max(s, axis=-1, keepdims=True)
        p = jnp.exp(s - mx)
        l = jnp.sum(p, axis=-1, keepdims=True)
        o_ref[0, :, cs] = (_dot(p.astype(BF16), rows(mv_ref, h)) / l).astype(o_ref.dtype)


def _mem_attn(q, mk, mv, tq):
    b, seq, dm = q.shape
    kv_spec = pl.BlockSpec((1,) + mk.shape[1:], lambda i, j: (i, 0, 0))
    q_spec = pl.BlockSpec((1, tq, dm), lambda i, j: (i, j, 0))
    return pl.pallas_call(
        _mem_kernel,
        grid=(b, seq // tq),
        in_specs=[q_spec, kv_spec, kv_spec],
        out_specs=q_spec,
        out_shape=jax.ShapeDtypeStruct((b, seq, dm), BF16),
        compiler_params=_params(("parallel", "parallel")),
        name="mem_attn",
    )(q, mk, mv)


def _merge_kernel(x_ref, oa_ref, ob_ref, om_ref, g_ref, wg_ref, bg_ref, wo_ref, wout_ref, o_ref, m_scr,
                  *, tn):
    x = x_ref[...]
    h = _rms(x, g_ref[...]).astype(BF16)
    o_refs = (oa_ref, ob_ref, om_ref)
    d = x.shape[1]
    for c0 in range(0, d, tn):
        cs = slice(c0, c0 + tn)
        m = None
        for br in range(N_BRANCH):
            gate = jax.nn.sigmoid(_dot(h, wg_ref[br, :, cs]) + bg_ref[br, :, cs])
            term = gate * _dot(o_refs[br][...], wo_ref[br, :, cs])
            m = term if m is None else m + term
        m_scr[:, cs] = m.astype(BF16)
    o_ref[...] = x + _dot(m_scr[...], wout_ref[...])


def _merge(x, oa, ob, om, g, wg, bg, wo, wout, tm, tn=256):
    n, d = x.shape
    kern = functools.partial(_merge_kernel, tn=tn)
    tok = lambda wd: pl.BlockSpec((tm, wd), lambda i: (i, 0))
    return pl.pallas_call(
        kern,
        grid=(n // tm,),
        in_specs=[tok(d), tok(oa.shape[1]), tok(ob.shape[1]), tok(om.shape[1]), _const_spec(g.shape),
                  _const_spec(wg.shape), _const_spec(bg.shape), _const_spec(wo.shape),
                  _const_spec(wout.shape)],
        out_specs=tok(d),
        out_shape=jax.ShapeDtypeStruct((n, d), F32),
        scratch_shapes=[pltpu.VMEM((tm, d), BF16)],
        compiler_params=_params(("parallel",)),
        name="merge",
    )(x, oa, ob, om, g, wg, bg, wo, wout)


def _ffn_kernel(x_ref, g_ref, prev_ref, wup_ref, wconv_ref, bconv_ref, wdown_ref, gfin_ref,
                y_ref, conv_ref, carry, u_scr, *, tm, nch, final_norm):
    @pl.when(pl.program_id(1) == 0)
    def _():
        carry[...] = prev_ref[0]

    x = x_ref[0]
    h = _rms(x, g_ref[...]).astype(BF16)
    row = lax.broadcasted_iota(jnp.int32, (tm, FF_CHUNK), 0)
    for ch in range(nch):
        cs = slice(ch * FF_CHUNK, (ch + 1) * FF_CHUNK)
        a = _dot(h, wup_ref[ch])
        gate = _dot(h, wup_ref[nch + ch])
        p2 = carry[0:1, cs]
        p1 = carry[1:2, cs]
        a1 = jnp.where(row == 0, p1, pltpu.roll(a, 1, axis=0))
        a2 = jnp.where(row == 0, p2, jnp.where(row == 1, p1, pltpu.roll(a, 2, axis=0)))
        conv = bconv_ref[:, cs] + a2 * wconv_ref[0:1, cs]
        conv = conv + a1 * wconv_ref[1:2, cs]
        conv = conv + a * wconv_ref[2:3, cs]
        carry[:, cs] = a[tm - (CONV_W - 1):, :]
        u_scr[:, cs] = (jax.nn.gelu(conv) * gate).astype(BF16)
    y = x + _dot(u_scr[...], wdown_ref[...])
    if final_norm:
        y = _rms(y, gfin_ref[...])
    y_ref[0] = y
    conv_ref[0] = carry[...]


def _conv_ffn(x, g, prev, wup, wconv, bconv, wdown, gfin, tm, final_norm):
    b, seq, d = x.shape
    ff = wdown.shape[0]
    nch = ff // FF_CHUNK
    kern = functools.partial(_ffn_kernel, tm=tm, nch=nch, final_norm=final_norm)
    tok = pl.BlockSpec((1, tm, d), lambda i, j: (i, j, 0))
    state = pl.BlockSpec((1, CONV_W - 1, ff), lambda i, j: (i, 0, 0))
    return pl.pallas_call(
        kern,
        grid=(b, seq // tm),
        in_specs=[tok, _const_spec(g.shape), state, _const_spec(wup.shape), _const_spec(wconv.shape),
                  _const_spec(bconv.shape), _const_spec(wdown.shape), _const_spec(gfin.shape)],
        out_specs=[tok, state],
        out_shape=[jax.ShapeDtypeStruct((b, seq, d), F32),
                   jax.ShapeDtypeStruct((b, CONV_W - 1, ff), F32)],
        scratch_shapes=[pltpu.VMEM((CONV_W - 1, ff), F32), pltpu.VMEM((tm, ff), BF16)],
        compiler_params=_params(("parallel", "arbitrary")),
        name="conv_ffn",
    )(x, g, prev, wup, wconv, bconv, wdown, gfin)


def _token_tile(n, pref):
    return pref if n % pref == 0 else n


def _layer(x, w, *, lam_init, final_norm, conv_prev, mem=None, cache=None):
    b, t, d = x.shape
    n = b * t
    tm = _token_tile(n, 512)
    d_a, d_b, d_m = H_A * DH_A, H_B * 2 * DH_B, H_M * DH_M
    cols = [(0, d_a), (d_a, d_a), (2 * d_a, d_a), (3 * d_a, d_b), (3 * d_a + d_b, d_b),
            (3 * d_a + 2 * d_b, d_b), (3 * d_a + 3 * d_b, d_m)]
    scales = [DH_A ** -0.5 * LOG2E, 1.0, 1.0, DH_B ** -0.5 * LOG2E, 1.0, 1.0, 1.0]
    dtypes = [BF16, F32, F32, BF16, F32, F32, BF16]
    by_head = [False, False, False, False, True, True, False]
    qa, ka, va, qb, kb, vb, qm = _norm_proj(x.reshape(n, d), w["g_mix"], w["w_qkv"], cols, scales,
                                            dtypes, tm, by_head)
    seq3 = lambda a: a.reshape(b, -1, a.shape[-1])
    qa, ka, va, qb, kb, vb, qm = map(seq3, (qa, ka, va, qb, kb, vb, qm))

    if cache is None:
        tq_a = 128
        oa = _band_attn(qa, ka, va, _band_bias_table(w["rel_bias"], tq_a), tq_a)
        ob = _diff_attn_prompt(qb, kb, vb, w["lam_q"], w["lam_k"], w["g_sub"], 256, lam_init)
        nm = mem.shape[1]
        mk, mv = _norm_proj(mem.reshape(b * nm, d), w["g_mem"], w["w_mem_kv"], [(0, d_m), (d_m, d_m)],
                            [1.0, 1.0], [F32, F32], _token_tile(b * nm, 512), [True, True])
        mk = mk.reshape(b, nm * H_M, DH_M)
        mv = mv.reshape(b, nm * H_M, DH_M)
        om = _mem_attn(qm, mk, mv, _token_tile(t, 512))
    else:
        oa = _band_attn(qa, ka, va, _band_bias_table(w["rel_bias"], t), t,
                        hist=(cache["a_k"], cache["a_v"]))
        ob = _diff_attn_sample(qb, cache["b_k"], cache["b_v"], kb, vb,
                               w["lam_q"], w["lam_k"], w["g_sub"], 1024, lam_init)
        mk, mv = cache["m_k"], cache["m_v"]
        om = _mem_attn(qm, mk, mv, t)

    flat = lambda a: a.reshape(n, a.shape[-1])
    x1 = _merge(x.reshape(n, d), flat(oa), flat(ob), flat(om), w["g_mix"], w["w_gate"], w["b_gate"],
                w["w_o"], w["w_out"], tm)
    y, conv = _conv_ffn(x1.reshape(b, t, d), w["g_ffn"], conv_prev, w["w_up"], w["w_conv"], w["b_conv"],
                        w["w_down"], w["g_final"], _token_tile(t, 512), final_norm)
    return y, dict(ka=ka, va=va, kb=kb, vb=vb, mk=mk, mv=mv, conv=conv)


def kernel(x_prompt, x_sample, mem_prompt, cache_a_k, cache_a_v, cache_b_k, cache_b_v, cache_mem_k, cache_mem_v, state_ffn_conv, g_mix, w_in, b_gate, rel_bias, lam_q, lam_k, g_sub, g_mem, w_mem_kv, w_oa, w_ob, w_om, w_out, g_ffn, w_up, w_conv, b_conv, w_down, g_final):
    depth, d, _ = w_in.shape
    b, s, _ = x_prompt.shape
    bd, t, _ = x_sample.shape
    past = cache_b_k.shape[2]
    a_buf = cache_a_k.shape[2]
    ff = w_down.shape[1]
    d_a, d_b, d_m = H_A * DH_A, H_B * 2 * DH_B, H_M * DH_M
    n_qkv = 3 * d_a + 3 * d_b + d_m
    a_keep = min(WIN_A, s)
    assert a_buf == WIN_A and a_keep == WIN_A and t == CHUNK and past % CHUNK == 0 and ff % FF_CHUNK == 0
    nch = ff // FF_CHUNK

    xp, xs = x_prompt, x_sample
    outs = {k: [] for k in ("pa_k", "pa_v", "pb_k", "pb_v", "pm_k", "pm_v", "pconv",
                            "sa_k", "sa_v", "sb_k", "sb_v", "sconv")}
    for l in range(depth):
        lam_init = 0.8 - 0.6 * math.exp(-0.3 * l)
        w = dict(
            g_mix=g_mix[l][None], g_mem=g_mem[l][None], g_ffn=g_ffn[l][None], g_final=g_final[None],
            g_sub=g_sub[l][None], lam_q=lam_q[l], lam_k=lam_k[l], rel_bias=rel_bias[l],
            w_qkv=w_in[l][:, :n_qkv].astype(BF16),
            w_gate=w_in[l][:, n_qkv:].reshape(d, N_BRANCH, d).transpose(1, 0, 2).astype(BF16),
            b_gate=b_gate[l].reshape(N_BRANCH, 1, d),
            w_mem_kv=w_mem_kv[l].astype(BF16),
            w_o=jnp.stack([w_oa[l], w_ob[l], w_om[l]]).astype(BF16),
            w_out=w_out[l].astype(BF16),
            w_up=w_up[l].reshape(d, 2 * nch, FF_CHUNK).transpose(1, 0, 2).astype(BF16),
            w_conv=w_conv[l], b_conv=b_conv[l][None],
            w_down=w_down[l].astype(BF16),
        )
        last = l == depth - 1
        xp, po = _layer(xp, w, lam_init=lam_init, final_norm=last,
                        conv_prev=jnp.zeros((b, CONV_W - 1, ff), F32), mem=mem_prompt)
        cache = dict(a_k=cache_a_k[l].reshape(bd, a_buf, d_a), a_v=cache_a_v[l].reshape(bd, a_buf, d_a),
                     b_k=cache_b_k[l].reshape(bd, past * H_B, 2 * DH_B),
                     b_v=cache_b_v[l].reshape(bd, past * H_B, 2 * DH_B),
                     m_k=cache_mem_k[l].reshape(bd, -1, DH_M), m_v=cache_mem_v[l].reshape(bd, -1, DH_M))
        xs, so = _layer(xs, w, lam_init=lam_init, final_norm=last, conv_prev=state_ffn_conv[l], cache=cache)

        outs["pa_k"].append(po["ka"][:, s - a_keep:].reshape(b, a_keep, H_A, DH_A))
        outs["pa_v"].append(po["va"][:, s - a_keep:].reshape(b, a_keep, H_A, DH_A))
        outs["pb_k"].append(po["kb"].reshape(b, s, H_B, 2 * DH_B))
        outs["pb_v"].append(po["vb"].reshape(b, s, H_B, 2 * DH_B))
        outs["pm_k"].append(po["mk"].reshape(b, -1, H_M, DH_M))
        outs["pm_v"].append(po["mv"].reshape(b, -1, H_M, DH_M))
        outs["pconv"].append(po["conv"])
        new_k = jnp.concatenate([cache["a_k"], so["ka"]], axis=1)[:, t:]
        new_v = jnp.concatenate([cache["a_v"], so["va"]], axis=1)[:, t:]
        outs["sa_k"].append(new_k.reshape(bd, a_buf, H_A, DH_A))
        outs["sa_v"].append(new_v.reshape(bd, a_buf, H_A, DH_A))
        outs["sb_k"].append(so["kb"].reshape(bd, t, H_B, 2 * DH_B))
        outs["sb_v"].append(so["vb"].reshape(bd, t, H_B, 2 * DH_B))
        outs["sconv"].append(so["conv"])

    st = {k: jnp.stack(v) for k, v in outs.items()}
    return (xp, xs, st["pa_k"], st["pa_v"], st["pb_k"], st["pb_v"], st["pm_k"], st["pm_v"], st["pconv"],
            st["sa_k"], st["sa_v"], st["sb_k"], st["sb_v"], st["sconv"])
```

```python
import functools
import math

import jax
import jax.numpy as jnp
from jax import lax
from jax.experimental import pallas as pl
from jax.experimental.pallas import tpu as pltpu

CHUNK = 64
BAND_CHUNKS = 8
WIN_A = BAND_CHUNKS * CHUNK
REL_CLIP = 128
H_A = 8
DH_A = 64
H_B = 4
DH_B = 64
H_M = 4
DH_M = 128
N_BRANCH = 3
CONV_W = 3
EPS = 1e-6
NEG_INF = -1e30

LANES = 128
FF_CHUNK = 256
VMEM_LIMIT = 56 * 1024 * 1024

LOG2E = math.log2(math.e)

BF16 = jnp.bfloat16
F32 = jnp.float32


def _rms(x, g):
    y = x * lax.rsqrt(jnp.mean(x * x, axis=-1, keepdims=True) + EPS)
    return y * g


def _dot(a, b):
    return jnp.dot(a, b, preferred_element_type=F32)


def _dot_nt(a, b):
    return lax.dot_general(a, b, (((1,), (1,)), ((), ())), preferred_element_type=F32)


def _dot_tn(a, b):
    return lax.dot_general(a, b, (((0,), (0,)), ((), ())), preferred_element_type=F32)


def _const_spec(shape):
    nd = len(shape)
    return pl.BlockSpec(shape, lambda *_: (0,) * nd, pipeline_mode=pl.Buffered(1))


def _params(sem):
    return pltpu.CompilerParams(dimension_semantics=sem, vmem_limit_bytes=VMEM_LIMIT)


def _head_rows(ref, h, n, heads, lead=()):
    return ref.at[lead + (pl.ds(h, n, stride=heads), slice(None))]


def _norm_proj_kernel(x_ref, g_ref, w_ref, *o_refs, outs):
    tm = x_ref.shape[0]
    h = _rms(x_ref[...], g_ref[...]).astype(BF16)
    z_of = {}
    for o_ref, (c0, wd, sc, _, mode) in zip(o_refs, outs):
        if (c0, wd) not in z_of:
            z_of[(c0, wd)] = _dot(h, w_ref[:, c0:c0 + wd])
        z = z_of[(c0, wd)]
        if sc != 1.0:
            z = z * sc
        if mode == "by_head":
            heads = wd // LANES
            for hd in range(heads):
                _head_rows(o_ref, hd, tm, heads)[...] = z[:, hd * LANES:(hd + 1) * LANES].astype(o_ref.dtype)
        else:
            o_ref[...] = z.astype(o_ref.dtype)


def _norm_proj(x, g, w, outs, tm, tiles_per_seq=1):
    n, d = x.shape
    kern = functools.partial(_norm_proj_kernel, outs=outs)
    specs, shapes = [], []
    for _, wd, _, dt, mode in outs:
        if mode == "by_head":
            blk, rows, imap = ((wd // LANES) * tm, LANES), (wd // LANES) * n, lambda i: (i, 0)
        elif mode == "last":
            blk, rows, imap = (tm, wd), n // tiles_per_seq, lambda i: (i // tiles_per_seq, 0)
        else:
            blk, rows, imap = (tm, wd), n, lambda i: (i, 0)
        specs.append(pl.BlockSpec(blk, imap))
        shapes.append(jax.ShapeDtypeStruct((rows, blk[1]), dt))
    return pl.pallas_call(
        kern,
        grid=(n // tm,),
        in_specs=[
            pl.BlockSpec((tm, d), lambda i: (i, 0)),
            _const_spec((1, d)),
            _const_spec(w.shape),
        ],
        out_specs=specs,
        out_shape=shapes,
        compiler_params=_params(("arbitrary",)),
        name="norm_proj",
    )(x, g, w)


def _band_kernel(*refs, tq, seq, use_hist):
    if use_hist:
        q_ref, k_ref, v_ref, hk_ref, hv_ref, bias_ref, o_ref, kp, vp, s_scr = refs
    else:
        q_ref, k_ref, v_ref, bias_ref, o_ref, kp, vp, s_scr = refs
    m = pl.program_id(1)
    win = WIN_A + tq

    @pl.when(m == 0)
    def _():
        if use_hist:
            kp[0:WIN_A, :] = hk_ref[0].astype(BF16)
            vp[0:WIN_A, :] = hv_ref[0].astype(BF16)
        else:
            kp[0:WIN_A, :] = jnp.zeros((WIN_A, kp.shape[1]), BF16)
            vp[0:WIN_A, :] = jnp.zeros((WIN_A, vp.shape[1]), BF16)
        kp[WIN_A:WIN_A + seq, :] = k_ref[0].astype(BF16)
        vp[WIN_A:WIN_A + seq, :] = v_ref[0].astype(BF16)

    off = pl.multiple_of(m * tq, tq)
    feat = lax.broadcasted_iota(jnp.int32, (LANES, tq), 0)
    lo = feat < DH_A
    slabs = [slice(hp * LANES, (hp + 1) * LANES) for hp in range(H_A // 2)]

    def run(mask_history):
        if mask_history:
            key = lax.broadcasted_iota(jnp.int32, (win, 2 * tq), 0)
            in_seq = key >= (WIN_A - m * tq)

        def stage_a(hp):
            s = _dot(kp[pl.ds(off, win), slabs[hp]], _split_maps_t(q_ref[0, :, slabs[hp]])) + bias_ref[hp]
            if mask_history:
                s = jnp.where(in_seq, s, NEG_INF)
            s_scr[hp] = s
            return jnp.max(s, axis=0, keepdims=True)

        def stage_b(hp, mx):
            p = jnp.exp2(s_scr[hp] - mx)
            l = jnp.sum(p, axis=0, keepdims=True)
            o = _dot_tn(vp[pl.ds(off, win), slabs[hp]], p.astype(BF16)) / l
            o_ref[0, :, slabs[hp]] = jnp.where(lo, o[:, :tq], o[:, tq:]).T.astype(o_ref.dtype)

        mx = stage_a(0)
        for hp in range(1, H_A // 2):
            mx_next = stage_a(hp)
            stage_b(hp - 1, mx)
            mx = mx_next
        stage_b(H_A // 2 - 1, mx)

    if use_hist:
        run(False)
    else:
        pl.when(m * tq < WIN_A)(lambda: run(True))
        pl.when(m * tq >= WIN_A)(lambda: run(False))


def _band_attn(q, k, v, bias, tq, hist=None):
    b, seq, da = q.shape
    win = WIN_A + tq
    use_hist = hist is not None
    kern = functools.partial(_band_kernel, tq=tq, seq=seq, use_hist=use_hist)
    full = pl.BlockSpec((1, seq, da), lambda i, m: (i, 0, 0))
    in_specs = [pl.BlockSpec((1, tq, da), lambda i, m: (i, m, 0)), full, full]
    args = [q, k, v]
    if use_hist:
        hspec = pl.BlockSpec((1, WIN_A, da), lambda i, m: (i, 0, 0))
        in_specs += [hspec, hspec]
        args += list(hist)
    in_specs.append(_const_spec(bias.shape))
    args.append(bias)
    return pl.pallas_call(
        kern,
        grid=(b, seq // tq),
        in_specs=in_specs,
        out_specs=pl.BlockSpec((1, tq, da), lambda i, m: (i, m, 0)),
        out_shape=jax.ShapeDtypeStruct((b, seq, da), BF16),
        scratch_shapes=[pltpu.VMEM((WIN_A + seq, da), BF16), pltpu.VMEM((WIN_A + seq, da), BF16),
                        pltpu.VMEM((H_A // 2, win, 2 * tq), F32)],
        compiler_params=_params(("parallel", "arbitrary")),
        name="band_attn",
    )(*args)


BIAS_LANE0 = WIN_A + REL_CLIP
BIAS_WIDTH = 768


def _band_bias_kernel(rb_ref, o_ref, *, tq, win):
    z = lax.broadcasted_iota(jnp.int32, (1, BIAS_WIDTH), 1)
    j = lax.broadcasted_iota(jnp.int32, (win, tq), 0)
    r = lax.broadcasted_iota(jnp.int32, (win, tq), 1)
    kc = j // CHUNK
    qc = r // CHUNK
    band = (kc >= qc) & (kc <= qc + BAND_CHUNKS)
    for h in range(H_A):
        row = jnp.where(z >= 2 * REL_CLIP, rb_ref[h:h + 1, 2 * REL_CLIP:2 * REL_CLIP + 1], rb_ref[h:h + 1, :])
        skew = pltpu.roll(jnp.broadcast_to(row, (win, BIAS_WIDTH)), 0, axis=1, stride=1, stride_axis=0)
        tbl = jnp.where(band, skew[:, BIAS_LANE0:BIAS_LANE0 + tq] * LOG2E, NEG_INF)
        o_ref[h // 2, :, (h % 2) * tq:(h % 2 + 1) * tq] = tbl


def _band_bias_table(rel_bias, tq):
    win = WIN_A + tq
    assert BIAS_LANE0 + tq <= BIAS_WIDTH and win <= BIAS_WIDTH
    rb = jnp.pad(rel_bias.astype(F32), ((0, 0), (0, BIAS_WIDTH - rel_bias.shape[1])))
    return pl.pallas_call(
        functools.partial(_band_bias_kernel, tq=tq, win=win),
        out_shape=jax.ShapeDtypeStruct((H_A // 2, win, 2 * tq), F32),
        compiler_params=pltpu.CompilerParams(vmem_limit_bytes=VMEM_LIMIT),
        name="band_bias",
    )(rb)


def _split_maps_t(q):
    qt = q.astype(F32).T
    feat = lax.broadcasted_iota(jnp.int32, qt.shape, 0)
    lo = feat < DH_B
    return jnp.concatenate([jnp.where(lo, qt, 0.0), jnp.where(lo, 0.0, qt)], axis=1).astype(BF16)


def _key_query_diff(tk, tq):
    c = lax.broadcasted_iota(jnp.int32, (tk, 2 * tq), 0)
    r = lax.broadcasted_iota(jnp.int32, (tk, 2 * tq), 1)
    rr = jnp.where(r >= tq, r - tq, r)
    return rr, c


def _alibi_slope(h):
    return 2.0 ** (-8.0 * (h + 1) / H_B) * LOG2E


def _online_update(s, vb, shift, carry):
    m, l, acc = carry
    m_new = jnp.maximum(m, jnp.max(s, axis=0, keepdims=True) + shift)
    alpha = jnp.exp2(m - m_new)
    p = jnp.exp2(s - (m_new - shift))
    l = alpha * l + jnp.sum(p, axis=0, keepdims=True)
    acc = alpha * acc + _dot_tn(vb, p.astype(BF16))
    return m_new, l, acc


def _diff_finish_t(carry, tq, lam, gsub, lam_init):
    _, l, acc = carry
    o = acc / l
    od = (o[:, :tq] - lam * o[:, tq:]).T
    return _rms(od, gsub) * (1.0 - lam_init)


def _diff_lambda(lq_ref, lk_ref, lam_init):
    e = jnp.exp(jnp.sum(lq_ref[...] * lk_ref[...], axis=-1, keepdims=True))
    return e[0:1] - e[1:2] + lam_init


def _diff_init_t(tq):
    return (jnp.full((1, 2 * tq), NEG_INF, F32), jnp.zeros((1, 2 * tq), F32),
            jnp.zeros((LANES, 2 * tq), F32))


def _diff_prompt_kernel(q_ref, k_ref, v_ref, lq_ref, lk_ref, gsub_ref, o_ref,
                        kbf, vbf, bias_scr, qq_scr, s0_scr, s1_scr, acc_scr, *, tq, lam_init):
    qi = pl.program_id(1)
    slabs = [slice(h * LANES, (h + 1) * LANES) for h in range(H_B)]

    @pl.when(qi == 0)
    def _():
        for h, cs in enumerate(slabs):
            kbf[:, cs] = _head_rows(k_ref, h, kbf.shape[0], H_B, lead=(0,))[...].astype(BF16)
            vbf[:, cs] = _head_rows(v_ref, h, vbf.shape[0], H_B, lead=(0,))[...].astype(BF16)
        rr, c = _key_query_diff(tq, tq)
        d = (rr - c).astype(F32)
        allowed = (c // CHUNK) <= (rr // CHUNK)
        for h in range(H_B):
            bias_scr[0, h] = -_alibi_slope(h) * d
            bias_scr[1, h] = jnp.where(allowed, -_alibi_slope(h) * jnp.abs(d), NEG_INF)

    for h, cs in enumerate(slabs):
        qq_scr[h] = _split_maps_t(q_ref[0, :, cs])
    acc_scr[...] = jnp.zeros(acc_scr.shape, F32)

    def stage_a(j, s_ref, ms):
        off = j * tq if isinstance(j, int) else pl.multiple_of(j * tq, tq)
        diag = (j == qi).astype(jnp.int32)
        out = []
        for h in range(H_B):
            s = _dot(kbf[pl.ds(off, tq), slabs[h]], qq_scr[h]) + bias_scr[diag, h]
            s_ref[h] = s
            shift = -_alibi_slope(h) * ((qi - j) * tq).astype(F32)
            m_new = jnp.maximum(ms[h], jnp.max(s, axis=0, keepdims=True) + shift)
            out.append((m_new, jnp.exp2(ms[h] - m_new), m_new - shift))
        return tuple(o[0] for o in out), tuple(o[1] for o in out), tuple(o[2] for o in out)

    def stage_b(j, s_ref, ls, alphas, subs):
        off = j * tq if isinstance(j, int) else pl.multiple_of(j * tq, tq)
        out = []
        for h in range(H_B):
            p = jnp.exp2(s_ref[h] - subs[h])
            acc_scr[h] = alphas[h] * acc_scr[h] + _dot_tn(vbf[pl.ds(off, tq), slabs[h]], p.astype(BF16))
            out.append(alphas[h] * ls[h] + jnp.sum(p, axis=0, keepdims=True))
        return tuple(out)

    def finish(ls):
        lam = _diff_lambda(lq_ref, lk_ref, lam_init)
        for h, cs in enumerate(slabs):
            o_ref[0, :, cs] = _diff_finish_t((None, ls[h], acc_scr[h]), tq, lam, gsub_ref[...],
                                             lam_init).astype(o_ref.dtype)

    row = lambda val: tuple(jnp.full((1, 2 * tq), val, F32) for _ in range(H_B))
    ms, alphas, subs = stage_a(0, s0_scr, row(NEG_INF))

    def two_blocks(jj, carry):
        ms, ls, alphas, subs = carry
        j = 2 * jj
        ms, alphas1, subs1 = stage_a(j + 1, s1_scr, ms)
        ls = stage_b(j, s0_scr, ls, alphas, subs)
        ms, alphas, subs = stage_a(j + 2, s0_scr, ms)
        ls = stage_b(j + 1, s1_scr, ls, alphas1, subs1)
        return ms, ls, alphas, subs

    ms, ls, alphas, subs = lax.fori_loop(0, qi // 2, two_blocks, (ms, row(0.0), alphas, subs))

    @pl.when(qi % 2 == 1)
    def _():
        _, alphas1, subs1 = stage_a(qi, s1_scr, ms)
        finish(stage_b(qi, s1_scr, stage_b(qi - 1, s0_scr, ls, alphas, subs), alphas1, subs1))

    @pl.when(qi % 2 == 0)
    def _():
        finish(stage_b(qi, s0_scr, ls, alphas, subs))


def _diff_attn_prompt(q, k, v, lam_q, lam_k, g_sub, tq, lam_init):
    b, seq, db = q.shape
    kern = functools.partial(_diff_prompt_kernel, tq=tq, lam_init=lam_init)
    kv_spec = pl.BlockSpec((1, seq * H_B, LANES), lambda i, j: (i, 0, 0))
    q_spec = pl.BlockSpec((1, tq, db), lambda i, j: (i, j, 0))
    return pl.pallas_call(
        kern,
        grid=(b, seq // tq),
        in_specs=[q_spec, kv_spec, kv_spec,
                  _const_spec(lam_q.shape), _const_spec(lam_k.shape), _const_spec(g_sub.shape)],
        out_specs=q_spec,
        out_shape=jax.ShapeDtypeStruct((b, seq, db), BF16),
        scratch_shapes=[pltpu.VMEM((seq, db), BF16), pltpu.VMEM((seq, db), BF16),
                        pltpu.VMEM((2, H_B, tq, 2 * tq), F32), pltpu.VMEM((H_B, LANES, 2 * tq), BF16),
                        pltpu.VMEM((H_B, tq, 2 * tq), F32), pltpu.VMEM((H_B, tq, 2 * tq), F32),
                        pltpu.VMEM((H_B, LANES, 2 * tq), F32)],
        compiler_params=_params(("parallel", "arbitrary")),
        name="diff_attn_prompt",
    )(q, k, v, lam_q, lam_k, g_sub)


def _diff_sample_kernel(q_ref, kc_ref, vc_ref, kn_ref, vn_ref, lq_ref, lk_ref, gsub_ref, o_ref,
                        m_s, l_s, acc_s, s_scr, *, t, past, tk, lam_init):
    ci = pl.program_id(1)

    @pl.when(ci == 0)
    def _():
        for h in range(H_B):
            m_s[h], l_s[h], acc_s[h] = _diff_init_t(t)

    slabs = [slice(h * LANES, (h + 1) * LANES) for h in range(H_B)]
    qqts = [_split_maps_t(q_ref[0, :, cs]) for cs in slabs]
    rr, c = _key_query_diff(tk, t)
    d = (rr - c).astype(F32)
    ahead = (past - ci * tk).astype(F32)
    rows = lambda ref, h, n: _head_rows(ref, h, n, H_B, lead=(0,))[...].astype(BF16)

    def stage_a(h):
        s = _dot(rows(kc_ref, h, tk), qqts[h]) - _alibi_slope(h) * d
        s_scr[h] = s
        shift = -_alibi_slope(h) * ahead
        m_new = jnp.maximum(m_s[h], jnp.max(s, axis=0, keepdims=True) + shift)
        alpha = jnp.exp2(m_s[h] - m_new)
        m_s[h] = m_new
        return alpha, m_new - shift

    def stage_b(h, alpha, sub):
        p = jnp.exp2(s_scr[h] - sub)
        l_s[h] = alpha * l_s[h] + jnp.sum(p, axis=0, keepdims=True)
        acc_s[h] = alpha * acc_s[h] + _dot_tn(rows(vc_ref, h, tk), p.astype(BF16))

    pend = stage_a(0)
    for h in range(1, H_B):
        nxt = stage_a(h)
        stage_b(h - 1, *pend)
        pend = nxt
    stage_b(H_B - 1, *pend)

    @pl.when(ci == past // tk - 1)
    def _():
        rn, cn = _key_query_diff(t, t)
        dn = jnp.abs((rn - cn).astype(F32))
        lam = _diff_lambda(lq_ref, lk_ref, lam_init)
        for h, cs in enumerate(slabs):
            s = _dot(rows(kn_ref, h, t), qqts[h]) - _alibi_slope(h) * dn
            carry = _online_update(s, rows(vn_ref, h, t), 0.0, (m_s[h], l_s[h], acc_s[h]))
            o_ref[0, :, cs] = _diff_finish_t(carry, t, lam, gsub_ref[...], lam_init).astype(o_ref.dtype)


def _diff_attn_sample(q, kc, vc, kn, vn, lam_q, lam_k, g_sub, tk, lam_init):
    b, t, db = q.shape
    past = kc.shape[1] // H_B
    kern = functools.partial(_diff_sample_kernel, t=t, past=past, tk=tk, lam_init=lam_init)
    new_spec = pl.BlockSpec((1, t, db), lambda i, j: (i, 0, 0))
    new_kv_spec = pl.BlockSpec((1, t * H_B, LANES), lambda i, j: (i, 0, 0))
    cache_spec = pl.BlockSpec((1, tk * H_B, LANES), lambda i, j: (i, j, 0))
    return pl.pallas_call(
        kern,
        grid=(b, past // tk),
        in_specs=[new_spec, cache_spec, cache_spec, new_kv_spec, new_kv_spec,
                  _const_spec(lam_q.shape), _const_spec(lam_k.shape), _const_spec(g_sub.shape)],
        out_specs=new_spec,
        out_shape=jax.ShapeDtypeStruct((b, t, db), BF16),
        scratch_shapes=[pltpu.VMEM((H_B, 1, 2 * t), F32), pltpu.VMEM((H_B, 1, 2 * t), F32),
                        pltpu.VMEM((H_B, LANES, 2 * t), F32), pltpu.VMEM((H_B, tk, 2 * t), F32)],
        compiler_params=_params(("parallel", "arbitrary")),
        name="diff_attn_sample",
    )(q, kc, vc, kn, vn, lam_q, lam_k, g_sub)


def _mem_kernel(q_ref, mk_ref, mv_ref, o_ref, s_scr):
    slabs = [slice(h * LANES, (h + 1) * LANES) for h in range(H_M)]
    nm = mk_ref.shape[1] // H_M
    rows = lambda ref, h: _head_rows(ref, h, nm, H_M, lead=(0,))[...].astype(BF16)

    def stage_a(h):
        s = _dot(rows(mk_ref, h), q_ref[0, :, slabs[h]].astype(F32).T.astype(BF16))
        s_scr[h] = s
        return jnp.max(s, axis=0, keepdims=True)

    def stage_b(h, mx):
        p = jnp.exp2(s_scr[h] - mx)
        l = jnp.sum(p, axis=0, keepdims=True)
        o_ref[0, :, slabs[h]] = (_dot_tn(rows(mv_ref, h), p.astype(BF16)) / l).T.astype(o_ref.dtype)

    mx = stage_a(0)
    for h in range(1, H_M):
        mx_next = stage_a(h)
        stage_b(h - 1, mx)
        mx = mx_next
    stage_b(H_M - 1, mx)


def _mem_attn(q, mk, mv, tq):
    b, seq, dm = q.shape
    kv_spec = pl.BlockSpec((1,) + mk.shape[1:], lambda i, j: (i, 0, 0))
    q_spec = pl.BlockSpec((1, tq, dm), lambda i, j: (i, j, 0))
    return pl.pallas_call(
        _mem_kernel,
        grid=(b, seq // tq),
        in_specs=[q_spec, kv_spec, kv_spec],
        out_specs=q_spec,
        out_shape=jax.ShapeDtypeStruct((b, seq, dm), BF16),
        scratch_shapes=[pltpu.VMEM((H_M, mk.shape[1] // H_M, tq), F32)],
        compiler_params=_params(("parallel", "parallel")),
        name="mem_attn",
    )(q, mk, mv)


def _merge_kernel(x_ref, oa_ref, ob_ref, om_ref, g_ref, win_ref, bg_ref, wo_ref, wout_ref, o_ref, m_scr,
                  *, tn):
    x = x_ref[...]
    h = _rms(x, g_ref[...]).astype(BF16)
    o_refs = (oa_ref, ob_ref, om_ref)
    d = x.shape[1]
    for c0 in range(0, d, tn):
        cs = slice(c0, c0 + tn)
        m = None
        for br in range(N_BRANCH):
            g0 = br * d + c0
            goff = win_ref.shape[1] - N_BRANCH * d
            gate = jax.nn.sigmoid(_dot(h, win_ref[:, goff + g0:goff + g0 + tn]) + bg_ref[:, g0:g0 + tn])
            term = gate * _dot(o_refs[br][...], wo_ref[br, :, cs])
            m = term if m is None else m + term
        m_scr[:, cs] = m.astype(BF16)
    o_ref[...] = x + _dot(m_scr[...], wout_ref[...])


def _merge(x, oa, ob, om, g, wg, bg, wo, wout, tm, tn=256):
    n, d = x.shape
    kern = functools.partial(_merge_kernel, tn=tn)
    tok = lambda wd: pl.BlockSpec((tm, wd), lambda i: (i, 0))
    return pl.pallas_call(
        kern,
        grid=(n // tm,),
        in_specs=[tok(d), tok(oa.shape[1]), tok(ob.shape[1]), tok(om.shape[1]), _const_spec(g.shape),
                  _const_spec(wg.shape), _const_spec(bg.shape), _const_spec(wo.shape),
                  _const_spec(wout.shape)],
        out_specs=tok(d),
        out_shape=jax.ShapeDtypeStruct((n, d), F32),
        scratch_shapes=[pltpu.VMEM((tm, d), BF16)],
        compiler_params=_params(("parallel",)),
        name="merge",
    )(x, oa, ob, om, g, wg, bg, wo, wout)


def _ffn_kernel(x_ref, g_ref, prev_ref, wup_ref, wconv_ref, bconv_ref, wdown_ref, gfin_ref,
                y_ref, conv_ref, carry, u_scr, *, tm, nch, nseq, final_norm):
    if nseq == 1:
        @pl.when(pl.program_id(1) == 0)
        def _():
            carry[...] = prev_ref[0, 0]

    x = x_ref[0]
    h = _rms(x, g_ref[...]).astype(BF16)
    row = lax.broadcasted_iota(jnp.int32, (tm, FF_CHUNK), 0)
    seq_rows = tm // nseq
    ff = nch * FF_CHUNK
    for ch in range(nch):
        cs = slice(ch * FF_CHUNK, (ch + 1) * FF_CHUNK)
        a = _dot(h, wup_ref[:, cs])
        gate = _dot(h, wup_ref[:, ff + ch * FF_CHUNK:ff + (ch + 1) * FF_CHUNK])
        a1 = pltpu.roll(a, 1, axis=0)
        a2 = pltpu.roll(a, 2, axis=0)
        for k in range(nseq):
            before = carry if nseq == 1 else prev_ref.at[0, k]
            p2 = before[0:1, cs]
            p1 = before[1:2, cs]
            a1 = jnp.where(row == k * seq_rows, p1, a1)
            a2 = jnp.where(row == k * seq_rows, p2, jnp.where(row == k * seq_rows + 1, p1, a2))
        conv = bconv_ref[:, cs] + a2 * wconv_ref[0:1, cs]
        conv = conv + a1 * wconv_ref[1:2, cs]
        conv = conv + a * wconv_ref[2:3, cs]
        if nseq == 1:
            carry[:, cs] = a[tm - (CONV_W - 1):, :]
        else:
            for k in range(nseq):
                conv_ref[0, k, :, cs] = a[(k + 1) * seq_rows - (CONV_W - 1):(k + 1) * seq_rows, :]
        u_scr[:, cs] = (jax.nn.gelu(conv) * gate).astype(BF16)
    y = x + _dot(u_scr[...], wdown_ref[...])
    if final_norm:
        y = _rms(y, gfin_ref[...])
    y_ref[0] = y
    if nseq == 1:
        conv_ref[0, 0] = carry[...]


def _conv_ffn(x, g, prev, wup, wconv, bconv, wdown, gfin, tm, final_norm):
    b, seq, d = x.shape
    ff = wdown.shape[0]
    nch = ff // FF_CHUNK
    nseq = max(tm // seq, 1)
    groups = b // nseq
    assert (seq % tm == 0) if nseq == 1 else (tm % seq == 0 and b % nseq == 0)
    kern = functools.partial(_ffn_kernel, tm=tm, nch=nch, nseq=nseq, final_norm=final_norm)
    tok = pl.BlockSpec((1, tm, d), lambda i, j: (i, j, 0))
    state = pl.BlockSpec((1, nseq, CONV_W - 1, ff), lambda i, j: (i, 0, 0, 0))
    y, conv = pl.pallas_call(
        kern,
        grid=(groups, nseq * seq // tm),
        in_specs=[tok, _const_spec(g.shape), state, _const_spec(wup.shape), _const_spec(wconv.shape),
                  _const_spec(bconv.shape), _const_spec(wdown.shape), _const_spec(gfin.shape)],
        out_specs=[tok, state],
        out_shape=[jax.ShapeDtypeStruct((groups, nseq * seq, d), F32),
                   jax.ShapeDtypeStruct((groups, nseq, CONV_W - 1, ff), F32)],
        scratch_shapes=[pltpu.VMEM((CONV_W - 1, ff), F32), pltpu.VMEM((tm, ff), BF16)],
        compiler_params=_params(("parallel", "arbitrary")),
        name="conv_ffn",
    )(x.reshape(groups, nseq * seq, d), g, prev.reshape(groups, nseq, CONV_W - 1, ff),
      wup, wconv, bconv, wdown, gfin)
    return y.reshape(b, seq, d), conv.reshape(b, CONV_W - 1, ff)


def _token_tile(n, pref):
    return pref if n % pref == 0 else n


def _layer(x, w, *, lam_init, final_norm, conv_prev, mem=None, cache=None):
    b, t, d = x.shape
    n = b * t
    tm = _token_tile(n, 512)
    d_a, d_b, d_m = H_A * DH_A, H_B * 2 * DH_B, H_M * DH_M
    prompt = cache is None
    assert not prompt or (tm == WIN_A and t % tm == 0)
    kv_a = [(BF16, "rows"), (F32, "last")] if prompt else [(F32, "rows")]
    outs = ([(0, d_a, DH_A ** -0.5 * LOG2E, BF16, "rows")]
            + [(d_a, d_a, 1.0, dt, mode) for dt, mode in kv_a]
            + [(2 * d_a, d_a, 1.0, dt, mode) for dt, mode in kv_a]
            + [(3 * d_a, d_b, DH_B ** -0.5 * LOG2E, BF16, "rows"),
               (3 * d_a + d_b, d_b, 1.0, F32, "by_head"),
               (3 * d_a + 2 * d_b, d_b, 1.0, F32, "by_head"),
               (3 * d_a + 3 * d_b, d_m, DH_M ** -0.5 * LOG2E, BF16, "rows")])
    res = _norm_proj(x.reshape(n, d), w["g_mix"], w["w_in"], outs, tm, tiles_per_seq=max(t // tm, 1))
    res = [a.reshape(b, -1, a.shape[-1]) for a in res]
    if prompt:
        qa, ka, ka_keep, va, va_keep, qb, kb, vb, qm = res
    else:
        qa, ka, va, qb, kb, vb, qm = res
        ka_keep, va_keep = ka, va

    if cache is None:
        tq_a = 128
        oa = _band_attn(qa, ka, va, _band_bias_table(w["rel_bias"], tq_a), tq_a)
        ob = _diff_attn_prompt(qb, kb, vb, w["lam_q"], w["lam_k"], w["g_sub"], 256, lam_init)
        nm = mem.shape[1]
        mk, mv = _norm_proj(mem.reshape(b * nm, d), w["g_mem"], w["w_mem_kv"],
                            [(0, d_m, 1.0, F32, "by_head"), (d_m, d_m, 1.0, F32, "by_head")],
                            _token_tile(b * nm, 512))
        mk = mk.reshape(b, nm * H_M, DH_M)
        mv = mv.reshape(b, nm * H_M, DH_M)
        om = _mem_attn(qm, mk, mv, _token_tile(t, 512))
    else:
        oa = _band_attn(qa, ka, va, _band_bias_table(w["rel_bias"], t), t,
                        hist=(cache["a_k"], cache["a_v"]))
        ob = _diff_attn_sample(qb, cache["b_k"], cache["b_v"], kb, vb,
                               w["lam_q"], w["lam_k"], w["g_sub"], 1024, lam_init)
        mk, mv = cache["m_k"], cache["m_v"]
        om = _mem_attn(qm, mk, mv, t)

    flat = lambda a: a.reshape(n, a.shape[-1])
    x1 = _merge(x.reshape(n, d), flat(oa), flat(ob), flat(om), w["g_mix"], w["w_in"], w["b_gate"],
                w["w_o"], w["w_out"], tm)
    y, conv = _conv_ffn(x1.reshape(b, t, d), w["g_ffn"], conv_prev, w["w_up"], w["w_conv"], w["b_conv"],
                        w["w_down"], w["g_final"], tm if (t % tm == 0 or tm % t == 0) else t, final_norm)
    return y, dict(ka=ka_keep, va=va_keep, kb=kb, vb=vb, mk=mk, mv=mv, conv=conv)


def kernel(x_prompt, x_sample, mem_prompt, cache_a_k, cache_a_v, cache_b_k, cache_b_v, cache_mem_k, cache_mem_v, state_ffn_conv, g_mix, w_in, b_gate, rel_bias, lam_q, lam_k, g_sub, g_mem, w_mem_kv, w_oa, w_ob, w_om, w_out, g_ffn, w_up, w_conv, b_conv, w_down, g_final):
    depth, d, _ = w_in.shape
    b, s, _ = x_prompt.shape
    bd, t, _ = x_sample.shape
    past = cache_b_k.shape[2]
    a_buf = cache_a_k.shape[2]
    ff = w_down.shape[1]
    d_a, d_b, d_m = H_A * DH_A, H_B * 2 * DH_B, H_M * DH_M
    a_keep = min(WIN_A, s)
    assert a_buf == WIN_A and a_keep == WIN_A and t == CHUNK and past % CHUNK == 0 and ff % FF_CHUNK == 0
    nch = ff // FF_CHUNK

    xp, xs = x_prompt, x_sample
    outs = {k: [] for k in ("pa_k", "pa_v", "pb_k", "pb_v", "pm_k", "pm_v", "pconv",
                            "sa_k", "sa_v", "sb_k", "sb_v", "sconv")}
    for l in range(depth):
        lam_init = 0.8 - 0.6 * math.exp(-0.3 * l)
        w = dict(
            g_mix=g_mix[l][None], g_mem=g_mem[l][None], g_ffn=g_ffn[l][None], g_final=g_final[None],
            g_sub=g_sub[l][None], lam_q=lam_q[l], lam_k=lam_k[l], rel_bias=rel_bias[l],
            w_in=w_in[l].astype(BF16),
            b_gate=b_gate[l][None],
            w_mem_kv=w_mem_kv[l].astype(BF16),
            w_o=jnp.stack([w_oa[l], w_ob[l], w_om[l]]).astype(BF16),
            w_out=w_out[l].astype(BF16),
            w_up=w_up[l].astype(BF16),
            w_conv=w_conv[l], b_conv=b_conv[l][None],
            w_down=w_down[l].astype(BF16),
        )
        last = l == depth - 1
        xp, po = _layer(xp, w, lam_init=lam_init, final_norm=last,
                        conv_prev=jnp.zeros((b, CONV_W - 1, ff), F32), mem=mem_prompt)
        cache = dict(a_k=cache_a_k[l].reshape(bd, a_buf, d_a), a_v=cache_a_v[l].reshape(bd, a_buf, d_a),
                     b_k=cache_b_k[l].reshape(bd, past * H_B, 2 * DH_B),
                     b_v=cache_b_v[l].reshape(bd, past * H_B, 2 * DH_B),
                     m_k=cache_mem_k[l].reshape(bd, -1, DH_M), m_v=cache_mem_v[l].reshape(bd, -1, DH_M))
        xs, so = _layer(xs, w, lam_init=lam_init, final_norm=last, conv_prev=state_ffn_conv[l], cache=cache)

        outs["pa_k"].append(po["ka"].reshape(b, a_keep, H_A, DH_A))
        outs["pa_v"].append(po["va"].reshape(b, a_keep, H_A, DH_A))
        outs["pb_k"].append(po["kb"].reshape(b, s, H_B, 2 * DH_B))
        outs["pb_v"].append(po["vb"].reshape(b, s, H_B, 2 * DH_B))
        outs["pm_k"].append(po["mk"].reshape(b, -1, H_M, DH_M))
        outs["pm_v"].append(po["mv"].reshape(b, -1, H_M, DH_M))
        outs["pconv"].append(po["conv"])
        new_k = jnp.concatenate([cache["a_k"], so["ka"]], axis=1)[:, t:]
        new_v = jnp.concatenate([cache["a_v"], so["va"]], axis=1)[:, t:]
        outs["sa_k"].append(new_k.reshape(bd, a_buf, H_A, DH_A))
        outs["sa_v"].append(new_v.reshape(bd, a_buf, H_A, DH_A))
        outs["sb_k"].append(so["kb"].reshape(bd, t, H_B, 2 * DH_B))
        outs["sb_v"].append(so["vb"].reshape(bd, t, H_B, 2 * DH_B))
        outs["sconv"].append(so["conv"])

    st = {k: jnp.stack(v) for k, v in outs.items()}
    return (xp, xs, st["pa_k"], st["pa_v"], st["pb_k"], st["pb_v"], st["pm_k"], st["pm_v"], st["pconv"],
            st["sa_k"], st["sa_v"], st["sb_k"], st["sb_v"], st["sconv"])
```

```python
import functools
import math

import jax
import jax.numpy as jnp
from jax import lax
from jax.experimental import pallas as pl
from jax.experimental.pallas import tpu as pltpu

CHUNK = 64
BAND_CHUNKS = 8
WIN_A = BAND_CHUNKS * CHUNK
REL_CLIP = 128
H_A = 8
DH_A = 64
H_B = 4
DH_B = 64
H_M = 4
DH_M = 128
N_BRANCH = 3
CONV_W = 3
EPS = 1e-6
NEG_INF = -1e30

LANES = 128
FF_CHUNK = 256
VMEM_LIMIT = 56 * 1024 * 1024

LOG2E = math.log2(math.e)

BF16 = jnp.bfloat16
F32 = jnp.float32


def _rms(x, g):
    y = x * lax.rsqrt(jnp.mean(x * x, axis=-1, keepdims=True) + EPS)
    return y * g


def _dot(a, b):
    return jnp.dot(a, b, preferred_element_type=F32)


def _dot_nt(a, b):
    return lax.dot_general(a, b, (((1,), (1,)), ((), ())), preferred_element_type=F32)


def _dot_tn(a, b):
    return lax.dot_general(a, b, (((0,), (0,)), ((), ())), preferred_element_type=F32)


def _const_spec(shape):
    nd = len(shape)
    return pl.BlockSpec(shape, lambda *_: (0,) * nd, pipeline_mode=pl.Buffered(1))


def _params(sem, flags=None):
    return pltpu.CompilerParams(dimension_semantics=sem, vmem_limit_bytes=VMEM_LIMIT, flags=flags)


def _head_rows(ref, h, n, heads, lead=(), first=0):
    return ref.at[lead + (pl.ds(first * heads + h, n, stride=heads), slice(None))]


def _norm_proj_kernel(x_ref, g_ref, w_ref, *o_refs, outs):
    tm = x_ref.shape[0]
    h = _rms(x_ref[...], g_ref[...]).astype(BF16)
    z_of = {}
    for o_ref, (c0, wd, sc, _, mode) in zip(o_refs, outs):
        if (c0, wd) not in z_of:
            z_of[(c0, wd)] = _dot(h, w_ref[:, c0:c0 + wd])
        z = z_of[(c0, wd)]
        if sc != 1.0:
            z = z * sc
        if mode == "by_head":
            heads = wd // LANES
            for hd in range(heads):
                _head_rows(o_ref, hd, tm, heads)[...] = z[:, hd * LANES:(hd + 1) * LANES].astype(o_ref.dtype)
        else:
            o_ref[...] = z.astype(o_ref.dtype)


def _norm_proj(x, g, w, outs, tm, tiles_per_seq=1):
    n, d = x.shape
    kern = functools.partial(_norm_proj_kernel, outs=outs)
    specs, shapes = [], []
    for _, wd, _, dt, mode in outs:
        if mode == "by_head":
            blk, rows, imap = ((wd // LANES) * tm, LANES), (wd // LANES) * n, lambda i: (i, 0)
        elif mode == "last":
            blk, rows, imap = (tm, wd), n // tiles_per_seq, lambda i: (i // tiles_per_seq, 0)
        else:
            blk, rows, imap = (tm, wd), n, lambda i: (i, 0)
        specs.append(pl.BlockSpec(blk, imap))
        shapes.append(jax.ShapeDtypeStruct((rows, blk[1]), dt))
    return pl.pallas_call(
        kern,
        grid=(n // tm,),
        in_specs=[
            pl.BlockSpec((tm, d), lambda i: (i, 0)),
            _const_spec((1, d)),
            _const_spec(w.shape),
        ],
        out_specs=specs,
        out_shape=shapes,
        compiler_params=_params(("arbitrary",)),
        name="norm_proj",
    )(x, g, w)


def _band_kernel(*refs, tq, seq, use_hist):
    if use_hist:
        q_ref, k_ref, v_ref, hk_ref, hv_ref, bias_ref, o_ref, kp, vp, s_scr, qt_scr = refs
    else:
        q_ref, k_ref, v_ref, bias_ref, o_ref, kp, vp, s_scr, qt_scr = refs
    win = WIN_A + tq
    nblk = seq // tq
    npair = H_A // 2

    if use_hist:
        kp[0:WIN_A, :] = hk_ref[0].astype(BF16)
        vp[0:WIN_A, :] = hv_ref[0].astype(BF16)
    else:
        kp[0:WIN_A, :] = jnp.zeros((WIN_A, kp.shape[1]), BF16)
        vp[0:WIN_A, :] = jnp.zeros((WIN_A, vp.shape[1]), BF16)
    kp[WIN_A:WIN_A + seq, :] = k_ref[0].astype(BF16)
    vp[WIN_A:WIN_A + seq, :] = v_ref[0].astype(BF16)

    feat = lax.broadcasted_iota(jnp.int32, (LANES, tq), 0)
    lo = feat < DH_A
    slabs = [slice(hp * LANES, (hp + 1) * LANES) for hp in range(npair)]

    for blk in range(nblk):
        for hp in range(npair):
            qt_scr[hp, blk] = _split_maps_t(q_ref[0, blk * tq:(blk + 1) * tq, slabs[hp]])

    def stage_a(m, hp, mask_history):
        off = m * tq if isinstance(m, int) else pl.multiple_of(m * tq, tq)
        s = _dot(kp[pl.ds(off, win), slabs[hp]], qt_scr[hp, m]) + bias_ref[hp]
        if mask_history:
            key = lax.broadcasted_iota(jnp.int32, (win, 2 * tq), 0)
            s = jnp.where(key >= (WIN_A - m * tq), s, NEG_INF)
        s_scr[hp] = s
        return jnp.max(s, axis=0, keepdims=True)

    def stage_b(m, hp, mx):
        off = pl.multiple_of(m * tq, tq)
        p = jnp.exp2(s_scr[hp] - mx)
        l = jnp.sum(p, axis=0, keepdims=True)
        o = _dot_tn(vp[pl.ds(off, win), slabs[hp]], p.astype(BF16)) / l
        o_ref[0, pl.ds(off, tq), slabs[hp]] = jnp.where(lo, o[:, :tq], o[:, tq:]).T.astype(o_ref.dtype)

    def block(mask_history, m, mx):
        for hp in range(1, npair):
            mx_next = stage_a(m, hp, mask_history)
            stage_b(m, hp - 1, mx)
            mx = mx_next
        mx_next = stage_a(jnp.minimum(m + 1, nblk - 1), 0, mask_history)
        stage_b(m, npair - 1, mx)
        return mx_next

    nmask = 0 if use_hist else min(WIN_A // tq, nblk)
    mx = stage_a(0, 0, nmask > 0)
    unroll = lambda trips: 4 if trips % 4 == 0 else (2 if trips % 2 == 0 else 1)
    mx = lax.fori_loop(0, nmask, functools.partial(block, True), mx, unroll=unroll(nmask))
    lax.fori_loop(nmask, nblk, functools.partial(block, False), mx, unroll=unroll(nblk - nmask))


def _band_attn(q, k, v, bias, tq, hist=None):
    b, seq, da = q.shape
    win = WIN_A + tq
    use_hist = hist is not None
    kern = functools.partial(_band_kernel, tq=tq, seq=seq, use_hist=use_hist)
    full = pl.BlockSpec((1, seq, da), lambda i: (i, 0, 0))
    in_specs = [full, full, full]
    args = [q, k, v]
    if use_hist:
        hspec = pl.BlockSpec((1, WIN_A, da), lambda i: (i, 0, 0))
        in_specs += [hspec, hspec]
        args += list(hist)
    in_specs.append(_const_spec(bias.shape))
    args.append(bias)
    return pl.pallas_call(
        kern,
        grid=(b,),
        in_specs=in_specs,
        out_specs=full,
        out_shape=jax.ShapeDtypeStruct((b, seq, da), BF16),
        scratch_shapes=[pltpu.VMEM((WIN_A + seq, da), BF16), pltpu.VMEM((WIN_A + seq, da), BF16),
                        pltpu.VMEM((H_A // 2, win, 2 * tq), F32),
                        pltpu.VMEM((H_A // 2, seq // tq, LANES, 2 * tq), BF16)],
        compiler_params=_params(("parallel",)),
        name="band_attn",
    )(*args)


BIAS_LANE0 = WIN_A + REL_CLIP
BIAS_WIDTH = 768


def _band_bias_kernel(rb_ref, o_ref, *, tq, win):
    z = lax.broadcasted_iota(jnp.int32, (1, BIAS_WIDTH), 1)
    j = lax.broadcasted_iota(jnp.int32, (win, tq), 0)
    r = lax.broadcasted_iota(jnp.int32, (win, tq), 1)
    kc = j // CHUNK
    qc = r // CHUNK
    band = (kc >= qc) & (kc <= qc + BAND_CHUNKS)
    for h in range(H_A):
        row = jnp.where(z >= 2 * REL_CLIP, rb_ref[h:h + 1, 2 * REL_CLIP:2 * REL_CLIP + 1], rb_ref[h:h + 1, :])
        skew = pltpu.roll(jnp.broadcast_to(row, (win, BIAS_WIDTH)), 0, axis=1, stride=1, stride_axis=0)
        tbl = jnp.where(band, skew[:, BIAS_LANE0:BIAS_LANE0 + tq] * LOG2E, NEG_INF)
        o_ref[h // 2, :, (h % 2) * tq:(h % 2 + 1) * tq] = tbl


def _band_bias_table(rel_bias, tq):
    win = WIN_A + tq
    assert BIAS_LANE0 + tq <= BIAS_WIDTH and win <= BIAS_WIDTH
    rb = jnp.pad(rel_bias.astype(F32), ((0, 0), (0, BIAS_WIDTH - rel_bias.shape[1])))
    return pl.pallas_call(
        functools.partial(_band_bias_kernel, tq=tq, win=win),
        out_shape=jax.ShapeDtypeStruct((H_A // 2, win, 2 * tq), F32),
        compiler_params=pltpu.CompilerParams(vmem_limit_bytes=VMEM_LIMIT),
        name="band_bias",
    )(rb)


def _split_maps_t(q):
    qt = q.astype(F32).T
    feat = lax.broadcasted_iota(jnp.int32, qt.shape, 0)
    lo = feat < DH_B
    return jnp.concatenate([jnp.where(lo, qt, 0.0), jnp.where(lo, 0.0, qt)], axis=1).astype(BF16)


def _key_query_diff(tk, tq):
    c = lax.broadcasted_iota(jnp.int32, (tk, 2 * tq), 0)
    r = lax.broadcasted_iota(jnp.int32, (tk, 2 * tq), 1)
    rr = jnp.where(r >= tq, r - tq, r)
    return rr, c


def _alibi_slope(h):
    return 2.0 ** (-8.0 * (h + 1) / H_B) * LOG2E


def _online_update(s, vb, shift, carry):
    m, l, acc = carry
    m_new = jnp.maximum(m, jnp.max(s, axis=0, keepdims=True) + shift)
    alpha = jnp.exp2(m - m_new)
    p = jnp.exp2(s - (m_new - shift))
    l = alpha * l + jnp.sum(p, axis=0, keepdims=True)
    acc = alpha * acc + _dot_tn(vb, p.astype(BF16))
    return m_new, l, acc


def _diff_finish_t(carry, tq, lam, gsub, lam_init):
    _, l, acc = carry
    o = acc / l
    od = (o[:, :tq] - lam * o[:, tq:]).T
    return _rms(od, gsub) * (1.0 - lam_init)


def _diff_lambda(lq_ref, lk_ref, lam_init):
    e = jnp.exp(jnp.sum(lq_ref[...] * lk_ref[...], axis=-1, keepdims=True))
    return e[0:1] - e[1:2] + lam_init


def _diff_init_t(tq):
    return (jnp.full((1, 2 * tq), NEG_INF, F32), jnp.zeros((1, 2 * tq), F32),
            jnp.zeros((LANES, 2 * tq), F32))


def _diff_prompt_kernel(q_ref, k_ref, v_ref, lq_ref, lk_ref, gsub_ref, o_ref,
                        kbf, vbf, bias_scr, qq_scr, s0_scr, s1_scr, acc_scr, *, tq, lam_init):
    qi = pl.program_id(1)
    slabs = [slice(h * LANES, (h + 1) * LANES) for h in range(H_B)]

    @pl.when(qi == 0)
    def _():
        for h, cs in enumerate(slabs):
            kbf[:, cs] = _head_rows(k_ref, h, kbf.shape[0], H_B, lead=(0,))[...].astype(BF16)
            vbf[:, cs] = _head_rows(v_ref, h, vbf.shape[0], H_B, lead=(0,))[...].astype(BF16)
        rr, c = _key_query_diff(tq, tq)
        d = (rr - c).astype(F32)
        allowed = (c // CHUNK) <= (rr // CHUNK)
        for h in range(H_B):
            bias_scr[0, h] = -_alibi_slope(h) * d
            bias_scr[1, h] = jnp.where(allowed, -_alibi_slope(h) * jnp.abs(d), NEG_INF)

    for h, cs in enumerate(slabs):
        qq_scr[h] = _split_maps_t(q_ref[0, :, cs])
    acc_scr[...] = jnp.zeros(acc_scr.shape, F32)

    def stage_a(j, s_ref, ms):
        off = j * tq if isinstance(j, int) else pl.multiple_of(j * tq, tq)
        diag = (j == qi).astype(jnp.int32)
        out = []
        for h in range(H_B):
            s = _dot(kbf[pl.ds(off, tq), slabs[h]], qq_scr[h]) + bias_scr[diag, h]
            s_ref[h] = s
            shift = -_alibi_slope(h) * ((qi - j) * tq).astype(F32)
            m_new = jnp.maximum(ms[h], jnp.max(s, axis=0, keepdims=True) + shift)
            out.append((m_new, jnp.exp2(ms[h] - m_new), m_new - shift))
        return tuple(o[0] for o in out), tuple(o[1] for o in out), tuple(o[2] for o in out)

    def stage_b(j, s_ref, ls, alphas, subs):
        off = j * tq if isinstance(j, int) else pl.multiple_of(j * tq, tq)
        out = []
        for h in range(H_B):
            p = jnp.exp2(s_ref[h] - subs[h])
            acc_scr[h] = alphas[h] * acc_scr[h] + _dot_tn(vbf[pl.ds(off, tq), slabs[h]], p.astype(BF16))
            out.append(alphas[h] * ls[h] + jnp.sum(p, axis=0, keepdims=True))
        return tuple(out)

    def finish(ls):
        lam = _diff_lambda(lq_ref, lk_ref, lam_init)
        for h, cs in enumerate(slabs):
            o_ref[0, :, cs] = _diff_finish_t((None, ls[h], acc_scr[h]), tq, lam, gsub_ref[...],
                                             lam_init).astype(o_ref.dtype)

    row = lambda val: tuple(jnp.full((1, 2 * tq), val, F32) for _ in range(H_B))
    ms, alphas, subs = stage_a(0, s0_scr, row(NEG_INF))

    def two_blocks(jj, carry):
        ms, ls, alphas, subs = carry
        j = 2 * jj
        ms, alphas1, subs1 = stage_a(j + 1, s1_scr, ms)
        ls = stage_b(j, s0_scr, ls, alphas, subs)
        ms, alphas, subs = stage_a(j + 2, s0_scr, ms)
        ls = stage_b(j + 1, s1_scr, ls, alphas1, subs1)
        return ms, ls, alphas, subs

    ms, ls, alphas, subs = lax.fori_loop(0, qi // 2, two_blocks, (ms, row(0.0), alphas, subs))

    @pl.when(qi % 2 == 1)
    def _():
        _, alphas1, subs1 = stage_a(qi, s1_scr, ms)
        finish(stage_b(qi, s1_scr, stage_b(qi - 1, s0_scr, ls, alphas, subs), alphas1, subs1))

    @pl.when(qi % 2 == 0)
    def _():
        finish(stage_b(qi, s0_scr, ls, alphas, subs))


def _diff_attn_prompt(q, k, v, lam_q, lam_k, g_sub, tq, lam_init):
    b, seq, db = q.shape
    kern = functools.partial(_diff_prompt_kernel, tq=tq, lam_init=lam_init)
    kv_spec = pl.BlockSpec((1, seq * H_B, LANES), lambda i, j: (i, 0, 0))
    q_spec = pl.BlockSpec((1, tq, db), lambda i, j: (i, j, 0))
    return pl.pallas_call(
        kern,
        grid=(b, seq // tq),
        in_specs=[q_spec, kv_spec, kv_spec,
                  _const_spec(lam_q.shape), _const_spec(lam_k.shape), _const_spec(g_sub.shape)],
        out_specs=q_spec,
        out_shape=jax.ShapeDtypeStruct((b, seq, db), BF16),
        scratch_shapes=[pltpu.VMEM((seq, db), BF16), pltpu.VMEM((seq, db), BF16),
                        pltpu.VMEM((2, H_B, tq, 2 * tq), F32), pltpu.VMEM((H_B, LANES, 2 * tq), BF16),
                        pltpu.VMEM((H_B, tq, 2 * tq), F32), pltpu.VMEM((H_B, tq, 2 * tq), F32),
                        pltpu.VMEM((H_B, LANES, 2 * tq), F32)],
        compiler_params=_params(("parallel", "arbitrary")),
        name="diff_attn_prompt",
    )(q, k, v, lam_q, lam_k, g_sub)


def _diff_sample_kernel(q_ref, kc_ref, vc_ref, kn_ref, vn_ref, lq_ref, lk_ref, gsub_ref, o_ref,
                        m_s, l_s, acc_s, s_scr, *, t, past, tk, ts, lam_init):
    ci = pl.program_id(1)

    @pl.when(ci == 0)
    def _():
        for h in range(H_B):
            m_s[h], l_s[h], acc_s[h] = _diff_init_t(t)

    slabs = [slice(h * LANES, (h + 1) * LANES) for h in range(H_B)]
    qqts = [_split_maps_t(q_ref[0, :, cs]) for cs in slabs]
    rr, c = _key_query_diff(ts, t)
    d = (rr - c).astype(F32)
    rows = lambda ref, h, n, first=0: _head_rows(ref, h, n, H_B, lead=(0,), first=first)[...].astype(BF16)
    items = [(sub, h) for sub in range(tk // ts) for h in range(H_B)]

    def stage_a(i):
        sub, h = items[i]
        s = _dot(rows(kc_ref, h, ts, sub * ts), qqts[h]) - _alibi_slope(h) * d
        s_scr[i % 2] = s
        shift = -_alibi_slope(h) * (past - ci * tk - sub * ts).astype(F32)
        m_new = jnp.maximum(m_s[h], jnp.max(s, axis=0, keepdims=True) + shift)
        alpha = jnp.exp2(m_s[h] - m_new)
        m_s[h] = m_new
        return alpha, m_new - shift

    def stage_b(i, alpha, ref_max):
        sub, h = items[i]
        p = jnp.exp2(s_scr[i % 2] - ref_max)
        l_s[h] = alpha * l_s[h] + jnp.sum(p, axis=0, keepdims=True)
        acc_s[h] = alpha * acc_s[h] + _dot_tn(rows(vc_ref, h, ts, sub * ts), p.astype(BF16))

    pend = stage_a(0)
    for i in range(1, len(items)):
        nxt = stage_a(i)
        stage_b(i - 1, *pend)
        pend = nxt
    stage_b(len(items) - 1, *pend)

    @pl.when(ci == past // tk - 1)
    def _():
        rn, cn = _key_query_diff(t, t)
        dn = jnp.abs((rn - cn).astype(F32))
        lam = _diff_lambda(lq_ref, lk_ref, lam_init)
        for h, cs in enumerate(slabs):
            s = _dot(rows(kn_ref, h, t), qqts[h]) - _alibi_slope(h) * dn
            carry = _online_update(s, rows(vn_ref, h, t), 0.0, (m_s[h], l_s[h], acc_s[h]))
            o_ref[0, :, cs] = _diff_finish_t(carry, t, lam, gsub_ref[...], lam_init).astype(o_ref.dtype)


def _diff_attn_sample(q, kc, vc, kn, vn, lam_q, lam_k, g_sub, tk, ts, lam_init):
    b, t, db = q.shape
    past = kc.shape[1] // H_B
    kern = functools.partial(_diff_sample_kernel, t=t, past=past, tk=tk, ts=ts, lam_init=lam_init)
    new_spec = pl.BlockSpec((1, t, db), lambda i, j: (i, 0, 0))
    new_kv_spec = pl.BlockSpec((1, t * H_B, LANES), lambda i, j: (i, 0, 0))
    cache_spec = pl.BlockSpec((1, tk * H_B, LANES), lambda i, j: (i, j, 0))
    return pl.pallas_call(
        kern,
        grid=(b, past // tk),
        in_specs=[new_spec, cache_spec, cache_spec, new_kv_spec, new_kv_spec,
                  _const_spec(lam_q.shape), _const_spec(lam_k.shape), _const_spec(g_sub.shape)],
        out_specs=new_spec,
        out_shape=jax.ShapeDtypeStruct((b, t, db), BF16),
        scratch_shapes=[pltpu.VMEM((H_B, 1, 2 * t), F32), pltpu.VMEM((H_B, 1, 2 * t), F32),
                        pltpu.VMEM((H_B, LANES, 2 * t), F32), pltpu.VMEM((2, ts, 2 * t), F32)],
        compiler_params=_params(("parallel", "arbitrary")),
        name="diff_attn_sample",
    )(q, kc, vc, kn, vn, lam_q, lam_k, g_sub)


def _mem_kernel(q_ref, mk_ref, mv_ref, o_ref):
    slabs = [slice(h * LANES, (h + 1) * LANES) for h in range(H_M)]
    nm = mk_ref.shape[1] // H_M
    rows = lambda ref, h: _head_rows(ref, h, nm, H_M, lead=(0,))[...].astype(BF16)
    scores = [_dot_nt(q_ref[0, :, cs], rows(mk_ref, h)) for h, cs in enumerate(slabs)]
    for h, cs in enumerate(slabs):
        mx = jnp.max(scores[h], axis=-1, keepdims=True)
        p = jnp.exp2(scores[h] - mx)
        l = jnp.sum(p, axis=-1, keepdims=True)
        o_ref[0, :, cs] = (_dot(p.astype(BF16), rows(mv_ref, h)) / l).astype(o_ref.dtype)


def _mem_attn(q, mk, mv, tq):
    b, seq, dm = q.shape
    kv_spec = pl.BlockSpec((1,) + mk.shape[1:], lambda i, j: (i, 0, 0))
    q_spec = pl.BlockSpec((1, tq, dm), lambda i, j: (i, j, 0))
    return pl.pallas_call(
        _mem_kernel,
        grid=(b, seq // tq),
        in_specs=[q_spec, kv_spec, kv_spec],
        out_specs=q_spec,
        out_shape=jax.ShapeDtypeStruct((b, seq, dm), BF16),
        compiler_params=_params(("parallel", "parallel")),
        name="mem_attn",
    )(q, mk, mv)


def _merge_kernel(x_ref, oa_ref, ob_ref, om_ref, g_ref, win_ref, bg_ref, wo_ref, wout_ref, o_ref, m_scr,
                  *, tn):
    x = x_ref[...]
    h = _rms(x, g_ref[...]).astype(BF16)
    o_refs = (oa_ref, ob_ref, om_ref)
    d = x.shape[1]
    for c0 in range(0, d, tn):
        cs = slice(c0, c0 + tn)
        m = None
        for br in range(N_BRANCH):
            g0 = br * d + c0
            goff = win_ref.shape[1] - N_BRANCH * d
            gate = jax.nn.sigmoid(_dot(h, win_ref[:, goff + g0:goff + g0 + tn]) + bg_ref[:, g0:g0 + tn])
            term = gate * _dot(o_refs[br][...], wo_ref[br, :, cs])
            m = term if m is None else m + term
        m_scr[:, cs] = m.astype(BF16)
    o_ref[...] = x + _dot(m_scr[...], wout_ref[...])


def _merge(x, oa, ob, om, g, wg, bg, wo, wout, tm, tn=256):
    n, d = x.shape
    kern = functools.partial(_merge_kernel, tn=tn)
    tok = lambda wd: pl.BlockSpec((tm, wd), lambda i: (i, 0))
    return pl.pallas_call(
        kern,
        grid=(n // tm,),
        in_specs=[tok(d), tok(oa.shape[1]), tok(ob.shape[1]), tok(om.shape[1]), _const_spec(g.shape),
                  _const_spec(wg.shape), _const_spec(bg.shape), _const_spec(wo.shape),
                  _const_spec(wout.shape)],
        out_specs=tok(d),
        out_shape=jax.ShapeDtypeStruct((n, d), F32),
        scratch_shapes=[pltpu.VMEM((tm, d), BF16)],
        compiler_params=_params(("parallel",)),
        name="merge",
    )(x, oa, ob, om, g, wg, bg, wo, wout)


def _ffn_kernel(x_ref, g_ref, prev_ref, wup_ref, wconv_ref, bconv_ref, wdown_ref, gfin_ref,
                y_ref, conv_ref, carry, u_scr, *, tm, nch, nseq, final_norm):
    if nseq == 1:
        @pl.when(pl.program_id(1) == 0)
        def _():
            carry[...] = prev_ref[0, 0]

    x = x_ref[0]
    h = _rms(x, g_ref[...]).astype(BF16)
    row = lax.broadcasted_iota(jnp.int32, (tm, FF_CHUNK), 0)
    seq_rows = tm // nseq
    ff = nch * FF_CHUNK
    for ch in range(nch):
        cs = slice(ch * FF_CHUNK, (ch + 1) * FF_CHUNK)
        a = _dot(h, wup_ref[:, cs])
        gate = _dot(h, wup_ref[:, ff + ch * FF_CHUNK:ff + (ch + 1) * FF_CHUNK])
        a1 = pltpu.roll(a, 1, axis=0)
        a2 = pltpu.roll(a, 2, axis=0)
        for k in range(nseq):
            before = carry if nseq == 1 else prev_ref.at[0, k]
            p2 = before[0:1, cs]
            p1 = before[1:2, cs]
            a1 = jnp.where(row == k * seq_rows, p1, a1)
            a2 = jnp.where(row == k * seq_rows, p2, jnp.where(row == k * seq_rows + 1, p1, a2))
        conv = bconv_ref[:, cs] + a2 * wconv_ref[0:1, cs]
        conv = conv + a1 * wconv_ref[1:2, cs]
        conv = conv + a * wconv_ref[2:3, cs]
        if nseq == 1:
            carry[:, cs] = a[tm - (CONV_W - 1):, :]
        else:
            for k in range(nseq):
                conv_ref[0, k, :, cs] = a[(k + 1) * seq_rows - (CONV_W - 1):(k + 1) * seq_rows, :]
        u_scr[:, cs] = (jax.nn.gelu(conv) * gate).astype(BF16)
    y = x + _dot(u_scr[...], wdown_ref[...])
    if final_norm:
        y = _rms(y, gfin_ref[...])
    y_ref[0] = y
    if nseq == 1:
        conv_ref[0, 0] = carry[...]


def _conv_ffn(x, g, prev, wup, wconv, bconv, wdown, gfin, tm, final_norm):
    b, seq, d = x.shape
    ff = wdown.shape[0]
    nch = ff // FF_CHUNK
    nseq = max(tm // seq, 1)
    groups = b // nseq
    assert (seq % tm == 0) if nseq == 1 else (tm % seq == 0 and b % nseq == 0)
    kern = functools.partial(_ffn_kernel, tm=tm, nch=nch, nseq=nseq, final_norm=final_norm)
    tok = pl.BlockSpec((1, tm, d), lambda i, j: (i, j, 0))
    state = pl.BlockSpec((1, nseq, CONV_W - 1, ff), lambda i, j: (i, 0, 0, 0))
    y, conv = pl.pallas_call(
        kern,
        grid=(groups, nseq * seq // tm),
        in_specs=[tok, _const_spec(g.shape), state, _const_spec(wup.shape), _const_spec(wconv.shape),
                  _const_spec(bconv.shape), _const_spec(wdown.shape), _const_spec(gfin.shape)],
        out_specs=[tok, state],
        out_shape=[jax.ShapeDtypeStruct((groups, nseq * seq, d), F32),
                   jax.ShapeDtypeStruct((groups, nseq, CONV_W - 1, ff), F32)],
        scratch_shapes=[pltpu.VMEM((CONV_W - 1, ff), F32), pltpu.VMEM((tm, ff), BF16)],
        compiler_params=_params(("parallel", "arbitrary")),
        name="conv_ffn",
    )(x.reshape(groups, nseq * seq, d), g, prev.reshape(groups, nseq, CONV_W - 1, ff),
      wup, wconv, bconv, wdown, gfin)
    return y.reshape(b, seq, d), conv.reshape(b, CONV_W - 1, ff)


def _token_tile(n, pref):
    return pref if n % pref == 0 else n


def _ffn_tile(b, t):
    if t % 1024 == 0:
        return 1024
    if t % 512 == 0 or (512 % t == 0 and (b * t) % 512 == 0):
        return 512
    return t


def _layer(x, w, *, lam_init, final_norm, conv_prev, mem=None, cache=None):
    b, t, d = x.shape
    n = b * t
    tm = _token_tile(n, 512)
    d_a, d_b, d_m = H_A * DH_A, H_B * 2 * DH_B, H_M * DH_M
    prompt = cache is None
    assert not prompt or (tm == WIN_A and t % tm == 0)
    kv_a = [(BF16, "rows"), (F32, "last")] if prompt else [(F32, "rows")]
    outs = ([(0, d_a, DH_A ** -0.5 * LOG2E, BF16, "rows")]
            + [(d_a, d_a, 1.0, dt, mode) for dt, mode in kv_a]
            + [(2 * d_a, d_a, 1.0, dt, mode) for dt, mode in kv_a]
            + [(3 * d_a, d_b, DH_B ** -0.5 * LOG2E, BF16, "rows"),
               (3 * d_a + d_b, d_b, 1.0, F32, "by_head"),
               (3 * d_a + 2 * d_b, d_b, 1.0, F32, "by_head"),
               (3 * d_a + 3 * d_b, d_m, DH_M ** -0.5 * LOG2E, BF16, "rows")])
    res = _norm_proj(x.reshape(n, d), w["g_mix"], w["w_in"], outs, tm, tiles_per_seq=max(t // tm, 1))
    res = [a.reshape(b, -1, a.shape[-1]) for a in res]
    if prompt:
        qa, ka, ka_keep, va, va_keep, qb, kb, vb, qm = res
    else:
        qa, ka, va, qb, kb, vb, qm = res
        ka_keep, va_keep = ka, va

    if cache is None:
        tq_a = 128
        oa = _band_attn(qa, ka, va, _band_bias_table(w["rel_bias"], tq_a), tq_a)
        ob = _diff_attn_prompt(qb, kb, vb, w["lam_q"], w["lam_k"], w["g_sub"], 256, lam_init)
        nm = mem.shape[1]
        mk, mv = _norm_proj(mem.reshape(b * nm, d), w["g_mem"], w["w_mem_kv"],
                            [(0, d_m, 1.0, F32, "by_head"), (d_m, d_m, 1.0, F32, "by_head")],
                            _token_tile(b * nm, 512))
        mk = mk.reshape(b, nm * H_M, DH_M)
        mv = mv.reshape(b, nm * H_M, DH_M)
        om = _mem_attn(qm, mk, mv, _token_tile(t, 512))
    else:
        oa = _band_attn(qa, ka, va, _band_bias_table(w["rel_bias"], t), t,
                        hist=(cache["a_k"], cache["a_v"]))
        ob = _diff_attn_sample(qb, cache["b_k"], cache["b_v"], kb, vb,
                               w["lam_q"], w["lam_k"], w["g_sub"], 2048, 512, lam_init)
        mk, mv = cache["m_k"], cache["m_v"]
        om = _mem_attn(qm, mk, mv, t)

    flat = lambda a: a.reshape(n, a.shape[-1])
    x1 = _merge(x.reshape(n, d), flat(oa), flat(ob), flat(om), w["g_mix"], w["w_in"], w["b_gate"],
                w["w_o"], w["w_out"], tm)
    y, conv = _conv_ffn(x1.reshape(b, t, d), w["g_ffn"], conv_prev, w["w_up"], w["w_conv"], w["b_conv"],
                        w["w_down"], w["g_final"], _ffn_tile(b, t), final_norm)
    return y, dict(ka=ka_keep, va=va_keep, kb=kb, vb=vb, mk=mk, mv=mv, conv=conv)


def kernel(x_prompt, x_sample, mem_prompt, cache_a_k, cache_a_v, cache_b_k, cache_b_v, cache_mem_k, cache_mem_v, state_ffn_conv, g_mix, w_in, b_gate, rel_bias, lam_q, lam_k, g_sub, g_mem, w_mem_kv, w_oa, w_ob, w_om, w_out, g_ffn, w_up, w_conv, b_conv, w_down, g_final):
    depth, d, _ = w_in.shape
    b, s, _ = x_prompt.shape
    bd, t, _ = x_sample.shape
    past = cache_b_k.shape[2]
    a_buf = cache_a_k.shape[2]
    ff = w_down.shape[1]
    d_a, d_b, d_m = H_A * DH_A, H_B * 2 * DH_B, H_M * DH_M
    a_keep = min(WIN_A, s)
    assert a_buf == WIN_A and a_keep == WIN_A and t == CHUNK and past % CHUNK == 0 and ff % FF_CHUNK == 0
    nch = ff // FF_CHUNK

    xp, xs = x_prompt, x_sample
    outs = {k: [] for k in ("pa_k", "pa_v", "pb_k", "pb_v", "pm_k", "pm_v", "pconv",
                            "sa_k", "sa_v", "sb_k", "sb_v", "sconv")}
    for l in range(depth):
        lam_init = 0.8 - 0.6 * math.exp(-0.3 * l)
        w = dict(
            g_mix=g_mix[l][None], g_mem=g_mem[l][None], g_ffn=g_ffn[l][None], g_final=g_final[None],
            g_sub=g_sub[l][None], lam_q=lam_q[l], lam_k=lam_k[l], rel_bias=rel_bias[l],
            w_in=w_in[l].astype(BF16),
            b_gate=b_gate[l][None],
            w_mem_kv=w_mem_kv[l].astype(BF16),
            w_o=jnp.stack([w_oa[l], w_ob[l], w_om[l]]).astype(BF16),
            w_out=w_out[l].astype(BF16),
            w_up=w_up[l].astype(BF16),
            w_conv=w_conv[l], b_conv=b_conv[l][None],
            w_down=w_down[l].astype(BF16),
        )
        last = l == depth - 1
        xp, po = _layer(xp, w, lam_init=lam_init, final_norm=last,
                        conv_prev=jnp.zeros((b, CONV_W - 1, ff), F32), mem=mem_prompt)
        cache = dict(a_k=cache_a_k[l].reshape(bd, a_buf, d_a), a_v=cache_a_v[l].reshape(bd, a_buf, d_a),
                     b_k=cache_b_k[l].reshape(bd, past * H_B, 2 * DH_B),
                     b_v=cache_b_v[l].reshape(bd, past * H_B, 2 * DH_B),
                     m_k=cache_mem_k[l].reshape(bd, -1, DH_M), m_v=cache_mem_v[l].reshape(bd, -1, DH_M))
        xs, so = _layer(xs, w, lam_init=lam_init, final_norm=last, conv_prev=state_ffn_conv[l], cache=cache)

        outs["pa_k"].append(po["ka"].reshape(b, a_keep, H_A, DH_A))
        outs["pa_v"].append(po["va"].reshape(b, a_keep, H_A, DH_A))
        outs["pb_k"].append(po["kb"].reshape(b, s, H_B, 2 * DH_B))
        outs["pb_v"].append(po["vb"].reshape(b, s, H_B, 2 * DH_B))
        outs["pm_k"].append(po["mk"].reshape(b, -1, H_M, DH_M))
        outs["pm_v"].append(po["mv"].reshape(b, -1, H_M, DH_M))
        outs["pconv"].append(po["conv"])
        new_k = jnp.concatenate([cache["a_k"], so["ka"]], axis=1)[:, t:]
        new_v = jnp.concatenate([cache["a_v"], so["va"]], axis=1)[:, t:]
        outs["sa_k"].append(new_k.reshape(bd, a_buf, H_A, DH_A))
        outs["sa_v"].append(new_v.reshape(bd, a_buf, H_A, DH_A))
        outs["sb_k"].append(so["kb"].reshape(bd, t, H_B, 2 * DH_B))
        outs["sb_v"].append(so["vb"].reshape(bd, t, H_B, 2 * DH_B))
        outs["sconv"].append(so["conv"])

    st = {k: jnp.stack(v) for k, v in outs.items()}
    return (xp, xs, st["pa_k"], st["pa_v"], st["pb_k"], st["pb_v"], st["pm_k"], st["pm_v"], st["pconv"],
            st["sa_k"], st["sa_v"], st["sb_k"], st["sb_v"], st["sconv"])
```

```python
import functools
import math

import jax
import jax.numpy as jnp
from jax import lax
from jax.experimental import pallas as pl
from jax.experimental.pallas import tpu as pltpu

CHUNK = 64
BAND_CHUNKS = 8
WIN_A = BAND_CHUNKS * CHUNK
REL_CLIP = 128
H_A = 8
DH_A = 64
H_B = 4
DH_B = 64
H_M = 4
DH_M = 128
N_BRANCH = 3
CONV_W = 3
EPS = 1e-6
NEG_INF = -1e30

LANES = 128
FF_CHUNK = 256
VMEM_LIMIT = 56 * 1024 * 1024

LOG2E = math.log2(math.e)

BF16 = jnp.bfloat16
F32 = jnp.float32


def _rms(x, g):
    y = x * lax.rsqrt(jnp.mean(x * x, axis=-1, keepdims=True) + EPS)
    return y * g


def _dot(a, b):
    return jnp.dot(a, b, preferred_element_type=F32)


def _dot_nt(a, b):
    return lax.dot_general(a, b, (((1,), (1,)), ((), ())), preferred_element_type=F32)


def _dot_tn(a, b):
    return lax.dot_general(a, b, (((0,), (0,)), ((), ())), preferred_element_type=F32)


def _const_spec(shape):
    nd = len(shape)
    return pl.BlockSpec(shape, lambda *_: (0,) * nd, pipeline_mode=pl.Buffered(1))


def _params(sem, flags=None):
    return pltpu.CompilerParams(dimension_semantics=sem, vmem_limit_bytes=VMEM_LIMIT, flags=flags)


def _head_rows(ref, h, n, heads, lead=(), first=0):
    return ref.at[lead + (pl.ds(first * heads + h, n, stride=heads), slice(None))]


def _norm_proj_kernel(x_ref, g_ref, w_ref, *o_refs, outs, tiles_per_seq):
    tm = x_ref.shape[0]
    last_tile = pl.program_id(0) % tiles_per_seq == tiles_per_seq - 1
    h = _rms(x_ref[...], g_ref[...]).astype(BF16)
    z_of = {}
    for o_ref, (c0, wd, sc, _, mode) in zip(o_refs, outs):
        if (c0, wd) not in z_of:
            z_of[(c0, wd)] = _dot(h, w_ref[:, c0:c0 + wd])
        z = z_of[(c0, wd)]
        if sc != 1.0:
            z = z * sc
        if mode == "by_head":
            heads = wd // LANES
            for hd in range(heads):
                _head_rows(o_ref, hd, tm, heads)[...] = z[:, hd * LANES:(hd + 1) * LANES].astype(o_ref.dtype)
        elif mode == "last_t":
            @pl.when(last_tile)
            def _(o_ref=o_ref, z=z):
                o_ref[...] = z.T.astype(o_ref.dtype)
        else:
            o_ref[...] = z.astype(o_ref.dtype)


def _norm_proj(x, g, w, outs, tm, tiles_per_seq=1):
    n, d = x.shape
    kern = functools.partial(_norm_proj_kernel, outs=outs, tiles_per_seq=tiles_per_seq)
    specs, shapes = [], []
    for _, wd, _, dt, mode in outs:
        if mode == "by_head":
            blk, rows, imap = ((wd // LANES) * tm, LANES), (wd // LANES) * n, lambda i: (i, 0)
        elif mode == "last_t":
            blk, rows, imap = (wd, tm), (n // tm // tiles_per_seq) * wd, lambda i: (i // tiles_per_seq, 0)
        else:
            blk, rows, imap = (tm, wd), n, lambda i: (i, 0)
        specs.append(pl.BlockSpec(blk, imap))
        shapes.append(jax.ShapeDtypeStruct((rows, blk[1]), dt))
    return pl.pallas_call(
        kern,
        grid=(n // tm,),
        in_specs=[
            pl.BlockSpec((tm, d), lambda i: (i, 0)),
            _const_spec((1, d)),
            _const_spec(w.shape),
        ],
        out_specs=specs,
        out_shape=shapes,
        compiler_params=_params(("arbitrary",)),
        name="norm_proj",
    )(x, g, w)


def _band_kernel(*refs, tq, seq, use_hist):
    if use_hist:
        q_ref, k_ref, v_ref, hk_ref, hv_ref, bias_ref, o_ref, kp, vp, s_scr, qt_scr = refs
    else:
        q_ref, k_ref, v_ref, bias_ref, o_ref, kp, vp, s_scr, qt_scr = refs
    win = WIN_A + tq
    nblk = seq // tq
    npair = H_A // 2

    if use_hist:
        kp[0:WIN_A, :] = hk_ref[0].T.astype(BF16)
        vp[0:WIN_A, :] = hv_ref[0].T.astype(BF16)
    else:
        kp[0:WIN_A, :] = jnp.zeros((WIN_A, kp.shape[1]), BF16)
        vp[0:WIN_A, :] = jnp.zeros((WIN_A, vp.shape[1]), BF16)
    kp[WIN_A:WIN_A + seq, :] = k_ref[0].astype(BF16)
    vp[WIN_A:WIN_A + seq, :] = v_ref[0].astype(BF16)

    feat = lax.broadcasted_iota(jnp.int32, (LANES, tq), 0)
    lo = feat < DH_A
    slabs = [slice(hp * LANES, (hp + 1) * LANES) for hp in range(npair)]

    for blk in range(nblk):
        for hp in range(npair):
            qt_scr[hp, blk] = _split_maps_t(q_ref[0, blk * tq:(blk + 1) * tq, slabs[hp]])

    def stage_a(m, hp, mask_history):
        off = m * tq if isinstance(m, int) else pl.multiple_of(m * tq, tq)
        s = _dot(kp[pl.ds(off, win), slabs[hp]], qt_scr[hp, m]) + bias_ref[hp]
        if mask_history:
            key = lax.broadcasted_iota(jnp.int32, (win, 2 * tq), 0)
            s = jnp.where(key >= (WIN_A - m * tq), s, NEG_INF)
        s_scr[hp] = s
        return jnp.max(s, axis=0, keepdims=True)

    def stage_b(m, hp, mx):
        off = pl.multiple_of(m * tq, tq)
        p = jnp.exp2(s_scr[hp] - mx)
        l = jnp.sum(p, axis=0, keepdims=True)
        o = _dot_tn(vp[pl.ds(off, win), slabs[hp]], p.astype(BF16)) / l
        o_ref[0, pl.ds(off, tq), slabs[hp]] = jnp.where(lo, o[:, :tq], o[:, tq:]).T.astype(o_ref.dtype)

    def block(mask_history, m, mx):
        for hp in range(1, npair):
            mx_next = stage_a(m, hp, mask_history)
            stage_b(m, hp - 1, mx)
            mx = mx_next
        mx_next = stage_a(jnp.minimum(m + 1, nblk - 1), 0, mask_history)
        stage_b(m, npair - 1, mx)
        return mx_next

    nmask = 0 if use_hist else min(WIN_A // tq, nblk)
    mx = stage_a(0, 0, nmask > 0)
    unroll = lambda trips: 4 if trips % 4 == 0 else (2 if trips % 2 == 0 else 1)
    mx = lax.fori_loop(0, nmask, functools.partial(block, True), mx, unroll=unroll(nmask))
    lax.fori_loop(nmask, nblk, functools.partial(block, False), mx, unroll=unroll(nblk - nmask))


def _band_attn(q, k, v, bias, tq, hist=None):
    b, seq, da = q.shape
    win = WIN_A + tq
    use_hist = hist is not None
    kern = functools.partial(_band_kernel, tq=tq, seq=seq, use_hist=use_hist)
    full = pl.BlockSpec((1, seq, da), lambda i: (i, 0, 0))
    in_specs = [full, full, full]
    args = [q, k, v]
    if use_hist:
        hspec = pl.BlockSpec((1, da, WIN_A), lambda i: (i, 0, 0))
        in_specs += [hspec, hspec]
        args += list(hist)
    in_specs.append(_const_spec(bias.shape))
    args.append(bias)
    return pl.pallas_call(
        kern,
        grid=(b,),
        in_specs=in_specs,
        out_specs=full,
        out_shape=jax.ShapeDtypeStruct((b, seq, da), BF16),
        scratch_shapes=[pltpu.VMEM((WIN_A + seq, da), BF16), pltpu.VMEM((WIN_A + seq, da), BF16),
                        pltpu.VMEM((H_A // 2, win, 2 * tq), F32),
                        pltpu.VMEM((H_A // 2, seq // tq, LANES, 2 * tq), BF16)],
        compiler_params=_params(("parallel",)),
        name="band_attn",
    )(*args)


BIAS_LANE0 = WIN_A + REL_CLIP
BIAS_WIDTH = 768


def _band_bias_kernel(rb_ref, o_ref, *, tq, win):
    z = lax.broadcasted_iota(jnp.int32, (1, BIAS_WIDTH), 1)
    j = lax.broadcasted_iota(jnp.int32, (win, tq), 0)
    r = lax.broadcasted_iota(jnp.int32, (win, tq), 1)
    kc = j // CHUNK
    qc = r // CHUNK
    band = (kc >= qc) & (kc <= qc + BAND_CHUNKS)
    for h in range(H_A):
        row = jnp.where(z >= 2 * REL_CLIP, rb_ref[h:h + 1, 2 * REL_CLIP:2 * REL_CLIP + 1], rb_ref[h:h + 1, :])
        skew = pltpu.roll(jnp.broadcast_to(row, (win, BIAS_WIDTH)), 0, axis=1, stride=1, stride_axis=0)
        tbl = jnp.where(band, skew[:, BIAS_LANE0:BIAS_LANE0 + tq] * LOG2E, NEG_INF)
        o_ref[h // 2, :, (h % 2) * tq:(h % 2 + 1) * tq] = tbl


def _band_bias_table(rel_bias, tq):
    win = WIN_A + tq
    assert BIAS_LANE0 + tq <= BIAS_WIDTH and win <= BIAS_WIDTH
    rb = jnp.pad(rel_bias.astype(F32), ((0, 0), (0, BIAS_WIDTH - rel_bias.shape[1])))
    return pl.pallas_call(
        functools.partial(_band_bias_kernel, tq=tq, win=win),
        out_shape=jax.ShapeDtypeStruct((H_A // 2, win, 2 * tq), F32),
        compiler_params=pltpu.CompilerParams(vmem_limit_bytes=VMEM_LIMIT),
        name="band_bias",
    )(rb)


def _split_maps_t(q):
    qt = q.astype(F32).T
    feat = lax.broadcasted_iota(jnp.int32, qt.shape, 0)
    lo = feat < DH_B
    return jnp.concatenate([jnp.where(lo, qt, 0.0), jnp.where(lo, 0.0, qt)], axis=1).astype(BF16)


def _key_query_diff(tk, tq):
    c = lax.broadcasted_iota(jnp.int32, (tk, 2 * tq), 0)
    r = lax.broadcasted_iota(jnp.int32, (tk, 2 * tq), 1)
    rr = jnp.where(r >= tq, r - tq, r)
    return rr, c


def _alibi_slope(h):
    return 2.0 ** (-8.0 * (h + 1) / H_B) * LOG2E


def _online_update(s, vb, shift, carry):
    m, l, acc = carry
    m_new = jnp.maximum(m, jnp.max(s, axis=0, keepdims=True) + shift)
    alpha = jnp.exp2(m - m_new)
    p = jnp.exp2(s - (m_new - shift))
    l = alpha * l + jnp.sum(p, axis=0, keepdims=True)
    acc = alpha * acc + _dot_tn(vb, p.astype(BF16))
    return m_new, l, acc


def _diff_finish_t(carry, tq, lam, gsub, lam_init):
    _, l, acc = carry
    o = acc / l
    od = (o[:, :tq] - lam * o[:, tq:]).T
    return _rms(od, gsub) * (1.0 - lam_init)


def _diff_lambda(lq_ref, lk_ref, lam_init):
    e = jnp.exp(jnp.sum(lq_ref[...] * lk_ref[...], axis=-1, keepdims=True))
    return e[0:1] - e[1:2] + lam_init


def _diff_init_t(tq):
    return (jnp.full((1, 2 * tq), NEG_INF, F32), jnp.zeros((1, 2 * tq), F32),
            jnp.zeros((LANES, 2 * tq), F32))


def _diff_prompt_kernel(q_ref, k_ref, v_ref, lq_ref, lk_ref, gsub_ref, o_ref,
                        bias_scr, qq_scr, s0_scr, s1_scr, acc_scr, *, tq, lam_init):
    qi = pl.program_id(1)
    slabs = [slice(h * LANES, (h + 1) * LANES) for h in range(H_B)]

    @pl.when(qi == 0)
    def _():
        rr, c = _key_query_diff(tq, tq)
        d = (rr - c).astype(F32)
        allowed = (c // CHUNK) <= (rr // CHUNK)
        for h in range(H_B):
            bias_scr[0, h] = -_alibi_slope(h) * d
            bias_scr[1, h] = jnp.where(allowed, -_alibi_slope(h) * jnp.abs(d), NEG_INF)

    for h, cs in enumerate(slabs):
        qq_scr[h] = _split_maps_t(q_ref[0, :, cs])
    acc_scr[...] = jnp.zeros(acc_scr.shape, F32)

    def stage_a(j, s_ref, ms):
        off = j * tq if isinstance(j, int) else pl.multiple_of(j * tq, tq)
        diag = (j == qi).astype(jnp.int32)
        out = []
        for h in range(H_B):
            s = _dot(k_ref[0, pl.ds(off, tq), slabs[h]], qq_scr[h]) + bias_scr[diag, h]
            s_ref[h] = s
            shift = -_alibi_slope(h) * ((qi - j) * tq).astype(F32)
            m_new = jnp.maximum(ms[h], jnp.max(s, axis=0, keepdims=True) + shift)
            out.append((m_new, jnp.exp2(ms[h] - m_new), m_new - shift))
        return tuple(o[0] for o in out), tuple(o[1] for o in out), tuple(o[2] for o in out)

    def stage_b(j, s_ref, ls, alphas, subs):
        off = j * tq if isinstance(j, int) else pl.multiple_of(j * tq, tq)
        out = []
        for h in range(H_B):
            p = jnp.exp2(s_ref[h] - subs[h])
            acc_scr[h] = alphas[h] * acc_scr[h] + _dot_tn(v_ref[0, pl.ds(off, tq), slabs[h]], p.astype(BF16))
            out.append(alphas[h] * ls[h] + jnp.sum(p, axis=0, keepdims=True))
        return tuple(out)

    def finish(ls):
        lam = _diff_lambda(lq_ref, lk_ref, lam_init)
        for h, cs in enumerate(slabs):
            o_ref[0, :, cs] = _diff_finish_t((None, ls[h], acc_scr[h]), tq, lam, gsub_ref[...],
                                             lam_init).astype(o_ref.dtype)

    row = lambda val: tuple(jnp.full((1, 2 * tq), val, F32) for _ in range(H_B))
    ms, alphas, subs = stage_a(0, s0_scr, row(NEG_INF))

    def two_blocks(jj, carry):
        ms, ls, alphas, subs = carry
        j = 2 * jj
        ms, alphas1, subs1 = stage_a(j + 1, s1_scr, ms)
        ls = stage_b(j, s0_scr, ls, alphas, subs)
        ms, alphas, subs = stage_a(j + 2, s0_scr, ms)
        ls = stage_b(j + 1, s1_scr, ls, alphas1, subs1)
        return ms, ls, alphas, subs

    ms, ls, alphas, subs = lax.fori_loop(0, qi // 2, two_blocks, (ms, row(0.0), alphas, subs))

    @pl.when(qi % 2 == 1)
    def _():
        _, alphas1, subs1 = stage_a(qi, s1_scr, ms)
        finish(stage_b(qi, s1_scr, stage_b(qi - 1, s0_scr, ls, alphas, subs), alphas1, subs1))

    @pl.when(qi % 2 == 0)
    def _():
        finish(stage_b(qi, s0_scr, ls, alphas, subs))


def _diff_attn_prompt(q, k, v, lam_q, lam_k, g_sub, tq, lam_init):
    b, seq, db = q.shape
    kern = functools.partial(_diff_prompt_kernel, tq=tq, lam_init=lam_init)
    kv_spec = pl.BlockSpec((1, seq, db), lambda i, j: (i, 0, 0))
    q_spec = pl.BlockSpec((1, tq, db), lambda i, j: (i, j, 0))
    return pl.pallas_call(
        kern,
        grid=(b, seq // tq),
        in_specs=[q_spec, kv_spec, kv_spec,
                  _const_spec(lam_q.shape), _const_spec(lam_k.shape), _const_spec(g_sub.shape)],
        out_specs=q_spec,
        out_shape=jax.ShapeDtypeStruct((b, seq, db), BF16),
        scratch_shapes=[pltpu.VMEM((2, H_B, tq, 2 * tq), F32), pltpu.VMEM((H_B, LANES, 2 * tq), BF16),
                        pltpu.VMEM((H_B, tq, 2 * tq), F32), pltpu.VMEM((H_B, tq, 2 * tq), F32),
                        pltpu.VMEM((H_B, LANES, 2 * tq), F32)],
        compiler_params=_params(("parallel", "arbitrary")),
        name="diff_attn_prompt",
    )(q, k, v, lam_q, lam_k, g_sub)


def _diff_sample_kernel(q_ref, kc_ref, vc_ref, kn_ref, vn_ref, lq_ref, lk_ref, gsub_ref, o_ref,
                        m_s, l_s, acc_s, s_scr, *, t, past, tk, ts, lam_init):
    ci = pl.program_id(1)

    @pl.when(ci == 0)
    def _():
        for h in range(H_B):
            m_s[h], l_s[h], acc_s[h] = _diff_init_t(t)

    slabs = [slice(h * LANES, (h + 1) * LANES) for h in range(H_B)]
    qqts = [_split_maps_t(q_ref[0, :, cs]) for cs in slabs]
    rr, c = _key_query_diff(ts, t)
    d = (rr - c).astype(F32)
    rows = lambda ref, h, n, first=0: _head_rows(ref, h, n, H_B, lead=(0,), first=first)[...].astype(BF16)
    items = [(sub, h) for sub in range(tk // ts) for h in range(H_B)]

    def stage_a(i):
        sub, h = items[i]
        s = _dot(rows(kc_ref, h, ts, sub * ts), qqts[h]) - _alibi_slope(h) * d
        s_scr[i % 2] = s
        shift = -_alibi_slope(h) * (past - ci * tk - sub * ts).astype(F32)
        m_new = jnp.maximum(m_s[h], jnp.max(s, axis=0, keepdims=True) + shift)
        alpha = jnp.exp2(m_s[h] - m_new)
        m_s[h] = m_new
        return alpha, m_new - shift

    def stage_b(i, alpha, ref_max):
        sub, h = items[i]
        p = jnp.exp2(s_scr[i % 2] - ref_max)
        l_s[h] = alpha * l_s[h] + jnp.sum(p, axis=0, keepdims=True)
        acc_s[h] = alpha * acc_s[h] + _dot_tn(rows(vc_ref, h, ts, sub * ts), p.astype(BF16))

    pend = stage_a(0)
    for i in range(1, len(items)):
        nxt = stage_a(i)
        stage_b(i - 1, *pend)
        pend = nxt
    stage_b(len(items) - 1, *pend)

    @pl.when(ci == past // tk - 1)
    def _():
        rn, cn = _key_query_diff(t, t)
        dn = jnp.abs((rn - cn).astype(F32))
        lam = _diff_lambda(lq_ref, lk_ref, lam_init)
        for h, cs in enumerate(slabs):
            s = _dot(rows(kn_ref, h, t), qqts[h]) - _alibi_slope(h) * dn
            carry = _online_update(s, rows(vn_ref, h, t), 0.0, (m_s[h], l_s[h], acc_s[h]))
            o_ref[0, :, cs] = _diff_finish_t(carry, t, lam, gsub_ref[...], lam_init).astype(o_ref.dtype)


def _diff_attn_sample(q, kc, vc, kn, vn, lam_q, lam_k, g_sub, tk, ts, lam_init):
    b, t, db = q.shape
    past = kc.shape[1] // H_B
    kern = functools.partial(_diff_sample_kernel, t=t, past=past, tk=tk, ts=ts, lam_init=lam_init)
    new_spec = pl.BlockSpec((1, t, db), lambda i, j: (i, 0, 0))
    new_kv_spec = pl.BlockSpec((1, t * H_B, LANES), lambda i, j: (i, 0, 0))
    cache_spec = pl.BlockSpec((1, tk * H_B, LANES), lambda i, j: (i, j, 0))
    return pl.pallas_call(
        kern,
        grid=(b, past // tk),
        in_specs=[new_spec, cache_spec, cache_spec, new_kv_spec, new_kv_spec,
                  _const_spec(lam_q.shape), _const_spec(lam_k.shape), _const_spec(g_sub.shape)],
        out_specs=new_spec,
        out_shape=jax.ShapeDtypeStruct((b, t, db), BF16),
        scratch_shapes=[pltpu.VMEM((H_B, 1, 2 * t), F32), pltpu.VMEM((H_B, 1, 2 * t), F32),
                        pltpu.VMEM((H_B, LANES, 2 * t), F32), pltpu.VMEM((2, ts, 2 * t), F32)],
        compiler_params=_params(("parallel", "arbitrary")),
        name="diff_attn_sample",
    )(q, kc, vc, kn, vn, lam_q, lam_k, g_sub)


def _mem_kernel(q_ref, mk_ref, mv_ref, o_ref):
    slabs = [slice(h * LANES, (h + 1) * LANES) for h in range(H_M)]
    nm = mk_ref.shape[1] // H_M
    rows = lambda ref, h: _head_rows(ref, h, nm, H_M, lead=(0,))[...].astype(BF16)
    scores = [_dot_nt(q_ref[0, :, cs], rows(mk_ref, h)) for h, cs in enumerate(slabs)]
    for h, cs in enumerate(slabs):
        mx = jnp.max(scores[h], axis=-1, keepdims=True)
        p = jnp.exp2(scores[h] - mx)
        l = jnp.sum(p, axis=-1, keepdims=True)
        o_ref[0, :, cs] = (_dot(p.astype(BF16), rows(mv_ref, h)) / l).astype(o_ref.dtype)


def _mem_attn(q, mk, mv, tq):
    b, seq, dm = q.shape
    kv_spec = pl.BlockSpec((1,) + mk.shape[1:], lambda i, j: (i, 0, 0))
    q_spec = pl.BlockSpec((1, tq, dm), lambda i, j: (i, j, 0))
    return pl.pallas_call(
        _mem_kernel,
        grid=(b, seq // tq),
        in_specs=[q_spec, kv_spec, kv_spec],
        out_specs=q_spec,
        out_shape=jax.ShapeDtypeStruct((b, seq, dm), BF16),
        compiler_params=_params(("parallel", "parallel")),
        name="mem_attn",
    )(q, mk, mv)


def _merge_kernel(x_ref, oa_ref, ob_ref, om_ref, g_ref, win_ref, bg_ref, wo_ref, wout_ref, o_ref, m_scr,
                  *, tn):
    x = x_ref[...]
    h = _rms(x, g_ref[...]).astype(BF16)
    o_refs = (oa_ref, ob_ref, om_ref)
    d = x.shape[1]
    for c0 in range(0, d, tn):
        cs = slice(c0, c0 + tn)
        m = None
        for br in range(N_BRANCH):
            g0 = br * d + c0
            goff = win_ref.shape[1] - N_BRANCH * d
            gate = jax.nn.sigmoid(_dot(h, win_ref[:, goff + g0:goff + g0 + tn]) + bg_ref[:, g0:g0 + tn])
            term = gate * _dot(o_refs[br][...], wo_ref[br, :, cs])
            m = term if m is None else m + term
        m_scr[:, cs] = m.astype(BF16)
    o_ref[...] = x + _dot(m_scr[...], wout_ref[...])


def _merge(x, oa, ob, om, g, wg, bg, wo, wout, tm, tn=256):
    n, d = x.shape
    kern = functools.partial(_merge_kernel, tn=tn)
    tok = lambda wd: pl.BlockSpec((tm, wd), lambda i: (i, 0))
    return pl.pallas_call(
        kern,
        grid=(n // tm,),
        in_specs=[tok(d), tok(oa.shape[1]), tok(ob.shape[1]), tok(om.shape[1]), _const_spec(g.shape),
                  _const_spec(wg.shape), _const_spec(bg.shape), _const_spec(wo.shape),
                  _const_spec(wout.shape)],
        out_specs=tok(d),
        out_shape=jax.ShapeDtypeStruct((n, d), F32),
        scratch_shapes=[pltpu.VMEM((tm, d), BF16)],
        compiler_params=_params(("parallel",)),
        name="merge",
    )(x, oa, ob, om, g, wg, bg, wo, wout)


def _ffn_kernel(x_ref, g_ref, prev_ref, wup_ref, wconv_ref, bconv_ref, wdown_ref, gfin_ref,
                y_ref, conv_ref, carry, u_scr, *, tm, nch, nseq, final_norm):
    if nseq == 1:
        @pl.when(pl.program_id(1) == 0)
        def _():
            carry[...] = prev_ref[0, 0]

    x = x_ref[0]
    h = _rms(x, g_ref[...]).astype(BF16)
    row = lax.broadcasted_iota(jnp.int32, (tm, FF_CHUNK), 0)
    seq_rows = tm // nseq
    ff = nch * FF_CHUNK
    for ch in range(nch):
        cs = slice(ch * FF_CHUNK, (ch + 1) * FF_CHUNK)
        a = _dot(h, wup_ref[:, cs])
        gate = _dot(h, wup_ref[:, ff + ch * FF_CHUNK:ff + (ch + 1) * FF_CHUNK])
        a1 = pltpu.roll(a, 1, axis=0)
        a2 = pltpu.roll(a, 2, axis=0)
        for k in range(nseq):
            before = carry if nseq == 1 else prev_ref.at[0, k]
            p2 = before[0:1, cs]
            p1 = before[1:2, cs]
            a1 = jnp.where(row == k * seq_rows, p1, a1)
            a2 = jnp.where(row == k * seq_rows, p2, jnp.where(row == k * seq_rows + 1, p1, a2))
        conv = bconv_ref[:, cs] + a2 * wconv_ref[0:1, cs]
        conv = conv + a1 * wconv_ref[1:2, cs]
        conv = conv + a * wconv_ref[2:3, cs]
        if nseq == 1:
            carry[:, cs] = a[tm - (CONV_W - 1):, :]
        else:
            for k in range(nseq):
                conv_ref[0, k, :, cs] = a[(k + 1) * seq_rows - (CONV_W - 1):(k + 1) * seq_rows, :]
        u_scr[:, cs] = (jax.nn.gelu(conv) * gate).astype(BF16)
    y = x + _dot(u_scr[...], wdown_ref[...])
    if final_norm:
        y = _rms(y, gfin_ref[...])
    y_ref[0] = y
    if nseq == 1:
        conv_ref[0, 0] = carry[...]


def _conv_ffn(x, g, prev, wup, wconv, bconv, wdown, gfin, tm, final_norm):
    b, seq, d = x.shape
    ff = wdown.shape[0]
    nch = ff // FF_CHUNK
    nseq = max(tm // seq, 1)
    groups = b // nseq
    assert (seq % tm == 0) if nseq == 1 else (tm % seq == 0 and b % nseq == 0)
    kern = functools.partial(_ffn_kernel, tm=tm, nch=nch, nseq=nseq, final_norm=final_norm)
    tok = pl.BlockSpec((1, tm, d), lambda i, j: (i, j, 0))
    state = pl.BlockSpec((1, nseq, CONV_W - 1, ff), lambda i, j: (i, 0, 0, 0))
    y, conv = pl.pallas_call(
        kern,
        grid=(groups, nseq * seq // tm),
        in_specs=[tok, _const_spec(g.shape), state, _const_spec(wup.shape), _const_spec(wconv.shape),
                  _const_spec(bconv.shape), _const_spec(wdown.shape), _const_spec(gfin.shape)],
        out_specs=[tok, state],
        out_shape=[jax.ShapeDtypeStruct((groups, nseq * seq, d), F32),
                   jax.ShapeDtypeStruct((groups, nseq, CONV_W - 1, ff), F32)],
        scratch_shapes=[pltpu.VMEM((CONV_W - 1, ff), F32), pltpu.VMEM((tm, ff), BF16)],
        compiler_params=_params(("parallel", "arbitrary")),
        name="conv_ffn",
    )(x.reshape(groups, nseq * seq, d), g, prev.reshape(groups, nseq, CONV_W - 1, ff),
      wup, wconv, bconv, wdown, gfin)
    return y.reshape(b, seq, d), conv.reshape(b, CONV_W - 1, ff)


def _token_tile(n, pref):
    return pref if n % pref == 0 else n


def _ffn_tile(b, t):
    if t % 1024 == 0:
        return 1024
    if t % 512 == 0 or (512 % t == 0 and (b * t) % 512 == 0):
        return 512
    return t


def _layer(x, w, *, lam_init, final_norm, conv_prev, mem=None, cache=None):
    b, t, d = x.shape
    n = b * t
    tm = _token_tile(n, 512)
    d_a, d_b, d_m = H_A * DH_A, H_B * 2 * DH_B, H_M * DH_M
    prompt = cache is None
    assert not prompt or (tm == WIN_A and t % tm == 0)
    kv_a = [(BF16, "rows"), (F32, "last_t")] if prompt else [(F32, "rows")]
    kv_b = [(BF16, "rows"), (F32, "by_head")] if prompt else [(F32, "by_head")]
    outs = ([(0, d_a, DH_A ** -0.5 * LOG2E, BF16, "rows")]
            + [(d_a, d_a, 1.0, dt, mode) for dt, mode in kv_a]
            + [(2 * d_a, d_a, 1.0, dt, mode) for dt, mode in kv_a]
            + [(3 * d_a, d_b, DH_B ** -0.5 * LOG2E, BF16, "rows")]
            + [(3 * d_a + d_b, d_b, 1.0, dt, mode) for dt, mode in kv_b]
            + [(3 * d_a + 2 * d_b, d_b, 1.0, dt, mode) for dt, mode in kv_b]
            + [(3 * d_a + 3 * d_b, d_m, DH_M ** -0.5 * LOG2E, BF16, "rows")])
    res = _norm_proj(x.reshape(n, d), w["g_mix"], w["w_in"], outs, tm, tiles_per_seq=max(t // tm, 1))
    res = [a.reshape(b, -1, a.shape[-1]) for a in res]
    if prompt:
        qa, ka, ka_keep, va, va_keep, qb, kb_bf, kb, vb_bf, vb, qm = res
    else:
        qa, ka, va, qb, kb, vb, qm = res
        ka_keep, va_keep = ka, va

    if cache is None:
        tq_a = 128
        oa = _band_attn(qa, ka, va, _band_bias_table(w["rel_bias"], tq_a), tq_a)
        ob = _diff_attn_prompt(qb, kb_bf, vb_bf, w["lam_q"], w["lam_k"], w["g_sub"], 256, lam_init)
        nm = mem.shape[1]
        mk, mv = _norm_proj(mem.reshape(b * nm, d), w["g_mem"], w["w_mem_kv"],
                            [(0, d_m, 1.0, F32, "by_head"), (d_m, d_m, 1.0, F32, "by_head")],
                            _token_tile(b * nm, 512))
        mk = mk.reshape(b, nm * H_M, DH_M)
        mv = mv.reshape(b, nm * H_M, DH_M)
        om = _mem_attn(qm, mk, mv, _token_tile(t, 512))
    else:
        oa = _band_attn(qa, ka, va, _band_bias_table(w["rel_bias"], t), t,
                        hist=(cache["a_k_t"], cache["a_v_t"]))
        ob = _diff_attn_sample(qb, cache["b_k"], cache["b_v"], kb, vb,
                               w["lam_q"], w["lam_k"], w["g_sub"], 2048, 512, lam_init)
        mk, mv = cache["m_k"], cache["m_v"]
        om = _mem_attn(qm, mk, mv, t)

    flat = lambda a: a.reshape(n, a.shape[-1])
    x1 = _merge(x.reshape(n, d), flat(oa), flat(ob), flat(om), w["g_mix"], w["w_in"], w["b_gate"],
                w["w_o"], w["w_out"], tm)
    y, conv = _conv_ffn(x1.reshape(b, t, d), w["g_ffn"], conv_prev, w["w_up"], w["w_conv"], w["b_conv"],
                        w["w_down"], w["g_final"], _ffn_tile(b, t), final_norm)
    return y, dict(ka=ka_keep, va=va_keep, kb=kb, vb=vb, mk=mk, mv=mv, conv=conv)


def kernel(x_prompt, x_sample, mem_prompt, cache_a_k, cache_a_v, cache_b_k, cache_b_v, cache_mem_k, cache_mem_v, state_ffn_conv, g_mix, w_in, b_gate, rel_bias, lam_q, lam_k, g_sub, g_mem, w_mem_kv, w_oa, w_ob, w_om, w_out, g_ffn, w_up, w_conv, b_conv, w_down, g_final):
    depth, d, _ = w_in.shape
    b, s, _ = x_prompt.shape
    bd, t, _ = x_sample.shape
    past = cache_b_k.shape[2]
    a_buf = cache_a_k.shape[2]
    ff = w_down.shape[1]
    d_a, d_b, d_m = H_A * DH_A, H_B * 2 * DH_B, H_M * DH_M
    a_keep = min(WIN_A, s)
    assert a_buf == WIN_A and a_keep == WIN_A and t == CHUNK and past % CHUNK == 0 and ff % FF_CHUNK == 0
    nch = ff // FF_CHUNK

    xp, xs = x_prompt, x_sample
    outs = {k: [] for k in ("pa_k", "pa_v", "pb_k", "pb_v", "pm_k", "pm_v", "pconv",
                            "sa_k", "sa_v", "sb_k", "sb_v", "sconv")}
    for l in range(depth):
        lam_init = 0.8 - 0.6 * math.exp(-0.3 * l)
        w = dict(
            g_mix=g_mix[l][None], g_mem=g_mem[l][None], g_ffn=g_ffn[l][None], g_final=g_final[None],
            g_sub=g_sub[l][None], lam_q=lam_q[l], lam_k=lam_k[l], rel_bias=rel_bias[l],
            w_in=w_in[l].astype(BF16),
            b_gate=b_gate[l][None],
            w_mem_kv=w_mem_kv[l].astype(BF16),
            w_o=jnp.stack([w_oa[l], w_ob[l], w_om[l]]).astype(BF16),
            w_out=w_out[l].astype(BF16),
            w_up=w_up[l].astype(BF16),
            w_conv=w_conv[l], b_conv=b_conv[l][None],
            w_down=w_down[l].astype(BF16),
        )
        last = l == depth - 1
        xp, po = _layer(xp, w, lam_init=lam_init, final_norm=last,
                        conv_prev=jnp.zeros((b, CONV_W - 1, ff), F32), mem=mem_prompt)
        feat_major = lambda a: a.transpose(0, 2, 3, 1).reshape(bd, d_a, a_buf)
        cache = dict(a_k=cache_a_k[l].reshape(bd, a_buf, d_a), a_v=cache_a_v[l].reshape(bd, a_buf, d_a),
                     a_k_t=feat_major(cache_a_k[l]), a_v_t=feat_major(cache_a_v[l]),
                     b_k=cache_b_k[l].reshape(bd, past * H_B, 2 * DH_B),
                     b_v=cache_b_v[l].reshape(bd, past * H_B, 2 * DH_B),
                     m_k=cache_mem_k[l].reshape(bd, -1, DH_M), m_v=cache_mem_v[l].reshape(bd, -1, DH_M))
        xs, so = _layer(xs, w, lam_init=lam_init, final_norm=last, conv_prev=state_ffn_conv[l], cache=cache)

        tok_major = lambda a: a.reshape(b, H_A, DH_A, a_keep).transpose(0, 3, 1, 2)
        outs["pa_k"].append(tok_major(po["ka"]))
        outs["pa_v"].append(tok_major(po["va"]))
        outs["pb_k"].append(po["kb"].reshape(b, s, H_B, 2 * DH_B))
        outs["pb_v"].append(po["vb"].reshape(b, s, H_B, 2 * DH_B))
        outs["pm_k"].append(po["mk"].reshape(b, -1, H_M, DH_M))
        outs["pm_v"].append(po["mv"].reshape(b, -1, H_M, DH_M))
        outs["pconv"].append(po["conv"])
        new_k = jnp.concatenate([cache["a_k"], so["ka"]], axis=1)[:, t:]
        new_v = jnp.concatenate([cache["a_v"], so["va"]], axis=1)[:, t:]
        outs["sa_k"].append(new_k.reshape(bd, a_buf, H_A, DH_A))
        outs["sa_v"].append(new_v.reshape(bd, a_buf, H_A, DH_A))
        outs["sb_k"].append(so["kb"].reshape(bd, t, H_B, 2 * DH_B))
        outs["sb_v"].append(so["vb"].reshape(bd, t, H_B, 2 * DH_B))
        outs["sconv"].append(so["conv"])

    st = {k: jnp.stack(v) for k, v in outs.items()}
    return (xp, xs, st["pa_k"], st["pa_v"], st["pb_k"], st["pb_v"], st["pm_k"], st["pm_v"], st["pconv"],
            st["sa_k"], st["sa_v"], st["sb_k"], st["sb_v"], st["sconv"])
```

```python
import functools
import math

import jax
import jax.numpy as jnp
from jax import lax
from jax.experimental import pallas as pl
from jax.experimental.pallas import tpu as pltpu

CHUNK = 64
BAND_CHUNKS = 8
WIN_A = BAND_CHUNKS * CHUNK
REL_CLIP = 128
H_A = 8
DH_A = 64
H_B = 4
DH_B = 64
H_M = 4
DH_M = 128
N_BRANCH = 3
CONV_W = 3
EPS = 1e-6
NEG_INF = -1e30

LANES = 128
FF_CHUNK = 256
VMEM_LIMIT = 56 * 1024 * 1024

LOG2E = math.log2(math.e)

BF16 = jnp.bfloat16
F32 = jnp.float32


def _rms(x, g):
    y = x * lax.rsqrt(jnp.mean(x * x, axis=-1, keepdims=True) + EPS)
    return y * g


def _dot(a, b):
    return jnp.dot(a, b, preferred_element_type=F32)


def _dot_nt(a, b):
    return lax.dot_general(a, b, (((1,), (1,)), ((), ())), preferred_element_type=F32)


def _dot_tn(a, b):
    return lax.dot_general(a, b, (((0,), (0,)), ((), ())), preferred_element_type=F32)


def _const_spec(shape):
    nd = len(shape)
    return pl.BlockSpec(shape, lambda *_: (0,) * nd, pipeline_mode=pl.Buffered(1))


def _params(sem, flags=None):
    return pltpu.CompilerParams(dimension_semantics=sem, vmem_limit_bytes=VMEM_LIMIT, flags=flags)


def _head_rows(ref, h, n, heads, lead=(), first=0):
    return ref.at[lead + (pl.ds(first * heads + h, n, stride=heads), slice(None))]


def _norm_proj_kernel(x_ref, g_ref, w_ref, *o_refs, outs, tiles_per_seq):
    tm = x_ref.shape[0]
    last_tile = pl.program_id(0) % tiles_per_seq == tiles_per_seq - 1
    h = _rms(x_ref[...], g_ref[...]).astype(BF16)
    z_of = {}
    transposed = []
    for o_ref, (c0, wd, sc, _, mode) in zip(o_refs, outs):
        if (c0, wd) not in z_of:
            z_of[(c0, wd)] = _dot(h, w_ref[:, c0:c0 + wd])
        z = z_of[(c0, wd)]
        if sc != 1.0:
            z = z * sc
        if mode == "by_head":
            heads = wd // LANES
            for hd in range(heads):
                _head_rows(o_ref, hd, tm, heads)[...] = z[:, hd * LANES:(hd + 1) * LANES].astype(o_ref.dtype)
        elif mode == "last_t":
            transposed.append((o_ref, z))
        else:
            o_ref[...] = z.astype(o_ref.dtype)

    if transposed:
        @pl.when(last_tile)
        def _():
            for o_ref, z in transposed:
                o_ref[...] = z.T.astype(o_ref.dtype)


def _norm_proj(x, g, w, outs, tm, tiles_per_seq=1):
    n, d = x.shape
    kern = functools.partial(_norm_proj_kernel, outs=outs, tiles_per_seq=tiles_per_seq)
    specs, shapes = [], []
    for _, wd, _, dt, mode in outs:
        if mode == "by_head":
            blk, rows, imap = ((wd // LANES) * tm, LANES), (wd // LANES) * n, lambda i: (i, 0)
        elif mode == "last_t":
            blk, rows, imap = (wd, tm), (n // tm // tiles_per_seq) * wd, lambda i: (i // tiles_per_seq, 0)
        else:
            blk, rows, imap = (tm, wd), n, lambda i: (i, 0)
        specs.append(pl.BlockSpec(blk, imap))
        shapes.append(jax.ShapeDtypeStruct((rows, blk[1]), dt))
    return pl.pallas_call(
        kern,
        grid=(n // tm,),
        in_specs=[
            pl.BlockSpec((tm, d), lambda i: (i, 0)),
            _const_spec((1, d)),
            _const_spec(w.shape),
        ],
        out_specs=specs,
        out_shape=shapes,
        compiler_params=_params(("arbitrary",)),
        name="norm_proj",
    )(x, g, w)


def _band_kernel(*refs, tq, seq, use_hist):
    if use_hist:
        q_ref, k_ref, v_ref, hk_ref, hv_ref, bias_ref, o_ref, kp, vp, s_scr, qt_scr = refs
    else:
        q_ref, k_ref, v_ref, bias_ref, o_ref, kp, vp, s_scr, qt_scr = refs
    win = WIN_A + tq
    nblk = seq // tq
    npair = H_A // 2

    if use_hist:
        kp[0:WIN_A, :] = hk_ref[0].T.astype(BF16)
        vp[0:WIN_A, :] = hv_ref[0].T.astype(BF16)
    else:
        kp[0:WIN_A, :] = jnp.zeros((WIN_A, kp.shape[1]), BF16)
        vp[0:WIN_A, :] = jnp.zeros((WIN_A, vp.shape[1]), BF16)
    kp[WIN_A:WIN_A + seq, :] = k_ref[0].astype(BF16)
    vp[WIN_A:WIN_A + seq, :] = v_ref[0].astype(BF16)

    feat = lax.broadcasted_iota(jnp.int32, (LANES, tq), 0)
    lo = feat < DH_A
    slabs = [slice(hp * LANES, (hp + 1) * LANES) for hp in range(npair)]

    for blk in range(nblk):
        for hp in range(npair):
            qt_scr[hp, blk] = _split_maps_t(q_ref[0, blk * tq:(blk + 1) * tq, slabs[hp]])

    def stage_a(m, hp, mask_history):
        off = m * tq if isinstance(m, int) else pl.multiple_of(m * tq, tq)
        s = _dot(kp[pl.ds(off, win), slabs[hp]], qt_scr[hp, m]) + bias_ref[hp]
        if mask_history:
            key = lax.broadcasted_iota(jnp.int32, (win, 2 * tq), 0)
            s = jnp.where(key >= (WIN_A - m * tq), s, NEG_INF)
        s_scr[hp] = s
        return jnp.max(s, axis=0, keepdims=True)

    def stage_b(m, hp, mx):
        off = pl.multiple_of(m * tq, tq)
        p = jnp.exp2(s_scr[hp] - mx)
        l = jnp.sum(p, axis=0, keepdims=True)
        o = _dot_tn(vp[pl.ds(off, win), slabs[hp]], p.astype(BF16)) / l
        o_ref[0, pl.ds(off, tq), slabs[hp]] = jnp.where(lo, o[:, :tq], o[:, tq:]).T.astype(o_ref.dtype)

    def block(mask_history, m, mx):
        for hp in range(1, npair):
            mx_next = stage_a(m, hp, mask_history)
            stage_b(m, hp - 1, mx)
            mx = mx_next
        mx_next = stage_a(jnp.minimum(m + 1, nblk - 1), 0, mask_history)
        stage_b(m, npair - 1, mx)
        return mx_next

    nmask = 0 if use_hist else min(WIN_A // tq, nblk)
    mx = stage_a(0, 0, nmask > 0)
    unroll = lambda trips: 4 if trips % 4 == 0 else (2 if trips % 2 == 0 else 1)
    mx = lax.fori_loop(0, nmask, functools.partial(block, True), mx, unroll=unroll(nmask))
    lax.fori_loop(nmask, nblk, functools.partial(block, False), mx, unroll=unroll(nblk - nmask))


def _band_attn(q, k, v, bias, tq, hist=None):
    b, seq, da = q.shape
    win = WIN_A + tq
    use_hist = hist is not None
    kern = functools.partial(_band_kernel, tq=tq, seq=seq, use_hist=use_hist)
    full = pl.BlockSpec((1, seq, da), lambda i: (i, 0, 0))
    in_specs = [full, full, full]
    args = [q, k, v]
    if use_hist:
        hspec = pl.BlockSpec((1, da, WIN_A), lambda i: (i, 0, 0))
        in_specs += [hspec, hspec]
        args += list(hist)
    in_specs.append(_const_spec(bias.shape))
    args.append(bias)
    return pl.pallas_call(
        kern,
        grid=(b,),
        in_specs=in_specs,
        out_specs=full,
        out_shape=jax.ShapeDtypeStruct((b, seq, da), BF16),
        scratch_shapes=[pltpu.VMEM((WIN_A + seq, da), BF16), pltpu.VMEM((WIN_A + seq, da), BF16),
                        pltpu.VMEM((H_A // 2, win, 2 * tq), F32),
                        pltpu.VMEM((H_A // 2, seq // tq, LANES, 2 * tq), BF16)],
        compiler_params=_params(("parallel",)),
        name="band_attn",
    )(*args)


BIAS_LANE0 = WIN_A + REL_CLIP
BIAS_WIDTH = 768


def _band_bias_kernel(rb_ref, o_ref, *, tq, win):
    z = lax.broadcasted_iota(jnp.int32, (1, BIAS_WIDTH), 1)
    j = lax.broadcasted_iota(jnp.int32, (win, tq), 0)
    r = lax.broadcasted_iota(jnp.int32, (win, tq), 1)
    kc = j // CHUNK
    qc = r // CHUNK
    band = (kc >= qc) & (kc <= qc + BAND_CHUNKS)
    for h in range(H_A):
        row = jnp.where(z >= 2 * REL_CLIP, rb_ref[h:h + 1, 2 * REL_CLIP:2 * REL_CLIP + 1], rb_ref[h:h + 1, :])
        skew = pltpu.roll(jnp.broadcast_to(row, (win, BIAS_WIDTH)), 0, axis=1, stride=1, stride_axis=0)
        tbl = jnp.where(band, skew[:, BIAS_LANE0:BIAS_LANE0 + tq] * LOG2E, NEG_INF)
        o_ref[h // 2, :, (h % 2) * tq:(h % 2 + 1) * tq] = tbl


def _band_bias_table(rel_bias, tq):
    win = WIN_A + tq
    assert BIAS_LANE0 + tq <= BIAS_WIDTH and win <= BIAS_WIDTH
    rb = jnp.pad(rel_bias.astype(F32), ((0, 0), (0, BIAS_WIDTH - rel_bias.shape[1])))
    return pl.pallas_call(
        functools.partial(_band_bias_kernel, tq=tq, win=win),
        out_shape=jax.ShapeDtypeStruct((H_A // 2, win, 2 * tq), F32),
        compiler_params=pltpu.CompilerParams(vmem_limit_bytes=VMEM_LIMIT),
        name="band_bias",
    )(rb)


def _split_maps_t(q):
    qt = q.astype(F32).T
    feat = lax.broadcasted_iota(jnp.int32, qt.shape, 0)
    lo = feat < DH_B
    return jnp.concatenate([jnp.where(lo, qt, 0.0), jnp.where(lo, 0.0, qt)], axis=1).astype(BF16)


def _key_query_diff(tk, tq):
    c = lax.broadcasted_iota(jnp.int32, (tk, 2 * tq), 0)
    r = lax.broadcasted_iota(jnp.int32, (tk, 2 * tq), 1)
    rr = jnp.where(r >= tq, r - tq, r)
    return rr, c


def _alibi_slope(h):
    return 2.0 ** (-8.0 * (h + 1) / H_B) * LOG2E


def _online_update(s, vb, shift, carry):
    m, l, acc = carry
    m_new = jnp.maximum(m, jnp.max(s, axis=0, keepdims=True) + shift)
    alpha = jnp.exp2(m - m_new)
    p = jnp.exp2(s - (m_new - shift))
    l = alpha * l + jnp.sum(p, axis=0, keepdims=True)
    acc = alpha * acc + _dot_tn(vb, p.astype(BF16))
    return m_new, l, acc


def _diff_finish_t(carry, tq, lam, gsub, lam_init):
    _, l, acc = carry
    o = acc / l
    od = (o[:, :tq] - lam * o[:, tq:]).T
    return _rms(od, gsub) * (1.0 - lam_init)


def _diff_lambda(lq_ref, lk_ref, lam_init):
    e = jnp.exp(jnp.sum(lq_ref[...] * lk_ref[...], axis=-1, keepdims=True))
    return e[0:1] - e[1:2] + lam_init


def _diff_init_t(tq):
    return (jnp.full((1, 2 * tq), NEG_INF, F32), jnp.zeros((1, 2 * tq), F32),
            jnp.zeros((LANES, 2 * tq), F32))


def _diff_prompt_kernel(q_ref, k_ref, v_ref, lq_ref, lk_ref, gsub_ref, o_ref,
                        bias_scr, qq_scr, s0_scr, s1_scr, acc_scr, *, tq, lam_init):
    qi = pl.program_id(1)
    slabs = [slice(h * LANES, (h + 1) * LANES) for h in range(H_B)]
    chains = [(bb, h) for bb in range(q_ref.shape[0]) for h in range(H_B)]

    @pl.when(qi == 0)
    def _():
        rr, c = _key_query_diff(tq, tq)
        d = (rr - c).astype(F32)
        allowed = (c // CHUNK) <= (rr // CHUNK)
        for h in range(H_B):
            bias_scr[0, h] = -_alibi_slope(h) * d
            bias_scr[1, h] = jnp.where(allowed, -_alibi_slope(h) * jnp.abs(d), NEG_INF)

    for c, (bb, h) in enumerate(chains):
        qq_scr[c] = _split_maps_t(q_ref[bb, :, slabs[h]])
    acc_scr[...] = jnp.zeros(acc_scr.shape, F32)

    def stage_a(j, s_ref, ms):
        off = j * tq if isinstance(j, int) else pl.multiple_of(j * tq, tq)
        diag = (j == qi).astype(jnp.int32)
        out = []
        for c, (bb, h) in enumerate(chains):
            s = _dot(k_ref[bb, pl.ds(off, tq), slabs[h]], qq_scr[c]) + bias_scr[diag, h]
            s_ref[c] = s
            shift = -_alibi_slope(h) * ((qi - j) * tq).astype(F32)
            m_new = jnp.maximum(ms[c], jnp.max(s, axis=0, keepdims=True) + shift)
            out.append((m_new, jnp.exp2(ms[c] - m_new), m_new - shift))
        return tuple(o[0] for o in out), tuple(o[1] for o in out), tuple(o[2] for o in out)

    def stage_b(j, s_ref, ls, alphas, subs):
        off = j * tq if isinstance(j, int) else pl.multiple_of(j * tq, tq)
        out = []
        for c, (bb, h) in enumerate(chains):
            p = jnp.exp2(s_ref[c] - subs[c])
            acc_scr[c] = alphas[c] * acc_scr[c] + _dot_tn(v_ref[bb, pl.ds(off, tq), slabs[h]], p.astype(BF16))
            out.append(alphas[c] * ls[c] + jnp.sum(p, axis=0, keepdims=True))
        return tuple(out)

    def finish(ls):
        lam = _diff_lambda(lq_ref, lk_ref, lam_init)
        for c, (bb, h) in enumerate(chains):
            o_ref[bb, :, slabs[h]] = _diff_finish_t((None, ls[c], acc_scr[c]), tq, lam, gsub_ref[...],
                                                    lam_init).astype(o_ref.dtype)

    row = lambda val: tuple(jnp.full((1, 2 * tq), val, F32) for _ in chains)
    ms, alphas, subs = stage_a(0, s0_scr, row(NEG_INF))

    def two_blocks(jj, carry):
        ms, ls, alphas, subs = carry
        j = 2 * jj
        ms, alphas1, subs1 = stage_a(j + 1, s1_scr, ms)
        ls = stage_b(j, s0_scr, ls, alphas, subs)
        ms, alphas, subs = stage_a(j + 2, s0_scr, ms)
        ls = stage_b(j + 1, s1_scr, ls, alphas1, subs1)
        return ms, ls, alphas, subs

    ms, ls, alphas, subs = lax.fori_loop(0, qi // 2, two_blocks, (ms, row(0.0), alphas, subs))

    @pl.when(qi % 2 == 1)
    def _():
        _, alphas1, subs1 = stage_a(qi, s1_scr, ms)
        finish(stage_b(qi, s1_scr, stage_b(qi - 1, s0_scr, ls, alphas, subs), alphas1, subs1))

    @pl.when(qi % 2 == 0)
    def _():
        finish(stage_b(qi, s0_scr, ls, alphas, subs))


def _diff_attn_prompt(q, k, v, lam_q, lam_k, g_sub, tq, lam_init):
    b, seq, db = q.shape
    nbat = 2 if b % 2 == 0 else 1
    kern = functools.partial(_diff_prompt_kernel, tq=tq, lam_init=lam_init)
    kv_spec = pl.BlockSpec((nbat, seq, db), lambda i, j: (i, 0, 0))
    q_spec = pl.BlockSpec((nbat, tq, db), lambda i, j: (i, j, 0))
    return pl.pallas_call(
        kern,
        grid=(b // nbat, seq // tq),
        in_specs=[q_spec, kv_spec, kv_spec,
                  _const_spec(lam_q.shape), _const_spec(lam_k.shape), _const_spec(g_sub.shape)],
        out_specs=q_spec,
        out_shape=jax.ShapeDtypeStruct((b, seq, db), BF16),
        scratch_shapes=[pltpu.VMEM((2, H_B, tq, 2 * tq), F32), pltpu.VMEM((nbat * H_B, LANES, 2 * tq), BF16),
                        pltpu.VMEM((nbat * H_B, tq, 2 * tq), F32), pltpu.VMEM((nbat * H_B, tq, 2 * tq), F32),
                        pltpu.VMEM((nbat * H_B, LANES, 2 * tq), F32)],
        compiler_params=_params(("parallel", "arbitrary")),
        name="diff_attn_prompt",
    )(q, k, v, lam_q, lam_k, g_sub)


def _diff_sample_kernel(q_ref, kc_ref, vc_ref, kn_ref, vn_ref, lq_ref, lk_ref, gsub_ref, o_ref,
                        m_s, l_s, acc_s, s_scr, *, t, past, tk, ts, lam_init):
    ci = pl.program_id(1)

    @pl.when(ci == 0)
    def _():
        for h in range(H_B):
            m_s[h], l_s[h], acc_s[h] = _diff_init_t(t)

    slabs = [slice(h * LANES, (h + 1) * LANES) for h in range(H_B)]
    qqts = [_split_maps_t(q_ref[0, :, cs]) for cs in slabs]
    rr, c = _key_query_diff(ts, t)
    d = (rr - c).astype(F32)
    rows = lambda ref, h, n, first=0: _head_rows(ref, h, n, H_B, lead=(0,), first=first)[...].astype(BF16)
    items = [(sub, h) for sub in range(tk // ts) for h in range(H_B)]

    def stage_a(i):
        sub, h = items[i]
        s = _dot(rows(kc_ref, h, ts, sub * ts), qqts[h]) - _alibi_slope(h) * d
        s_scr[i % 2] = s
        shift = -_alibi_slope(h) * (past - ci * tk - sub * ts).astype(F32)
        m_new = jnp.maximum(m_s[h], jnp.max(s, axis=0, keepdims=True) + shift)
        alpha = jnp.exp2(m_s[h] - m_new)
        m_s[h] = m_new
        return alpha, m_new - shift

    def stage_b(i, alpha, ref_max):
        sub, h = items[i]
        p = jnp.exp2(s_scr[i % 2] - ref_max)
        l_s[h] = alpha * l_s[h] + jnp.sum(p, axis=0, keepdims=True)
        acc_s[h] = alpha * acc_s[h] + _dot_tn(rows(vc_ref, h, ts, sub * ts), p.astype(BF16))

    pend = stage_a(0)
    for i in range(1, len(items)):
        nxt = stage_a(i)
        stage_b(i - 1, *pend)
        pend = nxt
    stage_b(len(items) - 1, *pend)

    @pl.when(ci == past // tk - 1)
    def _():
        rn, cn = _key_query_diff(t, t)
        dn = jnp.abs((rn - cn).astype(F32))
        lam = _diff_lambda(lq_ref, lk_ref, lam_init)
        for h, cs in enumerate(slabs):
            s = _dot(rows(kn_ref, h, t), qqts[h]) - _alibi_slope(h) * dn
            carry = _online_update(s, rows(vn_ref, h, t), 0.0, (m_s[h], l_s[h], acc_s[h]))
            o_ref[0, :, cs] = _diff_finish_t(carry, t, lam, gsub_ref[...], lam_init).astype(o_ref.dtype)


def _diff_attn_sample(q, kc, vc, kn, vn, lam_q, lam_k, g_sub, tk, ts, lam_init):
    b, t, db = q.shape
    past = kc.shape[1] // H_B
    kern = functools.partial(_diff_sample_kernel, t=t, past=past, tk=tk, ts=ts, lam_init=lam_init)
    new_spec = pl.BlockSpec((1, t, db), lambda i, j: (i, 0, 0))
    new_kv_spec = pl.BlockSpec((1, t * H_B, LANES), lambda i, j: (i, 0, 0))
    cache_spec = pl.BlockSpec((1, tk * H_B, LANES), lambda i, j: (i, j, 0))
    return pl.pallas_call(
        kern,
        grid=(b, past // tk),
        in_specs=[new_spec, cache_spec, cache_spec, new_kv_spec, new_kv_spec,
                  _const_spec(lam_q.shape), _const_spec(lam_k.shape), _const_spec(g_sub.shape)],
        out_specs=new_spec,
        out_shape=jax.ShapeDtypeStruct((b, t, db), BF16),
        scratch_shapes=[pltpu.VMEM((H_B, 1, 2 * t), F32), pltpu.VMEM((H_B, 1, 2 * t), F32),
                        pltpu.VMEM((H_B, LANES, 2 * t), F32), pltpu.VMEM((2, ts, 2 * t), F32)],
        compiler_params=_params(("parallel", "arbitrary")),
        name="diff_attn_sample",
    )(q, kc, vc, kn, vn, lam_q, lam_k, g_sub)


def _mem_kernel(q_ref, mk_ref, mv_ref, o_ref):
    slabs = [slice(h * LANES, (h + 1) * LANES) for h in range(H_M)]
    nm = mk_ref.shape[1] // H_M
    rows = lambda ref, h: _head_rows(ref, h, nm, H_M, lead=(0,))[...].astype(BF16)
    scores = [_dot_nt(q_ref[0, :, cs], rows(mk_ref, h)) for h, cs in enumerate(slabs)]
    for h, cs in enumerate(slabs):
        mx = jnp.max(scores[h], axis=-1, keepdims=True)
        p = jnp.exp2(scores[h] - mx)
        l = jnp.sum(p, axis=-1, keepdims=True)
        o_ref[0, :, cs] = (_dot(p.astype(BF16), rows(mv_ref, h)) / l).astype(o_ref.dtype)


def _mem_attn(q, mk, mv, tq):
    b, seq, dm = q.shape
    kv_spec = pl.BlockSpec((1,) + mk.shape[1:], lambda i, j: (i, 0, 0))
    q_spec = pl.BlockSpec((1, tq, dm), lambda i, j: (i, j, 0))
    return pl.pallas_call(
        _mem_kernel,
        grid=(b, seq // tq),
        in_specs=[q_spec, kv_spec, kv_spec],
        out_specs=q_spec,
        out_shape=jax.ShapeDtypeStruct((b, seq, dm), BF16),
        compiler_params=_params(("parallel", "parallel")),
        name="mem_attn",
    )(q, mk, mv)


def _merge_kernel(x_ref, oa_ref, ob_ref, om_ref, g_ref, win_ref, bg_ref, wo_ref, wout_ref, o_ref, m_scr,
                  *, tn):
    x = x_ref[...]
    h = _rms(x, g_ref[...]).astype(BF16)
    o_refs = (oa_ref, ob_ref, om_ref)
    d = x.shape[1]
    for c0 in range(0, d, tn):
        cs = slice(c0, c0 + tn)
        m = None
        for br in range(N_BRANCH):
            g0 = br * d + c0
            goff = win_ref.shape[1] - N_BRANCH * d
            gate = jax.nn.sigmoid(_dot(h, win_ref[:, goff + g0:goff + g0 + tn]) + bg_ref[:, g0:g0 + tn])
            term = gate * _dot(o_refs[br][...], wo_ref[br, :, cs])
            m = term if m is None else m + term
        m_scr[:, cs] = m.astype(BF16)
    o_ref[...] = x + _dot(m_scr[...], wout_ref[...])


def _merge(x, oa, ob, om, g, wg, bg, wo, wout, tm, tn=256):
    n, d = x.shape
    kern = functools.partial(_merge_kernel, tn=tn)
    tok = lambda wd: pl.BlockSpec((tm, wd), lambda i: (i, 0))
    return pl.pallas_call(
        kern,
        grid=(n // tm,),
        in_specs=[tok(d), tok(oa.shape[1]), tok(ob.shape[1]), tok(om.shape[1]), _const_spec(g.shape),
                  _const_spec(wg.shape), _const_spec(bg.shape), _const_spec(wo.shape),
                  _const_spec(wout.shape)],
        out_specs=tok(d),
        out_shape=jax.ShapeDtypeStruct((n, d), F32),
        scratch_shapes=[pltpu.VMEM((tm, d), BF16)],
        compiler_params=_params(("parallel",)),
        name="merge",
    )(x, oa, ob, om, g, wg, bg, wo, wout)


def _ffn_kernel(x_ref, g_ref, prev_ref, wup_ref, wconv_ref, bconv_ref, wdown_ref, gfin_ref,
                y_ref, conv_ref, carry, u_scr, *, tm, nch, nseq, final_norm):
    if nseq == 1:
        @pl.when(pl.program_id(1) == 0)
        def _():
            carry[...] = prev_ref[0, 0]

    x = x_ref[0]
    h = _rms(x, g_ref[...]).astype(BF16)
    row = lax.broadcasted_iota(jnp.int32, (tm, FF_CHUNK), 0)
    seq_rows = tm // nseq
    ff = nch * FF_CHUNK
    for ch in range(nch):
        cs = slice(ch * FF_CHUNK, (ch + 1) * FF_CHUNK)
        a = _dot(h, wup_ref[:, cs])
        gate = _dot(h, wup_ref[:, ff + ch * FF_CHUNK:ff + (ch + 1) * FF_CHUNK])
        a1 = pltpu.roll(a, 1, axis=0)
        a2 = pltpu.roll(a, 2, axis=0)
        for k in range(nseq):
            before = carry if nseq == 1 else prev_ref.at[0, k]
            p2 = before[0:1, cs]
            p1 = before[1:2, cs]
            a1 = jnp.where(row == k * seq_rows, p1, a1)
            a2 = jnp.where(row == k * seq_rows, p2, jnp.where(row == k * seq_rows + 1, p1, a2))
        conv = bconv_ref[:, cs] + a2 * wconv_ref[0:1, cs]
        conv = conv + a1 * wconv_ref[1:2, cs]
        conv = conv + a * wconv_ref[2:3, cs]
        if nseq == 1:
            carry[:, cs] = a[tm - (CONV_W - 1):, :]
        else:
            for k in range(nseq):
                conv_ref[0, k, :, cs] = a[(k + 1) * seq_rows - (CONV_W - 1):(k + 1) * seq_rows, :]
        u_scr[:, cs] = (jax.nn.gelu(conv) * gate).astype(BF16)
    y = x + _dot(u_scr[...], wdown_ref[...])
    if final_norm:
        y = _rms(y, gfin_ref[...])
    y_ref[0] = y
    if nseq == 1:
        conv_ref[0, 0] = carry[...]


def _conv_ffn(x, g, prev, wup, wconv, bconv, wdown, gfin, tm, final_norm):
    b, seq, d = x.shape
    ff = wdown.shape[0]
    nch = ff // FF_CHUNK
    nseq = max(tm // seq, 1)
    groups = b // nseq
    assert (seq % tm == 0) if nseq == 1 else (tm % seq == 0 and b % nseq == 0)
    kern = functools.partial(_ffn_kernel, tm=tm, nch=nch, nseq=nseq, final_norm=final_norm)
    tok = pl.BlockSpec((1, tm, d), lambda i, j: (i, j, 0))
    state = pl.BlockSpec((1, nseq, CONV_W - 1, ff), lambda i, j: (i, 0, 0, 0))
    y, conv = pl.pallas_call(
        kern,
        grid=(groups, nseq * seq // tm),
        in_specs=[tok, _const_spec(g.shape), state, _const_spec(wup.shape), _const_spec(wconv.shape),
                  _const_spec(bconv.shape), _const_spec(wdown.shape), _const_spec(gfin.shape)],
        out_specs=[tok, state],
        out_shape=[jax.ShapeDtypeStruct((groups, nseq * seq, d), F32),
                   jax.ShapeDtypeStruct((groups, nseq, CONV_W - 1, ff), F32)],
        scratch_shapes=[pltpu.VMEM((CONV_W - 1, ff), F32), pltpu.VMEM((tm, ff), BF16)],
        compiler_params=_params(("parallel", "arbitrary")),
        name="conv_ffn",
    )(x.reshape(groups, nseq * seq, d), g, prev.reshape(groups, nseq, CONV_W - 1, ff),
      wup, wconv, bconv, wdown, gfin)
    return y.reshape(b, seq, d), conv.reshape(b, CONV_W - 1, ff)


def _token_tile(n, pref):
    return pref if n % pref == 0 else n


def _ffn_tile(b, t):
    if t % 1024 == 0:
        return 1024
    if t % 512 == 0 or (512 % t == 0 and (b * t) % 512 == 0):
        return 512
    return t


def _layer(x, w, *, lam_init, final_norm, conv_prev, mem=None, cache=None):
    b, t, d = x.shape
    n = b * t
    tm = _token_tile(n, 512)
    d_a, d_b, d_m = H_A * DH_A, H_B * 2 * DH_B, H_M * DH_M
    prompt = cache is None
    assert not prompt or (tm == WIN_A and t % tm == 0)
    kv_a = [(BF16, "rows"), (F32, "last_t")] if prompt else [(F32, "rows")]
    kv_b = [(BF16, "rows"), (F32, "by_head")] if prompt else [(F32, "by_head")]
    outs = ([(0, d_a, DH_A ** -0.5 * LOG2E, BF16, "rows")]
            + [(d_a, d_a, 1.0, dt, mode) for dt, mode in kv_a]
            + [(2 * d_a, d_a, 1.0, dt, mode) for dt, mode in kv_a]
            + [(3 * d_a, d_b, DH_B ** -0.5 * LOG2E, BF16, "rows")]
            + [(3 * d_a + d_b, d_b, 1.0, dt, mode) for dt, mode in kv_b]
            + [(3 * d_a + 2 * d_b, d_b, 1.0, dt, mode) for dt, mode in kv_b]
            + [(3 * d_a + 3 * d_b, d_m, DH_M ** -0.5 * LOG2E, BF16, "rows")])
    res = _norm_proj(x.reshape(n, d), w["g_mix"], w["w_in"], outs, tm, tiles_per_seq=max(t // tm, 1))
    res = [a.reshape(b, -1, a.shape[-1]) for a in res]
    if prompt:
        qa, ka, ka_keep, va, va_keep, qb, kb_bf, kb, vb_bf, vb, qm = res
    else:
        qa, ka, va, qb, kb, vb, qm = res
        ka_keep, va_keep = ka, va

    if cache is None:
        tq_a = 128
        oa = _band_attn(qa, ka, va, _band_bias_table(w["rel_bias"], tq_a), tq_a)
        ob = _diff_attn_prompt(qb, kb_bf, vb_bf, w["lam_q"], w["lam_k"], w["g_sub"], 256, lam_init)
        nm = mem.shape[1]
        mk, mv = _norm_proj(mem.reshape(b * nm, d), w["g_mem"], w["w_mem_kv"],
                            [(0, d_m, 1.0, F32, "by_head"), (d_m, d_m, 1.0, F32, "by_head")],
                            _token_tile(b * nm, 512))
        mk = mk.reshape(b, nm * H_M, DH_M)
        mv = mv.reshape(b, nm * H_M, DH_M)
        om = _mem_attn(qm, mk, mv, _token_tile(t, 512))
    else:
        oa = _band_attn(qa, ka, va, _band_bias_table(w["rel_bias"], t), t,
                        hist=(cache["a_k_t"], cache["a_v_t"]))
        ob = _diff_attn_sample(qb, cache["b_k"], cache["b_v"], kb, vb,
                               w["lam_q"], w["lam_k"], w["g_sub"], 2048, 512, lam_init)
        mk, mv = cache["m_k"], cache["m_v"]
        om = _mem_attn(qm, mk, mv, t)

    flat = lambda a: a.reshape(n, a.shape[-1])
    x1 = _merge(x.reshape(n, d), flat(oa), flat(ob), flat(om), w["g_mix"], w["w_in"], w["b_gate"],
                w["w_o"], w["w_out"], tm)
    y, conv = _conv_ffn(x1.reshape(b, t, d), w["g_ffn"], conv_prev, w["w_up"], w["w_conv"], w["b_conv"],
                        w["w_down"], w["g_final"], _ffn_tile(b, t), final_norm)
    return y, dict(ka=ka_keep, va=va_keep, kb=kb, vb=vb, mk=mk, mv=mv, conv=conv)


def kernel(x_prompt, x_sample, mem_prompt, cache_a_k, cache_a_v, cache_b_k, cache_b_v, cache_mem_k, cache_mem_v, state_ffn_conv, g_mix, w_in, b_gate, rel_bias, lam_q, lam_k, g_sub, g_mem, w_mem_kv, w_oa, w_ob, w_om, w_out, g_ffn, w_up, w_conv, b_conv, w_down, g_final):
    depth, d, _ = w_in.shape
    b, s, _ = x_prompt.shape
    bd, t, _ = x_sample.shape
    past = cache_b_k.shape[2]
    a_buf = cache_a_k.shape[2]
    ff = w_down.shape[1]
    d_a, d_b, d_m = H_A * DH_A, H_B * 2 * DH_B, H_M * DH_M
    a_keep = min(WIN_A, s)
    assert a_buf == WIN_A and a_keep == WIN_A and t == CHUNK and past % CHUNK == 0 and ff % FF_CHUNK == 0
    nch = ff // FF_CHUNK

    xp, xs = x_prompt, x_sample
    outs = {k: [] for k in ("pa_k", "pa_v", "pb_k", "pb_v", "pm_k", "pm_v", "pconv",
                            "sa_k", "sa_v", "sb_k", "sb_v", "sconv")}
    for l in range(depth):
        lam_init = 0.8 - 0.6 * math.exp(-0.3 * l)
        w = dict(
            g_mix=g_mix[l][None], g_mem=g_mem[l][None], g_ffn=g_ffn[l][None], g_final=g_final[None],
            g_sub=g_sub[l][None], lam_q=lam_q[l], lam_k=lam_k[l], rel_bias=rel_bias[l],
            w_in=w_in[l].astype(BF16),
            b_gate=b_gate[l][None],
            w_mem_kv=w_mem_kv[l].astype(BF16),
            w_o=jnp.stack([w_oa[l], w_ob[l], w_om[l]]).astype(BF16),
            w_out=w_out[l].astype(BF16),
            w_up=w_up[l].astype(BF16),
            w_conv=w_conv[l], b_conv=b_conv[l][None],
            w_down=w_down[l].astype(BF16),
        )
        last = l == depth - 1
        xp, po = _layer(xp, w, lam_init=lam_init, final_norm=last,
                        conv_prev=jnp.zeros((b, CONV_W - 1, ff), F32), mem=mem_prompt)
        feat_major = lambda a: a.transpose(0, 2, 3, 1).reshape(bd, d_a, a_buf)
        cache = dict(a_k=cache_a_k[l].reshape(bd, a_buf, d_a), a_v=cache_a_v[l].reshape(bd, a_buf, d_a),
                     a_k_t=feat_major(cache_a_k[l]), a_v_t=feat_major(cache_a_v[l]),
                     b_k=cache_b_k[l].reshape(bd, past * H_B, 2 * DH_B),
                     b_v=cache_b_v[l].reshape(bd, past * H_B, 2 * DH_B),
                     m_k=cache_mem_k[l].reshape(bd, -1, DH_M), m_v=cache_mem_v[l].reshape(bd, -1, DH_M))
        xs, so = _layer(xs, w, lam_init=lam_init, final_norm=last, conv_prev=state_ffn_conv[l], cache=cache)

        tok_major = lambda a: a.reshape(b, H_A, DH_A, a_keep).transpose(0, 3, 1, 2)
        outs["pa_k"].append(tok_major(po["ka"]))
        outs["pa_v"].append(tok_major(po["va"]))
        outs["pb_k"].append(po["kb"].reshape(b, s, H_B, 2 * DH_B))
        outs["pb_v"].append(po["vb"].reshape(b, s, H_B, 2 * DH_B))
        outs["pm_k"].append(po["mk"].reshape(b, -1, H_M, DH_M))
        outs["pm_v"].append(po["mv"].reshape(b, -1, H_M, DH_M))
        outs["pconv"].append(po["conv"])
        new_k = jnp.concatenate([cache["a_k"], so["ka"]], axis=1)[:, t:]
        new_v = jnp.concatenate([cache["a_v"], so["va"]], axis=1)[:, t:]
        outs["sa_k"].append(new_k.reshape(bd, a_buf, H_A, DH_A))
        outs["sa_v"].append(new_v.reshape(bd, a_buf, H_A, DH_A))
        outs["sb_k"].append(so["kb"].reshape(bd, t, H_B, 2 * DH_B))
        outs["sb_v"].append(so["vb"].reshape(bd, t, H_B, 2 * DH_B))
        outs["sconv"].append(so["conv"])

    st = {k: jnp.stack(v) for k, v in outs.items()}
    return (xp, xs, st["pa_k"], st["pa_v"], st["pb_k"], st["pb_v"], st["pm_k"], st["pm_v"], st["pconv"],
            st["sa_k"], st["sa_v"], st["sb_k"], st["sb_v"], st["sconv"])
```

```python
import functools
import math

import jax
import jax.numpy as jnp
from jax import lax
from jax.experimental import pallas as pl
from jax.experimental.pallas import tpu as pltpu

CHUNK = 64
BAND_CHUNKS = 8
WIN_A = BAND_CHUNKS * CHUNK
REL_CLIP = 128
H_A = 8
DH_A = 64
H_B = 4
DH_B = 64
H_M = 4
DH_M = 128
N_BRANCH = 3
CONV_W = 3
EPS = 1e-6
NEG_INF = -1e30

LANES = 128
FF_CHUNK = 256
VMEM_LIMIT = 56 * 1024 * 1024

LOG2E = math.log2(math.e)

BF16 = jnp.bfloat16
F32 = jnp.float32


def _rms(x, g):
    y = x * lax.rsqrt(jnp.mean(x * x, axis=-1, keepdims=True) + EPS)
    return y * g


def _dot(a, b):
    return jnp.dot(a, b, preferred_element_type=F32)


def _dot_nt(a, b):
    return lax.dot_general(a, b, (((1,), (1,)), ((), ())), preferred_element_type=F32)


def _dot_tn(a, b):
    return lax.dot_general(a, b, (((0,), (0,)), ((), ())), preferred_element_type=F32)


def _const_spec(shape):
    nd = len(shape)
    return pl.BlockSpec(shape, lambda *_: (0,) * nd, pipeline_mode=pl.Buffered(1))


def _params(sem, flags=None):
    return pltpu.CompilerParams(dimension_semantics=sem, vmem_limit_bytes=VMEM_LIMIT, flags=flags)


def _head_rows(ref, h, n, heads, lead=(), first=0):
    return ref.at[lead + (pl.ds(first * heads + h, n, stride=heads), slice(None))]


def _norm_proj_kernel(x_ref, g_ref, w_ref, *o_refs, outs, tiles_per_seq):
    tm = x_ref.shape[0]
    last_tile = pl.program_id(0) % tiles_per_seq == tiles_per_seq - 1
    h = _rms(x_ref[...], g_ref[...]).astype(BF16)
    z_of = {}
    transposed = []
    for o_ref, (c0, wd, sc, _, mode) in zip(o_refs, outs):
        if (c0, wd) not in z_of:
            z_of[(c0, wd)] = _dot(h, w_ref[:, c0:c0 + wd])
        z = z_of[(c0, wd)]
        if sc != 1.0:
            z = z * sc
        if mode == "by_head":
            heads = wd // LANES
            for hd in range(heads):
                _head_rows(o_ref, hd, tm, heads)[...] = z[:, hd * LANES:(hd + 1) * LANES].astype(o_ref.dtype)
        elif mode == "last_t":
            transposed.append((o_ref, z))
        else:
            o_ref[...] = z.astype(o_ref.dtype)

    if transposed:
        @pl.when(last_tile)
        def _():
            for o_ref, z in transposed:
                o_ref[...] = z.T.astype(o_ref.dtype)


def _norm_proj(x, g, w, outs, tm, tiles_per_seq=1):
    n, d = x.shape
    kern = functools.partial(_norm_proj_kernel, outs=outs, tiles_per_seq=tiles_per_seq)
    specs, shapes = [], []
    for _, wd, _, dt, mode in outs:
        if mode == "by_head":
            blk, rows, imap = ((wd // LANES) * tm, LANES), (wd // LANES) * n, lambda i: (i, 0)
        elif mode == "last_t":
            blk, rows, imap = (wd, tm), (n // tm // tiles_per_seq) * wd, lambda i: (i // tiles_per_seq, 0)
        else:
            blk, rows, imap = (tm, wd), n, lambda i: (i, 0)
        specs.append(pl.BlockSpec(blk, imap))
        shapes.append(jax.ShapeDtypeStruct((rows, blk[1]), dt))
    return pl.pallas_call(
        kern,
        grid=(n // tm,),
        in_specs=[
            pl.BlockSpec((tm, d), lambda i: (i, 0)),
            _const_spec((1, d)),
            _const_spec(w.shape),
        ],
        out_specs=specs,
        out_shape=shapes,
        compiler_params=_params(("arbitrary",)),
        name="norm_proj",
    )(x, g, w)


def _band_kernel(*refs, tq, seq, use_hist):
    if use_hist:
        q_ref, k_ref, v_ref, hk_ref, hv_ref, bias_ref, o_ref, kp, vp, s_scr, qt_scr = refs
    else:
        q_ref, k_ref, v_ref, bias_ref, o_ref, kp, vp, s_scr, qt_scr = refs
    win = WIN_A + tq
    nblk = seq // tq
    npair = H_A // 2

    if use_hist:
        kp[0:WIN_A, :] = hk_ref[0].T.astype(BF16)
        vp[0:WIN_A, :] = hv_ref[0].T.astype(BF16)
    else:
        kp[0:WIN_A, :] = jnp.zeros((WIN_A, kp.shape[1]), BF16)
        vp[0:WIN_A, :] = jnp.zeros((WIN_A, vp.shape[1]), BF16)
    kp[WIN_A:WIN_A + seq, :] = k_ref[0].astype(BF16)
    vp[WIN_A:WIN_A + seq, :] = v_ref[0].astype(BF16)

    feat = lax.broadcasted_iota(jnp.int32, (LANES, tq), 0)
    lo = feat < DH_A
    slabs = [slice(hp * LANES, (hp + 1) * LANES) for hp in range(npair)]

    for blk in range(nblk):
        for hp in range(npair):
            qt_scr[hp, blk] = _split_maps_t(q_ref[0, blk * tq:(blk + 1) * tq, slabs[hp]])

    def stage_a(m, hp, mask_history, slot):
        off = m * tq if isinstance(m, int) else pl.multiple_of(m * tq, tq)
        s = _dot(kp[pl.ds(off, win), slabs[hp]], qt_scr[hp, m]) + bias_ref[hp]
        if mask_history:
            key = lax.broadcasted_iota(jnp.int32, (win, 2 * tq), 0)
            s = jnp.where(key >= (WIN_A - m * tq), s, NEG_INF)
        s_scr[slot, hp] = s
        return jnp.max(s, axis=0, keepdims=True)

    def stage_b(m, hp, mx, slot):
        off = pl.multiple_of(m * tq, tq)
        p = jnp.exp2(s_scr[slot, hp] - mx)
        l = jnp.sum(p, axis=0, keepdims=True)
        o = _dot_tn(vp[pl.ds(off, win), slabs[hp]], p.astype(BF16)) / l
        o_ref[0, pl.ds(off, tq), slabs[hp]] = jnp.where(lo, o[:, :tq], o[:, tq:]).T.astype(o_ref.dtype)

    nmask = 0 if use_hist else min(WIN_A // tq, nblk)
    wide = s_scr.shape[0]
    assert nblk % wide == 0 and nmask % wide == 0

    def group(mask_history, g, mxs):
        blocks = [g * wide + i for i in range(wide)]
        for hp in range(1, npair):
            nxt = []
            for i in range(wide):
                nxt.append(stage_a(blocks[i], hp, mask_history, i))
                stage_b(blocks[i], hp - 1, mxs[i], i)
            mxs = nxt
        nxt = []
        for i in range(wide):
            nxt.append(stage_a(jnp.minimum(blocks[i] + wide, nblk - wide + i), 0, mask_history, i))
            stage_b(blocks[i], npair - 1, mxs[i], i)
        return tuple(nxt)

    mxs = tuple(stage_a(i, 0, nmask > 0, i) for i in range(wide))
    unroll = lambda trips: 2 if trips % 2 == 0 else 1
    mxs = lax.fori_loop(0, nmask // wide, functools.partial(group, True), mxs, unroll=unroll(nmask // wide))
    lax.fori_loop(nmask // wide, nblk // wide, functools.partial(group, False), mxs,
                  unroll=unroll((nblk - nmask) // wide))


def _band_blocks_in_flight(nblk):
    return 2 if nblk % 2 == 0 else 1


def _band_attn(q, k, v, bias, tq, hist=None):
    b, seq, da = q.shape
    win = WIN_A + tq
    use_hist = hist is not None
    kern = functools.partial(_band_kernel, tq=tq, seq=seq, use_hist=use_hist)
    full = pl.BlockSpec((1, seq, da), lambda i: (i, 0, 0))
    in_specs = [full, full, full]
    args = [q, k, v]
    if use_hist:
        hspec = pl.BlockSpec((1, da, WIN_A), lambda i: (i, 0, 0))
        in_specs += [hspec, hspec]
        args += list(hist)
    in_specs.append(_const_spec(bias.shape))
    args.append(bias)
    return pl.pallas_call(
        kern,
        grid=(b,),
        in_specs=in_specs,
        out_specs=full,
        out_shape=jax.ShapeDtypeStruct((b, seq, da), BF16),
        scratch_shapes=[pltpu.VMEM((WIN_A + seq, da), BF16), pltpu.VMEM((WIN_A + seq, da), BF16),
                        pltpu.VMEM((_band_blocks_in_flight(seq // tq), H_A // 2, win, 2 * tq), F32),
                        pltpu.VMEM((H_A // 2, seq // tq, LANES, 2 * tq), BF16)],
        compiler_params=_params(("parallel",)),
        name="band_attn",
    )(*args)


BIAS_LANE0 = WIN_A + REL_CLIP
BIAS_WIDTH = 768


def _band_bias_kernel(rb_ref, o_ref, *, tq, win):
    z = lax.broadcasted_iota(jnp.int32, (1, BIAS_WIDTH), 1)
    j = lax.broadcasted_iota(jnp.int32, (win, tq), 0)
    r = lax.broadcasted_iota(jnp.int32, (win, tq), 1)
    kc = j // CHUNK
    qc = r // CHUNK
    band = (kc >= qc) & (kc <= qc + BAND_CHUNKS)
    for h in range(H_A):
        row = jnp.where(z >= 2 * REL_CLIP, rb_ref[h:h + 1, 2 * REL_CLIP:2 * REL_CLIP + 1], rb_ref[h:h + 1, :])
        skew = pltpu.roll(jnp.broadcast_to(row, (win, BIAS_WIDTH)), 0, axis=1, stride=1, stride_axis=0)
        tbl = jnp.where(band, skew[:, BIAS_LANE0:BIAS_LANE0 + tq] * LOG2E, NEG_INF)
        o_ref[h // 2, :, (h % 2) * tq:(h % 2 + 1) * tq] = tbl


def _band_bias_table(rel_bias, tq):
    win = WIN_A + tq
    assert BIAS_LANE0 + tq <= BIAS_WIDTH and win <= BIAS_WIDTH
    rb = jnp.pad(rel_bias.astype(F32), ((0, 0), (0, BIAS_WIDTH - rel_bias.shape[1])))
    return pl.pallas_call(
        functools.partial(_band_bias_kernel, tq=tq, win=win),
        out_shape=jax.ShapeDtypeStruct((H_A // 2, win, 2 * tq), F32),
        compiler_params=pltpu.CompilerParams(vmem_limit_bytes=VMEM_LIMIT),
        name="band_bias",
    )(rb)


def _split_maps_t(q):
    qt = q.astype(F32).T
    feat = lax.broadcasted_iota(jnp.int32, qt.shape, 0)
    lo = feat < DH_B
    return jnp.concatenate([jnp.where(lo, qt, 0.0), jnp.where(lo, 0.0, qt)], axis=1).astype(BF16)


def _key_query_diff(tk, tq):
    c = lax.broadcasted_iota(jnp.int32, (tk, 2 * tq), 0)
    r = lax.broadcasted_iota(jnp.int32, (tk, 2 * tq), 1)
    rr = jnp.where(r >= tq, r - tq, r)
    return rr, c


def _alibi_slope(h):
    return 2.0 ** (-8.0 * (h + 1) / H_B) * LOG2E


def _online_update(s, vb, shift, carry):
    m, l, acc = carry
    m_new = jnp.maximum(m, jnp.max(s, axis=0, keepdims=True) + shift)
    alpha = jnp.exp2(m - m_new)
    p = jnp.exp2(s - (m_new - shift))
    l = alpha * l + jnp.sum(p, axis=0, keepdims=True)
    acc = alpha * acc + _dot_tn(vb, p.astype(BF16))
    return m_new, l, acc


def _diff_finish_t(carry, tq, lam, gsub, lam_init):
    _, l, acc = carry
    o = acc / l
    od = (o[:, :tq] - lam * o[:, tq:]).T
    return _rms(od, gsub) * (1.0 - lam_init)


def _diff_lambda(lq_ref, lk_ref, lam_init):
    e = jnp.exp(jnp.sum(lq_ref[...] * lk_ref[...], axis=-1, keepdims=True))
    return e[0:1] - e[1:2] + lam_init


def _diff_init_t(tq):
    return (jnp.full((1, 2 * tq), NEG_INF, F32), jnp.zeros((1, 2 * tq), F32),
            jnp.zeros((LANES, 2 * tq), F32))


def _diff_prompt_kernel(q_ref, k_ref, v_ref, lq_ref, lk_ref, gsub_ref, o_ref,
                        bias_scr, qq_scr, s0_scr, s1_scr, acc_scr, *, tq, lam_init):
    qi = pl.program_id(1)
    slabs = [slice(h * LANES, (h + 1) * LANES) for h in range(H_B)]
    chains = [(bb, h) for bb in range(q_ref.shape[0]) for h in range(H_B)]

    @pl.when(qi == 0)
    def _():
        rr, c = _key_query_diff(tq, tq)
        d = (rr - c).astype(F32)
        allowed = (c // CHUNK) <= (rr // CHUNK)
        for h in range(H_B):
            bias_scr[0, h] = -_alibi_slope(h) * d
            bias_scr[1, h] = jnp.where(allowed, -_alibi_slope(h) * jnp.abs(d), NEG_INF)

    for c, (bb, h) in enumerate(chains):
        qq_scr[c] = _split_maps_t(q_ref[bb, :, slabs[h]])
    acc_scr[...] = jnp.zeros(acc_scr.shape, F32)

    def chain_a(j, s_ref, c, m):
        bb, h = chains[c]
        off = j * tq if isinstance(j, int) else pl.multiple_of(j * tq, tq)
        diag = (j == qi).astype(jnp.int32)
        s = _dot(k_ref[bb, pl.ds(off, tq), slabs[h]], qq_scr[c]) + bias_scr[diag, h]
        s_ref[c] = s
        shift = -_alibi_slope(h) * ((qi - j) * tq).astype(F32)
        m_new = jnp.maximum(m, jnp.max(s, axis=0, keepdims=True) + shift)
        return m_new, jnp.exp2(m - m_new), m_new - shift

    def chain_b(j, s_ref, c, l, alpha, sub):
        bb, h = chains[c]
        off = j * tq if isinstance(j, int) else pl.multiple_of(j * tq, tq)
        p = jnp.exp2(s_ref[c] - sub)
        acc_scr[c] = alpha * acc_scr[c] + _dot_tn(v_ref[bb, pl.ds(off, tq), slabs[h]], p.astype(BF16))
        return alpha * l + jnp.sum(p, axis=0, keepdims=True)

    def stage_a(j, s_ref, ms):
        out = [chain_a(j, s_ref, c, ms[c]) for c in range(len(chains))]
        return tuple(o[0] for o in out), tuple(o[1] for o in out), tuple(o[2] for o in out)

    def stage_b(j, s_ref, ls, alphas, subs):
        return tuple(chain_b(j, s_ref, c, ls[c], alphas[c], subs[c]) for c in range(len(chains)))

    def stage_ab(ja, sa_ref, ms, jb, sb_ref, ls, alphas, subs):
        out, new_l = [], []
        for c in range(len(chains)):
            out.append(chain_a(ja, sa_ref, c, ms[c]))
            new_l.append(chain_b(jb, sb_ref, c, ls[c], alphas[c], subs[c]))
        return (tuple(o[0] for o in out), tuple(o[1] for o in out), tuple(o[2] for o in out)), tuple(new_l)

    def finish(ls):
        lam = _diff_lambda(lq_ref, lk_ref, lam_init)
        for c, (bb, h) in enumerate(chains):
            o_ref[bb, :, slabs[h]] = _diff_finish_t((None, ls[c], acc_scr[c]), tq, lam, gsub_ref[...],
                                                    lam_init).astype(o_ref.dtype)

    row = lambda val: tuple(jnp.full((1, 2 * tq), val, F32) for _ in chains)
    ms, alphas, subs = stage_a(0, s0_scr, row(NEG_INF))

    def two_blocks(jj, carry):
        ms, ls, alphas, subs = carry
        j = 2 * jj
        (ms, alphas1, subs1), ls = stage_ab(j + 1, s1_scr, ms, j, s0_scr, ls, alphas, subs)
        (ms, alphas, subs), ls = stage_ab(j + 2, s0_scr, ms, j + 1, s1_scr, ls, alphas1, subs1)
        return ms, ls, alphas, subs

    ms, ls, alphas, subs = lax.fori_loop(0, qi // 2, two_blocks, (ms, row(0.0), alphas, subs))

    @pl.when(qi % 2 == 1)
    def _():
        (_, alphas1, subs1), ls1 = stage_ab(qi, s1_scr, ms, qi - 1, s0_scr, ls, alphas, subs)
        finish(stage_b(qi, s1_scr, ls1, alphas1, subs1))

    @pl.when(qi % 2 == 0)
    def _():
        finish(stage_b(qi, s0_scr, ls, alphas, subs))


def _diff_attn_prompt(q, k, v, lam_q, lam_k, g_sub, tq, lam_init):
    b, seq, db = q.shape
    nbat = 2 if b % 2 == 0 else 1
    kern = functools.partial(_diff_prompt_kernel, tq=tq, lam_init=lam_init)
    kv_spec = pl.BlockSpec((nbat, seq, db), lambda i, j: (i, 0, 0))
    q_spec = pl.BlockSpec((nbat, tq, db), lambda i, j: (i, j, 0))
    return pl.pallas_call(
        kern,
        grid=(b // nbat, seq // tq),
        in_specs=[q_spec, kv_spec, kv_spec,
                  _const_spec(lam_q.shape), _const_spec(lam_k.shape), _const_spec(g_sub.shape)],
        out_specs=q_spec,
        out_shape=jax.ShapeDtypeStruct((b, seq, db), BF16),
        scratch_shapes=[pltpu.VMEM((2, H_B, tq, 2 * tq), F32), pltpu.VMEM((nbat * H_B, LANES, 2 * tq), BF16),
                        pltpu.VMEM((nbat * H_B, tq, 2 * tq), F32), pltpu.VMEM((nbat * H_B, tq, 2 * tq), F32),
                        pltpu.VMEM((nbat * H_B, LANES, 2 * tq), F32)],
        compiler_params=_params(("parallel", "arbitrary")),
        name="diff_attn_prompt",
    )(q, k, v, lam_q, lam_k, g_sub)


def _diff_sample_kernel(q_ref, kc_ref, vc_ref, kn_ref, vn_ref, lq_ref, lk_ref, gsub_ref, o_ref,
                        m_s, l_s, acc_s, s_scr, *, t, past, tk, ts, lam_init):
    ci = pl.program_id(1)

    @pl.when(ci == 0)
    def _():
        for h in range(H_B):
            m_s[h], l_s[h], acc_s[h] = _diff_init_t(t)

    slabs = [slice(h * LANES, (h + 1) * LANES) for h in range(H_B)]
    qqts = [_split_maps_t(q_ref[0, :, cs]) for cs in slabs]
    rr, c = _key_query_diff(ts, t)
    d = (rr - c).astype(F32)
    rows = lambda ref, h, n, first=0: _head_rows(ref, h, n, H_B, lead=(0,), first=first)[...].astype(BF16)
    nstream = s_scr.shape[0]
    per = H_B // nstream
    items = [[(sub, h) for sub in range(tk // ts) for h in range(st * per, (st + 1) * per)]
             for st in range(nstream)]
    nitem = len(items[0])

    def stage_a(st, i):
        sub, h = items[st][i]
        s = _dot(rows(kc_ref, h, ts, sub * ts), qqts[h]) - _alibi_slope(h) * d
        s_scr[st, i % 2] = s
        shift = -_alibi_slope(h) * (past - ci * tk - sub * ts).astype(F32)
        m_new = jnp.maximum(m_s[h], jnp.max(s, axis=0, keepdims=True) + shift)
        alpha = jnp.exp2(m_s[h] - m_new)
        m_s[h] = m_new
        return alpha, m_new - shift

    def stage_b(st, i, alpha, ref_max):
        sub, h = items[st][i]
        p = jnp.exp2(s_scr[st, i % 2] - ref_max)
        l_s[h] = alpha * l_s[h] + jnp.sum(p, axis=0, keepdims=True)
        acc_s[h] = alpha * acc_s[h] + _dot_tn(rows(vc_ref, h, ts, sub * ts), p.astype(BF16))

    pend = [stage_a(st, 0) for st in range(nstream)]
    for i in range(1, nitem):
        nxt = [stage_a(st, i) for st in range(nstream)]
        for st in range(nstream):
            stage_b(st, i - 1, *pend[st])
        pend = nxt
    for st in range(nstream):
        stage_b(st, nitem - 1, *pend[st])

    @pl.when(ci == past // tk - 1)
    def _():
        rn, cn = _key_query_diff(t, t)
        dn = jnp.abs((rn - cn).astype(F32))
        lam = _diff_lambda(lq_ref, lk_ref, lam_init)
        for h, cs in enumerate(slabs):
            s = _dot(rows(kn_ref, h, t), qqts[h]) - _alibi_slope(h) * dn
            carry = _online_update(s, rows(vn_ref, h, t), 0.0, (m_s[h], l_s[h], acc_s[h]))
            o_ref[0, :, cs] = _diff_finish_t(carry, t, lam, gsub_ref[...], lam_init).astype(o_ref.dtype)


def _diff_attn_sample(q, kc, vc, kn, vn, lam_q, lam_k, g_sub, tk, ts, lam_init):
    b, t, db = q.shape
    past = kc.shape[1] // H_B
    kern = functools.partial(_diff_sample_kernel, t=t, past=past, tk=tk, ts=ts, lam_init=lam_init)
    new_spec = pl.BlockSpec((1, t, db), lambda i, j: (i, 0, 0))
    new_kv_spec = pl.BlockSpec((1, t * H_B, LANES), lambda i, j: (i, 0, 0))
    cache_spec = pl.BlockSpec((1, tk * H_B, LANES), lambda i, j: (i, j, 0))
    return pl.pallas_call(
        kern,
        grid=(b, past // tk),
        in_specs=[new_spec, cache_spec, cache_spec, new_kv_spec, new_kv_spec,
                  _const_spec(lam_q.shape), _const_spec(lam_k.shape), _const_spec(g_sub.shape)],
        out_specs=new_spec,
        out_shape=jax.ShapeDtypeStruct((b, t, db), BF16),
        scratch_shapes=[pltpu.VMEM((H_B, 1, 2 * t), F32), pltpu.VMEM((H_B, 1, 2 * t), F32),
                        pltpu.VMEM((H_B, LANES, 2 * t), F32), pltpu.VMEM((2, 2, ts, 2 * t), F32)],
        compiler_params=_params(("parallel", "arbitrary")),
        name="diff_attn_sample",
    )(q, kc, vc, kn, vn, lam_q, lam_k, g_sub)


def _mem_kernel(q_ref, mk_ref, mv_ref, o_ref):
    slabs = [slice(h * LANES, (h + 1) * LANES) for h in range(H_M)]
    nm = mk_ref.shape[1] // H_M
    rows = lambda ref, h: _head_rows(ref, h, nm, H_M, lead=(0,))[...].astype(BF16)
    scores = [_dot_nt(q_ref[0, :, cs], rows(mk_ref, h)) for h, cs in enumerate(slabs)]
    for h, cs in enumerate(slabs):
        mx = jnp.max(scores[h], axis=-1, keepdims=True)
        p = jnp.exp2(scores[h] - mx)
        l = jnp.sum(p, axis=-1, keepdims=True)
        o_ref[0, :, cs] = (_dot(p.astype(BF16), rows(mv_ref, h)) / l).astype(o_ref.dtype)


def _mem_attn(q, mk, mv, tq):
    b, seq, dm = q.shape
    kv_spec = pl.BlockSpec((1,) + mk.shape[1:], lambda i, j: (i, 0, 0))
    q_spec = pl.BlockSpec((1, tq, dm), lambda i, j: (i, j, 0))
    return pl.pallas_call(
        _mem_kernel,
        grid=(b, seq // tq),
        in_specs=[q_spec, kv_spec, kv_spec],
        out_specs=q_spec,
        out_shape=jax.ShapeDtypeStruct((b, seq, dm), BF16),
        compiler_params=_params(("parallel", "parallel")),
        name="mem_attn",
    )(q, mk, mv)


def _merge_kernel(x_ref, oa_ref, ob_ref, om_ref, g_ref, win_ref, bg_ref, wo_ref, wout_ref, o_ref, m_scr,
                  *, tn):
    x = x_ref[...]
    h = _rms(x, g_ref[...]).astype(BF16)
    o_refs = (oa_ref, ob_ref, om_ref)
    d = x.shape[1]
    for c0 in range(0, d, tn):
        cs = slice(c0, c0 + tn)
        m = None
        for br in range(N_BRANCH):
            g0 = br * d + c0
            goff = win_ref.shape[1] - N_BRANCH * d
            gate = jax.nn.sigmoid(_dot(h, win_ref[:, goff + g0:goff + g0 + tn]) + bg_ref[:, g0:g0 + tn])
            term = gate * _dot(o_refs[br][...], wo_ref[br, :, cs])
            m = term if m is None else m + term
        m_scr[:, cs] = m.astype(BF16)
    o_ref[...] = x + _dot(m_scr[...], wout_ref[...])


def _merge(x, oa, ob, om, g, wg, bg, wo, wout, tm, tn=256):
    n, d = x.shape
    kern = functools.partial(_merge_kernel, tn=tn)
    tok = lambda wd: pl.BlockSpec((tm, wd), lambda i: (i, 0))
    return pl.pallas_call(
        kern,
        grid=(n // tm,),
        in_specs=[tok(d), tok(oa.shape[1]), tok(ob.shape[1]), tok(om.shape[1]), _const_spec(g.shape),
                  _const_spec(wg.shape), _const_spec(bg.shape), _const_spec(wo.shape),
                  _const_spec(wout.shape)],
        out_specs=tok(d),
        out_shape=jax.ShapeDtypeStruct((n, d), F32),
        scratch_shapes=[pltpu.VMEM((tm, d), BF16)],
        compiler_params=_params(("parallel",)),
        name="merge",
    )(x, oa, ob, om, g, wg, bg, wo, wout)


def _ffn_kernel(x_ref, g_ref, prev_ref, wup_ref, wconv_ref, bconv_ref, wdown_ref, gfin_ref,
                y_ref, conv_ref, carry, u_scr, *, tm, nch, nseq, final_norm):
    if nseq == 1:
        @pl.when(pl.program_id(1) == 0)
        def _():
            carry[...] = prev_ref[0, 0]

    x = x_ref[0]
    h = _rms(x, g_ref[...]).astype(BF16)
    row = lax.broadcasted_iota(jnp.int32, (tm, FF_CHUNK), 0)
    seq_rows = tm // nseq
    ff = nch * FF_CHUNK
    for ch in range(nch):
        cs = slice(ch * FF_CHUNK, (ch + 1) * FF_CHUNK)
        a = _dot(h, wup_ref[:, cs])
        gate = _dot(h, wup_ref[:, ff + ch * FF_CHUNK:ff + (ch + 1) * FF_CHUNK])
        a1 = pltpu.roll(a, 1, axis=0)
        a2 = pltpu.roll(a, 2, axis=0)
        for k in range(nseq):
            before = carry if nseq == 1 else prev_ref.at[0, k]
            p2 = before[0:1, cs]
            p1 = before[1:2, cs]
            a1 = jnp.where(row == k * seq_rows, p1, a1)
            a2 = jnp.where(row == k * seq_rows, p2, jnp.where(row == k * seq_rows + 1, p1, a2))
        conv = bconv_ref[:, cs] + a2 * wconv_ref[0:1, cs]
        conv = conv + a1 * wconv_ref[1:2, cs]
        conv = conv + a * wconv_ref[2:3, cs]
        if nseq == 1:
            carry[:, cs] = a[tm - (CONV_W - 1):, :]
        else:
            for k in range(nseq):
                conv_ref[0, k, :, cs] = a[(k + 1) * seq_rows - (CONV_W - 1):(k + 1) * seq_rows, :]
        u_scr[:, cs] = (jax.nn.gelu(conv) * gate).astype(BF16)
    y = x + _dot(u_scr[...], wdown_ref[...])
    if final_norm:
        y = _rms(y, gfin_ref[...])
    y_ref[0] = y
    if nseq == 1:
        conv_ref[0, 0] = carry[...]


def _conv_ffn(x, g, prev, wup, wconv, bconv, wdown, gfin, tm, final_norm):
    b, seq, d = x.shape
    ff = wdown.shape[0]
    nch = ff // FF_CHUNK
    nseq = max(tm // seq, 1)
    groups = b // nseq
    assert (seq % tm == 0) if nseq == 1 else (tm % seq == 0 and b % nseq == 0)
    kern = functools.partial(_ffn_kernel, tm=tm, nch=nch, nseq=nseq, final_norm=final_norm)
    tok = pl.BlockSpec((1, tm, d), lambda i, j: (i, j, 0))
    state = pl.BlockSpec((1, nseq, CONV_W - 1, ff), lambda i, j: (i, 0, 0, 0))
    y, conv = pl.pallas_call(
        kern,
        grid=(groups, nseq * seq // tm),
        in_specs=[tok, _const_spec(g.shape), state, _const_spec(wup.shape), _const_spec(wconv.shape),
                  _const_spec(bconv.shape), _const_spec(wdown.shape), _const_spec(gfin.shape)],
        out_specs=[tok, state],
        out_shape=[jax.ShapeDtypeStruct((groups, nseq * seq, d), F32),
                   jax.ShapeDtypeStruct((groups, nseq, CONV_W - 1, ff), F32)],
        scratch_shapes=[pltpu.VMEM((CONV_W - 1, ff), F32), pltpu.VMEM((tm, ff), BF16)],
        compiler_params=_params(("parallel", "arbitrary")),
        name="conv_ffn",
    )(x.reshape(groups, nseq * seq, d), g, prev.reshape(groups, nseq, CONV_W - 1, ff),
      wup, wconv, bconv, wdown, gfin)
    return y.reshape(b, seq, d), conv.reshape(b, CONV_W - 1, ff)


def _token_tile(n, pref):
    return pref if n % pref == 0 else n


def _tiles(b, t, prompt):
    n = b * t
    if t % 1024 == 0:
        ffn = 1024
    elif t % 512 == 0 or (512 % t == 0 and n % 512 == 0):
        ffn = 512
    else:
        ffn = t
    return dict(
        proj=_token_tile(n, WIN_A),
        merge=_token_tile(n, 1024),
        ffn=ffn,
        band_q=128 if prompt else t,
        diff_q=256,
        mem_q=_token_tile(t, 1024),
        cache_keys=2048,
        cache_item=512,
    )


def _layer(x, w, *, lam_init, final_norm, conv_prev, mem=None, cache=None):
    b, t, d = x.shape
    n = b * t
    tile = _tiles(b, t, cache is None)
    tm = tile["proj"]
    d_a, d_b, d_m = H_A * DH_A, H_B * 2 * DH_B, H_M * DH_M
    prompt = cache is None
    assert not prompt or (tm == WIN_A and t % tm == 0)
    kv_a = [(BF16, "rows"), (F32, "last_t")] if prompt else [(F32, "rows")]
    kv_b = [(BF16, "rows"), (F32, "by_head")] if prompt else [(F32, "by_head")]
    outs = ([(0, d_a, DH_A ** -0.5 * LOG2E, BF16, "rows")]
            + [(d_a, d_a, 1.0, dt, mode) for dt, mode in kv_a]
            + [(2 * d_a, d_a, 1.0, dt, mode) for dt, mode in kv_a]
            + [(3 * d_a, d_b, DH_B ** -0.5 * LOG2E, BF16, "rows")]
            + [(3 * d_a + d_b, d_b, 1.0, dt, mode) for dt, mode in kv_b]
            + [(3 * d_a + 2 * d_b, d_b, 1.0, dt, mode) for dt, mode in kv_b]
            + [(3 * d_a + 3 * d_b, d_m, DH_M ** -0.5 * LOG2E, BF16, "rows")])
    res = _norm_proj(x.reshape(n, d), w["g_mix"], w["w_in"], outs, tm, tiles_per_seq=max(t // tm, 1))
    res = [a.reshape(b, -1, a.shape[-1]) for a in res]
    if prompt:
        qa, ka, ka_keep, va, va_keep, qb, kb_bf, kb, vb_bf, vb, qm = res
    else:
        qa, ka, va, qb, kb, vb, qm = res
        ka_keep, va_keep = ka, va

    if cache is None:
        oa = _band_attn(qa, ka, va, _band_bias_table(w["rel_bias"], tile["band_q"]), tile["band_q"])
        ob = _diff_attn_prompt(qb, kb_bf, vb_bf, w["lam_q"], w["lam_k"], w["g_sub"], tile["diff_q"], lam_init)
        nm = mem.shape[1]
        mk, mv = _norm_proj(mem.reshape(b * nm, d), w["g_mem"], w["w_mem_kv"],
                            [(0, d_m, 1.0, F32, "by_head"), (d_m, d_m, 1.0, F32, "by_head")],
                            _token_tile(b * nm, WIN_A))
        mk = mk.reshape(b, nm * H_M, DH_M)
        mv = mv.reshape(b, nm * H_M, DH_M)
        om = _mem_attn(qm, mk, mv, tile["mem_q"])
    else:
        oa = _band_attn(qa, ka, va, _band_bias_table(w["rel_bias"], tile["band_q"]), tile["band_q"],
                        hist=(cache["a_k_t"], cache["a_v_t"]))
        ob = _diff_attn_sample(qb, cache["b_k"], cache["b_v"], kb, vb,
                               w["lam_q"], w["lam_k"], w["g_sub"], tile["cache_keys"], tile["cache_item"], lam_init)
        mk, mv = cache["m_k"], cache["m_v"]
        om = _mem_attn(qm, mk, mv, tile["mem_q"])

    flat = lambda a: a.reshape(n, a.shape[-1])
    x1 = _merge(x.reshape(n, d), flat(oa), flat(ob), flat(om), w["g_mix"], w["w_in"], w["b_gate"],
                w["w_o"], w["w_out"], tile["merge"])
    y, conv = _conv_ffn(x1.reshape(b, t, d), w["g_ffn"], conv_prev, w["w_up"], w["w_conv"], w["b_conv"],
                        w["w_down"], w["g_final"], tile["ffn"], final_norm)
    return y, dict(ka=ka_keep, va=va_keep, kb=kb, vb=vb, mk=mk, mv=mv, conv=conv)


def kernel(x_prompt, x_sample, mem_prompt, cache_a_k, cache_a_v, cache_b_k, cache_b_v, cache_mem_k, cache_mem_v, state_ffn_conv, g_mix, w_in, b_gate, rel_bias, lam_q, lam_k, g_sub, g_mem, w_mem_kv, w_oa, w_ob, w_om, w_out, g_ffn, w_up, w_conv, b_conv, w_down, g_final):
    depth, d, _ = w_in.shape
    b, s, _ = x_prompt.shape
    bd, t, _ = x_sample.shape
    past = cache_b_k.shape[2]
    a_buf = cache_a_k.shape[2]
    ff = w_down.shape[1]
    d_a, d_b, d_m = H_A * DH_A, H_B * 2 * DH_B, H_M * DH_M
    a_keep = min(WIN_A, s)
    assert a_buf == WIN_A and a_keep == WIN_A and t == CHUNK and past % CHUNK == 0 and ff % FF_CHUNK == 0
    nch = ff // FF_CHUNK

    xp, xs = x_prompt, x_sample
    outs = {k: [] for k in ("pa_k", "pa_v", "pb_k", "pb_v", "pm_k", "pm_v", "pconv",
                            "sa_k", "sa_v", "sb_k", "sb_v", "sconv")}
    for l in range(depth):
        lam_init = 0.8 - 0.6 * math.exp(-0.3 * l)
        w = dict(
            g_mix=g_mix[l][None], g_mem=g_mem[l][None], g_ffn=g_ffn[l][None], g_final=g_final[None],
            g_sub=g_sub[l][None], lam_q=lam_q[l], lam_k=lam_k[l], rel_bias=rel_bias[l],
            w_in=w_in[l].astype(BF16),
            b_gate=b_gate[l][None],
            w_mem_kv=w_mem_kv[l].astype(BF16),
            w_o=jnp.stack([w_oa[l], w_ob[l], w_om[l]]).astype(BF16),
            w_out=w_out[l].astype(BF16),
            w_up=w_up[l].astype(BF16),
            w_conv=w_conv[l], b_conv=b_conv[l][None],
            w_down=w_down[l].astype(BF16),
        )
        last = l == depth - 1
        xp, po = _layer(xp, w, lam_init=lam_init, final_norm=last,
                        conv_prev=jnp.zeros((b, CONV_W - 1, ff), F32), mem=mem_prompt)
        feat_major = lambda a: a.transpose(0, 2, 3, 1).reshape(bd, d_a, a_buf)
        cache = dict(a_k=cache_a_k[l].reshape(bd, a_buf, d_a), a_v=cache_a_v[l].reshape(bd, a_buf, d_a),
                     a_k_t=feat_major(cache_a_k[l]), a_v_t=feat_major(cache_a_v[l]),
                     b_k=cache_b_k[l].reshape(bd, past * H_B, 2 * DH_B),
                     b_v=cache_b_v[l].reshape(bd, past * H_B, 2 * DH_B),
                     m_k=cache_mem_k[l].reshape(bd, -1, DH_M), m_v=cache_mem_v[l].reshape(bd, -1, DH_M))
        xs, so = _layer(xs, w, lam_init=lam_init, final_norm=last, conv_prev=state_ffn_conv[l], cache=cache)

        tok_major = lambda a: a.reshape(b, H_A, DH_A, a_keep).transpose(0, 3, 1, 2)
        outs["pa_k"].append(tok_major(po["ka"]))
        outs["pa_v"].append(tok_major(po["va"]))
        outs["pb_k"].append(po["kb"].reshape(b, s, H_B, 2 * DH_B))
        outs["pb_v"].append(po["vb"].reshape(b, s, H_B, 2 * DH_B))
        outs["pm_k"].append(po["mk"].reshape(b, -1, H_M, DH_M))
        outs["pm_v"].append(po["mv"].reshape(b, -1, H_M, DH_M))
        outs["pconv"].append(po["conv"])
        new_k = jnp.concatenate([cache["a_k"], so["ka"]], axis=1)[:, t:]
        new_v = jnp.concatenate([cache["a_v"], so["va"]], axis=1)[:, t:]
        outs["sa_k"].append(new_k.reshape(bd, a_buf, H_A, DH_A))
        outs["sa_v"].append(new_v.reshape(bd, a_buf, H_A, DH_A))
        outs["sb_k"].append(so["kb"].reshape(bd, t, H_B, 2 * DH_B))
        outs["sb_v"].append(so["vb"].reshape(bd, t, H_B, 2 * DH_B))
        outs["sconv"].append(so["conv"])

    st = {k: jnp.stack(v) for k, v in outs.items()}
    return (xp, xs, st["pa_k"], st["pa_v"], st["pb_k"], st["pb_v"], st["pm_k"], st["pm_v"], st["pconv"],
            st["sa_k"], st["sa_v"], st["sb_k"], st["sb_v"], st["sconv"])
```

```python
import functools
import math

import jax
import jax.numpy as jnp
from jax import lax
from jax.experimental import pallas as pl
from jax.experimental.pallas import tpu as pltpu

CHUNK = 64
BAND_CHUNKS = 8
WIN_A = BAND_CHUNKS * CHUNK
REL_CLIP = 128
H_A = 8
DH_A = 64
H_B = 4
DH_B = 64
H_M = 4
DH_M = 128
N_BRANCH = 3
CONV_W = 3
EPS = 1e-6
NEG_INF = -1e30

LANES = 128
FF_CHUNK = 256
VMEM_LIMIT = 56 * 1024 * 1024

LOG2E = math.log2(math.e)

BF16 = jnp.bfloat16
F32 = jnp.float32


def _rms(x, g):
    y = x * lax.rsqrt(jnp.mean(x * x, axis=-1, keepdims=True) + EPS)
    return y * g


def _dot(a, b):
    return jnp.dot(a, b, preferred_element_type=F32)


def _dot_nt(a, b):
    return lax.dot_general(a, b, (((1,), (1,)), ((), ())), preferred_element_type=F32)


def _dot_tn(a, b):
    return lax.dot_general(a, b, (((0,), (0,)), ((), ())), preferred_element_type=F32)


def _const_spec(shape):
    nd = len(shape)
    return pl.BlockSpec(shape, lambda *_: (0,) * nd, pipeline_mode=pl.Buffered(1))


def _params(sem, flags=None):
    return pltpu.CompilerParams(dimension_semantics=sem, vmem_limit_bytes=VMEM_LIMIT, flags=flags)


def _head_rows(ref, h, n, heads, lead=(), first=0):
    return ref.at[lead + (pl.ds(first * heads + h, n, stride=heads), slice(None))]


def _norm_proj_kernel(x_ref, g_ref, w_ref, *o_refs, outs, tiles_per_seq):
    tm = x_ref.shape[0]
    last_tile = pl.program_id(0) % tiles_per_seq == tiles_per_seq - 1
    h = _rms(x_ref[...], g_ref[...]).astype(BF16)
    z_of = {}
    transposed = []
    for o_ref, (c0, wd, sc, _, mode) in zip(o_refs, outs):
        if (c0, wd) not in z_of:
            z_of[(c0, wd)] = _dot(h, w_ref[:, c0:c0 + wd])
        z = z_of[(c0, wd)]
        if sc != 1.0:
            z = z * sc
        if mode == "by_head":
            heads = wd // LANES
            for hd in range(heads):
                _head_rows(o_ref, hd, tm, heads)[...] = z[:, hd * LANES:(hd + 1) * LANES].astype(o_ref.dtype)
        elif mode == "last_t":
            transposed.append((o_ref, z))
        else:
            o_ref[...] = z.astype(o_ref.dtype)

    if transposed:
        @pl.when(last_tile)
        def _():
            for o_ref, z in transposed:
                o_ref[...] = z.T.astype(o_ref.dtype)


def _norm_proj(x, g, w, outs, tm, tiles_per_seq=1):
    n, d = x.shape
    kern = functools.partial(_norm_proj_kernel, outs=outs, tiles_per_seq=tiles_per_seq)
    specs, shapes = [], []
    for _, wd, _, dt, mode in outs:
        if mode == "by_head":
            blk, rows, imap = ((wd // LANES) * tm, LANES), (wd // LANES) * n, lambda i: (i, 0)
        elif mode == "last_t":
            blk, rows, imap = (wd, tm), (n // tm // tiles_per_seq) * wd, lambda i: (i // tiles_per_seq, 0)
        else:
            blk, rows, imap = (tm, wd), n, lambda i: (i, 0)
        specs.append(pl.BlockSpec(blk, imap))
        shapes.append(jax.ShapeDtypeStruct((rows, blk[1]), dt))
    return pl.pallas_call(
        kern,
        grid=(n // tm,),
        in_specs=[
            pl.BlockSpec((tm, d), lambda i: (i, 0)),
            _const_spec((1, d)),
            _const_spec(w.shape),
        ],
        out_specs=specs,
        out_shape=shapes,
        compiler_params=_params(("arbitrary",)),
        name="norm_proj",
    )(x, g, w)


def _band_kernel(*refs, tq, seq, use_hist):
    if use_hist:
        q_ref, k_ref, v_ref, hk_ref, hv_ref, bias_ref, o_ref, kp, vp, s_scr, qt_scr = refs
    else:
        q_ref, k_ref, v_ref, bias_ref, o_ref, kp, vp, s_scr, qt_scr = refs
    win = WIN_A + tq
    nblk = seq // tq
    npair = H_A // 2

    if use_hist:
        kp[0:WIN_A, :] = hk_ref[0].T.astype(BF16)
        vp[0:WIN_A, :] = hv_ref[0].T.astype(BF16)
    else:
        kp[0:WIN_A, :] = jnp.zeros((WIN_A, kp.shape[1]), BF16)
        vp[0:WIN_A, :] = jnp.zeros((WIN_A, vp.shape[1]), BF16)
    kp[WIN_A:WIN_A + seq, :] = k_ref[0].astype(BF16)
    vp[WIN_A:WIN_A + seq, :] = v_ref[0].astype(BF16)

    feat = lax.broadcasted_iota(jnp.int32, (LANES, tq), 0)
    lo = feat < DH_A
    slabs = [slice(hp * LANES, (hp + 1) * LANES) for hp in range(npair)]

    for blk in range(nblk):
        for hp in range(npair):
            qt_scr[hp, blk] = _split_maps_t(q_ref[0, blk * tq:(blk + 1) * tq, slabs[hp]])

    def stage_a(m, hp, mask_history, slot):
        off = m * tq if isinstance(m, int) else pl.multiple_of(m * tq, tq)
        s = _dot(kp[pl.ds(off, win), slabs[hp]], qt_scr[hp, m]) + bias_ref[hp]
        if mask_history:
            key = lax.broadcasted_iota(jnp.int32, (win, 2 * tq), 0)
            s = jnp.where(key >= (WIN_A - m * tq), s, NEG_INF)
        s_scr[slot, hp] = s
        return jnp.max(s, axis=0, keepdims=True)

    def stage_b(m, hp, mx, slot):
        off = pl.multiple_of(m * tq, tq)
        p = jnp.exp2(s_scr[slot, hp] - mx)
        l = jnp.sum(p, axis=0, keepdims=True)
        o = _dot_tn(vp[pl.ds(off, win), slabs[hp]], p.astype(BF16)) / l
        o_ref[0, pl.ds(off, tq), slabs[hp]] = jnp.where(lo, o[:, :tq], o[:, tq:]).T.astype(o_ref.dtype)

    nmask = 0 if use_hist else min(WIN_A // tq, nblk)
    wide = s_scr.shape[0]
    assert nblk % wide == 0 and nmask % wide == 0

    def group(mask_history, g, mxs):
        blocks = [g * wide + i for i in range(wide)]
        for hp in range(1, npair):
            nxt = []
            for i in range(wide):
                nxt.append(stage_a(blocks[i], hp, mask_history, i))
                stage_b(blocks[i], hp - 1, mxs[i], i)
            mxs = nxt
        nxt = []
        for i in range(wide):
            nxt.append(stage_a(jnp.minimum(blocks[i] + wide, nblk - wide + i), 0, mask_history, i))
            stage_b(blocks[i], npair - 1, mxs[i], i)
        return tuple(nxt)

    mxs = tuple(stage_a(i, 0, nmask > 0, i) for i in range(wide))
    unroll = lambda trips: 2 if trips % 2 == 0 else 1
    mxs = lax.fori_loop(0, nmask // wide, functools.partial(group, True), mxs, unroll=unroll(nmask // wide))
    lax.fori_loop(nmask // wide, nblk // wide, functools.partial(group, False), mxs,
                  unroll=unroll((nblk - nmask) // wide))


def _band_blocks_in_flight(nblk):
    return 2 if nblk % 2 == 0 else 1


def _band_attn(q, k, v, bias, tq, hist=None):
    b, seq, da = q.shape
    win = WIN_A + tq
    use_hist = hist is not None
    kern = functools.partial(_band_kernel, tq=tq, seq=seq, use_hist=use_hist)
    full = pl.BlockSpec((1, seq, da), lambda i: (i, 0, 0))
    in_specs = [full, full, full]
    args = [q, k, v]
    if use_hist:
        hspec = pl.BlockSpec((1, da, WIN_A), lambda i: (i, 0, 0))
        in_specs += [hspec, hspec]
        args += list(hist)
    in_specs.append(_const_spec(bias.shape))
    args.append(bias)
    return pl.pallas_call(
        kern,
        grid=(b,),
        in_specs=in_specs,
        out_specs=full,
        out_shape=jax.ShapeDtypeStruct((b, seq, da), BF16),
        scratch_shapes=[pltpu.VMEM((WIN_A + seq, da), BF16), pltpu.VMEM((WIN_A + seq, da), BF16),
                        pltpu.VMEM((_band_blocks_in_flight(seq // tq), H_A // 2, win, 2 * tq), F32),
                        pltpu.VMEM((H_A // 2, seq // tq, LANES, 2 * tq), BF16)],
        compiler_params=_params(("parallel",)),
        name="band_attn",
    )(*args)


BIAS_LANE0 = WIN_A + REL_CLIP
BIAS_WIDTH = 768


def _band_bias_kernel(rb_ref, o_ref, *, tq, win):
    z = lax.broadcasted_iota(jnp.int32, (1, BIAS_WIDTH), 1)
    j = lax.broadcasted_iota(jnp.int32, (win, tq), 0)
    r = lax.broadcasted_iota(jnp.int32, (win, tq), 1)
    kc = j // CHUNK
    qc = r // CHUNK
    band = (kc >= qc) & (kc <= qc + BAND_CHUNKS)
    for h in range(H_A):
        row = jnp.where(z >= 2 * REL_CLIP, rb_ref[h:h + 1, 2 * REL_CLIP:2 * REL_CLIP + 1], rb_ref[h:h + 1, :])
        skew = pltpu.roll(jnp.broadcast_to(row, (win, BIAS_WIDTH)), 0, axis=1, stride=1, stride_axis=0)
        tbl = jnp.where(band, skew[:, BIAS_LANE0:BIAS_LANE0 + tq] * LOG2E, NEG_INF)
        o_ref[h // 2, :, (h % 2) * tq:(h % 2 + 1) * tq] = tbl


def _band_bias_table(rel_bias, tq):
    win = WIN_A + tq
    assert BIAS_LANE0 + tq <= BIAS_WIDTH and win <= BIAS_WIDTH
    rb = jnp.pad(rel_bias.astype(F32), ((0, 0), (0, BIAS_WIDTH - rel_bias.shape[1])))
    return pl.pallas_call(
        functools.partial(_band_bias_kernel, tq=tq, win=win),
        out_shape=jax.ShapeDtypeStruct((H_A // 2, win, 2 * tq), F32),
        compiler_params=pltpu.CompilerParams(vmem_limit_bytes=VMEM_LIMIT),
        name="band_bias",
    )(rb)


def _split_maps_t(q):
    qt = q.astype(F32).T
    feat = lax.broadcasted_iota(jnp.int32, qt.shape, 0)
    lo = feat < DH_B
    return jnp.concatenate([jnp.where(lo, qt, 0.0), jnp.where(lo, 0.0, qt)], axis=1).astype(BF16)


def _key_query_diff(tk, tq):
    c = lax.broadcasted_iota(jnp.int32, (tk, 2 * tq), 0)
    r = lax.broadcasted_iota(jnp.int32, (tk, 2 * tq), 1)
    rr = jnp.where(r >= tq, r - tq, r)
    return rr, c


def _alibi_slope(h):
    return 2.0 ** (-8.0 * (h + 1) / H_B) * LOG2E


def _online_update(s, vb, shift, carry):
    m, l, acc = carry
    m_new = jnp.maximum(m, jnp.max(s, axis=0, keepdims=True) + shift)
    alpha = jnp.exp2(m - m_new)
    p = jnp.exp2(s - (m_new - shift))
    l = alpha * l + jnp.sum(p, axis=0, keepdims=True)
    acc = alpha * acc + _dot_tn(vb, p.astype(BF16))
    return m_new, l, acc


def _diff_finish_t(carry, tq, lam, gsub, lam_init):
    _, l, acc = carry
    o = acc / l
    od = (o[:, :tq] - lam * o[:, tq:]).T
    return _rms(od, gsub) * (1.0 - lam_init)


def _diff_lambda(lq_ref, lk_ref, lam_init):
    e = jnp.exp(jnp.sum(lq_ref[...] * lk_ref[...], axis=-1, keepdims=True))
    return e[0:1] - e[1:2] + lam_init


def _diff_init_t(tq):
    return (jnp.full((1, 2 * tq), NEG_INF, F32), jnp.zeros((1, 2 * tq), F32),
            jnp.zeros((LANES, 2 * tq), F32))


def _diff_prompt_kernel(q_ref, k_ref, v_ref, lq_ref, lk_ref, gsub_ref, o_ref,
                        bias_scr, qq_scr, s0_scr, s1_scr, acc_scr, *, tq, lam_init):
    sp = pl.program_id(1)
    slabs = [slice(h * LANES, (h + 1) * LANES) for h in range(H_B)]
    chains = [(e, bb, h) for e in range(2) for bb in range(q_ref.shape[0]) for h in range(H_B)]
    every = list(range(len(chains)))
    even = [c for c in every if chains[c][0] == 0]
    odd = [c for c in every if chains[c][0] == 1]

    @pl.when(sp == 0)
    def _():
        rr, c = _key_query_diff(tq, tq)
        d = (rr - c).astype(F32)
        allowed = (c // CHUNK) <= (rr // CHUNK)
        for h in range(H_B):
            bias_scr[0, h] = -_alibi_slope(h) * d
            bias_scr[1, h] = jnp.where(allowed, -_alibi_slope(h) * jnp.abs(d), NEG_INF)

    for c, (e, bb, h) in enumerate(chains):
        qq_scr[c] = _split_maps_t(q_ref[bb, e * tq:(e + 1) * tq, slabs[h]])
    acc_scr[...] = jnp.zeros(acc_scr.shape, F32)
    bufs = (s0_scr, s1_scr)

    def chain_a(j, buf, c, m):
        e, bb, h = chains[c]
        qi = 2 * sp + e
        off = j * tq if isinstance(j, int) else pl.multiple_of(j * tq, tq)
        diag = (j == qi).astype(jnp.int32)
        s = _dot(k_ref[bb, pl.ds(off, tq), slabs[h]], qq_scr[c]) + bias_scr[diag, h]
        bufs[buf][c] = s
        shift = -_alibi_slope(h) * ((qi - j) * tq).astype(F32)
        m_new = jnp.maximum(m, jnp.max(s, axis=0, keepdims=True) + shift)
        return m_new, jnp.exp2(m - m_new), m_new - shift

    def chain_b(j, buf, c, l, alpha, sub):
        e, bb, h = chains[c]
        off = j * tq if isinstance(j, int) else pl.multiple_of(j * tq, tq)
        p = jnp.exp2(bufs[buf][c] - sub)
        acc_scr[c] = alpha * acc_scr[c] + _dot_tn(v_ref[bb, pl.ds(off, tq), slabs[h]], p.astype(BF16))
        return alpha * l + jnp.sum(p, axis=0, keepdims=True)

    def stages(a_spec, b_spec, state):
        ms, ls, pend = [list(x) for x in state[:2]] + [[[list(x) for x in p] for p in state[2]]]
        for c in every:
            if a_spec is not None and c in a_spec[2]:
                ms[c], pend[a_spec[1]][0][c], pend[a_spec[1]][1][c] = chain_a(a_spec[0], a_spec[1], c, ms[c])
            if b_spec is not None and c in b_spec[2]:
                ls[c] = chain_b(b_spec[0], b_spec[1], c, ls[c], pend[b_spec[1]][0][c], pend[b_spec[1]][1][c])
        return tuple(ms), tuple(ls), tuple((tuple(p[0]), tuple(p[1])) for p in pend)

    def finish(which, ls):
        lam = _diff_lambda(lq_ref, lk_ref, lam_init)
        for c in which:
            e, bb, h = chains[c]
            o_ref[bb, e * tq:(e + 1) * tq, slabs[h]] = _diff_finish_t(
                (None, ls[c], acc_scr[c]), tq, lam, gsub_ref[...], lam_init).astype(o_ref.dtype)

    row = lambda val: tuple(jnp.full((1, 2 * tq), val, F32) for _ in chains)
    state = (row(NEG_INF), row(0.0), ((row(0.0), row(0.0)), (row(0.0), row(0.0))))
    state = stages((0, 0, every), None, state)

    def two_blocks(jj, state):
        j = 2 * jj
        state = stages((j + 1, 1, every), (j, 0, every), state)
        return stages((j + 2, 0, every), (j + 1, 1, every), state)

    state = lax.fori_loop(0, sp, two_blocks, state)
    state = stages((2 * sp + 1, 1, odd), (2 * sp, 0, every), state)
    finish(even, state[1])
    state = stages(None, (2 * sp + 1, 1, odd), state)
    finish(odd, state[1])


def _diff_attn_prompt(q, k, v, lam_q, lam_k, g_sub, tq, lam_init):
    b, seq, db = q.shape
    nbat = 2 if b % 2 == 0 else 1
    kern = functools.partial(_diff_prompt_kernel, tq=tq, lam_init=lam_init)
    assert seq % (2 * tq) == 0
    nchain = 2 * nbat * H_B
    kv_spec = pl.BlockSpec((nbat, seq, db), lambda i, j: (i, 0, 0))
    q_spec = pl.BlockSpec((nbat, 2 * tq, db), lambda i, j: (i, j, 0))
    return pl.pallas_call(
        kern,
        grid=(b // nbat, seq // (2 * tq)),
        in_specs=[q_spec, kv_spec, kv_spec,
                  _const_spec(lam_q.shape), _const_spec(lam_k.shape), _const_spec(g_sub.shape)],
        out_specs=q_spec,
        out_shape=jax.ShapeDtypeStruct((b, seq, db), BF16),
        scratch_shapes=[pltpu.VMEM((2, H_B, tq, 2 * tq), F32), pltpu.VMEM((nchain, LANES, 2 * tq), BF16),
                        pltpu.VMEM((nchain, tq, 2 * tq), F32), pltpu.VMEM((nchain, tq, 2 * tq), F32),
                        pltpu.VMEM((nchain, LANES, 2 * tq), F32)],
        compiler_params=_params(("parallel", "arbitrary")),
        name="diff_attn_prompt",
    )(q, k, v, lam_q, lam_k, g_sub)


def _diff_sample_kernel(q_ref, kc_ref, vc_ref, kn_ref, vn_ref, lq_ref, lk_ref, gsub_ref, o_ref,
                        m_s, l_s, acc_s, s_scr, *, t, past, tk, ts, lam_init):
    ci = pl.program_id(1)

    @pl.when(ci == 0)
    def _():
        for h in range(H_B):
            m_s[h], l_s[h], acc_s[h] = _diff_init_t(t)

    slabs = [slice(h * LANES, (h + 1) * LANES) for h in range(H_B)]
    qqts = [_split_maps_t(q_ref[0, :, cs]) for cs in slabs]
    rr, c = _key_query_diff(ts, t)
    d = (rr - c).astype(F32)
    rows = lambda ref, h, n, first=0: _head_rows(ref, h, n, H_B, lead=(0,), first=first)[...].astype(BF16)
    nstream = s_scr.shape[0]
    per = H_B // nstream
    items = [[(sub, h) for sub in range(tk // ts) for h in range(st * per, (st + 1) * per)]
             for st in range(nstream)]
    nitem = len(items[0])

    def stage_a(st, i):
        sub, h = items[st][i]
        s = _dot(rows(kc_ref, h, ts, sub * ts), qqts[h]) - _alibi_slope(h) * d
        s_scr[st, i % 2] = s
        shift = -_alibi_slope(h) * (past - ci * tk - sub * ts).astype(F32)
        m_new = jnp.maximum(m_s[h], jnp.max(s, axis=0, keepdims=True) + shift)
        alpha = jnp.exp2(m_s[h] - m_new)
        m_s[h] = m_new
        return alpha, m_new - shift

    def stage_b(st, i, alpha, ref_max):
        sub, h = items[st][i]
        p = jnp.exp2(s_scr[st, i % 2] - ref_max)
        l_s[h] = alpha * l_s[h] + jnp.sum(p, axis=0, keepdims=True)
        acc_s[h] = alpha * acc_s[h] + _dot_tn(rows(vc_ref, h, ts, sub * ts), p.astype(BF16))

    pend = [stage_a(st, 0) for st in range(nstream)]
    for i in range(1, nitem):
        nxt = [stage_a(st, i) for st in range(nstream)]
        for st in range(nstream):
            stage_b(st, i - 1, *pend[st])
        pend = nxt
    for st in range(nstream):
        stage_b(st, nitem - 1, *pend[st])

    @pl.when(ci == past // tk - 1)
    def _():
        rn, cn = _key_query_diff(t, t)
        dn = jnp.abs((rn - cn).astype(F32))
        lam = _diff_lambda(lq_ref, lk_ref, lam_init)
        for h, cs in enumerate(slabs):
            s = _dot(rows(kn_ref, h, t), qqts[h]) - _alibi_slope(h) * dn
            carry = _online_update(s, rows(vn_ref, h, t), 0.0, (m_s[h], l_s[h], acc_s[h]))
            o_ref[0, :, cs] = _diff_finish_t(carry, t, lam, gsub_ref[...], lam_init).astype(o_ref.dtype)


def _diff_attn_sample(q, kc, vc, kn, vn, lam_q, lam_k, g_sub, tk, ts, lam_init):
    b, t, db = q.shape
    past = kc.shape[1] // H_B
    kern = functools.partial(_diff_sample_kernel, t=t, past=past, tk=tk, ts=ts, lam_init=lam_init)
    new_spec = pl.BlockSpec((1, t, db), lambda i, j: (i, 0, 0))
    new_kv_spec = pl.BlockSpec((1, t * H_B, LANES), lambda i, j: (i, 0, 0))
    cache_spec = pl.BlockSpec((1, tk * H_B, LANES), lambda i, j: (i, j, 0))
    return pl.pallas_call(
        kern,
        grid=(b, past // tk),
        in_specs=[new_spec, cache_spec, cache_spec, new_kv_spec, new_kv_spec,
                  _const_spec(lam_q.shape), _const_spec(lam_k.shape), _const_spec(g_sub.shape)],
        out_specs=new_spec,
        out_shape=jax.ShapeDtypeStruct((b, t, db), BF16),
        scratch_shapes=[pltpu.VMEM((H_B, 1, 2 * t), F32), pltpu.VMEM((H_B, 1, 2 * t), F32),
                        pltpu.VMEM((H_B, LANES, 2 * t), F32), pltpu.VMEM((2, 2, ts, 2 * t), F32)],
        compiler_params=_params(("parallel", "arbitrary")),
        name="diff_attn_sample",
    )(q, kc, vc, kn, vn, lam_q, lam_k, g_sub)


def _mem_kernel(q_ref, mk_ref, mv_ref, o_ref):
    slabs = [slice(h * LANES, (h + 1) * LANES) for h in range(H_M)]
    nm = mk_ref.shape[1] // H_M
    rows = lambda ref, h: _head_rows(ref, h, nm, H_M, lead=(0,))[...].astype(BF16)
    scores = [_dot_nt(q_ref[0, :, cs], rows(mk_ref, h)) for h, cs in enumerate(slabs)]
    for h, cs in enumerate(slabs):
        mx = jnp.max(scores[h], axis=-1, keepdims=True)
        p = jnp.exp2(scores[h] - mx)
        l = jnp.sum(p, axis=-1, keepdims=True)
        o_ref[0, :, cs] = (_dot(p.astype(BF16), rows(mv_ref, h)) / l).astype(o_ref.dtype)


def _mem_attn(q, mk, mv, tq):
    b, seq, dm = q.shape
    kv_spec = pl.BlockSpec((1,) + mk.shape[1:], lambda i, j: (i, 0, 0))
    q_spec = pl.BlockSpec((1, tq, dm), lambda i, j: (i, j, 0))
    return pl.pallas_call(
        _mem_kernel,
        grid=(b, seq // tq),
        in_specs=[q_spec, kv_spec, kv_spec],
        out_specs=q_spec,
        out_shape=jax.ShapeDtypeStruct((b, seq, dm), BF16),
        compiler_params=_params(("parallel", "parallel")),
        name="mem_attn",
    )(q, mk, mv)


def _merge_kernel(x_ref, oa_ref, ob_ref, om_ref, g_ref, win_ref, bg_ref, wo_ref, wout_ref, o_ref, m_scr,
                  *, tn):
    x = x_ref[...]
    h = _rms(x, g_ref[...]).astype(BF16)
    o_refs = (oa_ref, ob_ref, om_ref)
    d = x.shape[1]
    for c0 in range(0, d, tn):
        cs = slice(c0, c0 + tn)
        m = None
        for br in range(N_BRANCH):
            g0 = br * d + c0
            goff = win_ref.shape[1] - N_BRANCH * d
            gate = jax.nn.sigmoid(_dot(h, win_ref[:, goff + g0:goff + g0 + tn]) + bg_ref[:, g0:g0 + tn])
            term = gate * _dot(o_refs[br][...], wo_ref[br, :, cs])
            m = term if m is None else m + term
        m_scr[:, cs] = m.astype(BF16)
    o_ref[...] = x + _dot(m_scr[...], wout_ref[...])


def _merge(x, oa, ob, om, g, wg, bg, wo, wout, tm, tn=256):
    n, d = x.shape
    kern = functools.partial(_merge_kernel, tn=tn)
    tok = lambda wd: pl.BlockSpec((tm, wd), lambda i: (i, 0))
    return pl.pallas_call(
        kern,
        grid=(n // tm,),
        in_specs=[tok(d), tok(oa.shape[1]), tok(ob.shape[1]), tok(om.shape[1]), _const_spec(g.shape),
                  _const_spec(wg.shape), _const_spec(bg.shape), _const_spec(wo.shape),
                  _const_spec(wout.shape)],
        out_specs=tok(d),
        out_shape=jax.ShapeDtypeStruct((n, d), F32),
        scratch_shapes=[pltpu.VMEM((tm, d), BF16)],
        compiler_params=_params(("parallel",)),
        name="merge",
    )(x, oa, ob, om, g, wg, bg, wo, wout)


def _ffn_kernel(x_ref, g_ref, prev_ref, wup_ref, wconv_ref, bconv_ref, wdown_ref, gfin_ref,
                y_ref, conv_ref, carry, u_scr, *, tm, nch, nseq, final_norm):
    if nseq == 1:
        @pl.when(pl.program_id(1) == 0)
        def _():
            carry[...] = prev_ref[0, 0]

    x = x_ref[0]
    h = _rms(x, g_ref[...]).astype(BF16)
    row = lax.broadcasted_iota(jnp.int32, (tm, FF_CHUNK), 0)
    seq_rows = tm // nseq
    ff = nch * FF_CHUNK
    for ch in range(nch):
        cs = slice(ch * FF_CHUNK, (ch + 1) * FF_CHUNK)
        a = _dot(h, wup_ref[:, cs])
        gate = _dot(h, wup_ref[:, ff + ch * FF_CHUNK:ff + (ch + 1) * FF_CHUNK])
        a1 = pltpu.roll(a, 1, axis=0)
        a2 = pltpu.roll(a, 2, axis=0)
        for k in range(nseq):
            before = carry if nseq == 1 else prev_ref.at[0, k]
            p2 = before[0:1, cs]
            p1 = before[1:2, cs]
            a1 = jnp.where(row == k * seq_rows, p1, a1)
            a2 = jnp.where(row == k * seq_rows, p2, jnp.where(row == k * seq_rows + 1, p1, a2))
        conv = bconv_ref[:, cs] + a2 * wconv_ref[0:1, cs]
        conv = conv + a1 * wconv_ref[1:2, cs]
        conv = conv + a * wconv_ref[2:3, cs]
        if nseq == 1:
            carry[:, cs] = a[tm - (CONV_W - 1):, :]
        else:
            for k in range(nseq):
                conv_ref[0, k, :, cs] = a[(k + 1) * seq_rows - (CONV_W - 1):(k + 1) * seq_rows, :]
        u_scr[:, cs] = (jax.nn.gelu(conv) * gate).astype(BF16)
    y = x + _dot(u_scr[...], wdown_ref[...])
    if final_norm:
        y = _rms(y, gfin_ref[...])
    y_ref[0] = y
    if nseq == 1:
        conv_ref[0, 0] = carry[...]


def _conv_ffn(x, g, prev, wup, wconv, bconv, wdown, gfin, tm, final_norm):
    b, seq, d = x.shape
    ff = wdown.shape[0]
    nch = ff // FF_CHUNK
    nseq = max(tm // seq, 1)
    groups = b // nseq
    assert (seq % tm == 0) if nseq == 1 else (tm % seq == 0 and b % nseq == 0)
    kern = functools.partial(_ffn_kernel, tm=tm, nch=nch, nseq=nseq, final_norm=final_norm)
    tok = pl.BlockSpec((1, tm, d), lambda i, j: (i, j, 0))
    state = pl.BlockSpec((1, nseq, CONV_W - 1, ff), lambda i, j: (i, 0, 0, 0))
    y, conv = pl.pallas_call(
        kern,
        grid=(groups, nseq * seq // tm),
        in_specs=[tok, _const_spec(g.shape), state, _const_spec(wup.shape), _const_spec(wconv.shape),
                  _const_spec(bconv.shape), _const_spec(wdown.shape), _const_spec(gfin.shape)],
        out_specs=[tok, state],
        out_shape=[jax.ShapeDtypeStruct((groups, nseq * seq, d), F32),
                   jax.ShapeDtypeStruct((groups, nseq, CONV_W - 1, ff), F32)],
        scratch_shapes=[pltpu.VMEM((CONV_W - 1, ff), F32), pltpu.VMEM((tm, ff), BF16)],
        compiler_params=_params(("parallel", "arbitrary")),
        name="conv_ffn",
    )(x.reshape(groups, nseq * seq, d), g, prev.reshape(groups, nseq, CONV_W - 1, ff),
      wup, wconv, bconv, wdown, gfin)
    return y.reshape(b, seq, d), conv.reshape(b, CONV_W - 1, ff)


def _token_tile(n, pref):
    return pref if n % pref == 0 else n


def _tiles(b, t, prompt):
    n = b * t
    if t % 1024 == 0:
        ffn = 1024
    elif t % 512 == 0 or (512 % t == 0 and n % 512 == 0):
        ffn = 512
    else:
        ffn = t
    return dict(
        proj=_token_tile(n, WIN_A),
        merge=_token_tile(n, 1024),
        ffn=ffn,
        band_q=128 if prompt else t,
        diff_q=256,
        mem_q=_token_tile(t, 1024),
        cache_keys=2048,
        cache_item=512,
    )


def _layer(x, w, *, lam_init, final_norm, conv_prev, mem=None, cache=None):
    b, t, d = x.shape
    n = b * t
    tile = _tiles(b, t, cache is None)
    tm = tile["proj"]
    d_a, d_b, d_m = H_A * DH_A, H_B * 2 * DH_B, H_M * DH_M
    prompt = cache is None
    assert not prompt or (tm == WIN_A and t % tm == 0)
    kv_a = [(BF16, "rows"), (F32, "last_t")] if prompt else [(F32, "rows")]
    kv_b = [(BF16, "rows"), (F32, "by_head")] if prompt else [(F32, "by_head")]
    outs = ([(0, d_a, DH_A ** -0.5 * LOG2E, BF16, "rows")]
            + [(d_a, d_a, 1.0, dt, mode) for dt, mode in kv_a]
            + [(2 * d_a, d_a, 1.0, dt, mode) for dt, mode in kv_a]
            + [(3 * d_a, d_b, DH_B ** -0.5 * LOG2E, BF16, "rows")]
            + [(3 * d_a + d_b, d_b, 1.0, dt, mode) for dt, mode in kv_b]
            + [(3 * d_a + 2 * d_b, d_b, 1.0, dt, mode) for dt, mode in kv_b]
            + [(3 * d_a + 3 * d_b, d_m, DH_M ** -0.5 * LOG2E, BF16, "rows")])
    res = _norm_proj(x.reshape(n, d), w["g_mix"], w["w_in"], outs, tm, tiles_per_seq=max(t // tm, 1))
    res = [a.reshape(b, -1, a.shape[-1]) for a in res]
    if prompt:
        qa, ka, ka_keep, va, va_keep, qb, kb_bf, kb, vb_bf, vb, qm = res
    else:
        qa, ka, va, qb, kb, vb, qm = res
        ka_keep, va_keep = ka, va

    if cache is None:
        oa = _band_attn(qa, ka, va, _band_bias_table(w["rel_bias"], tile["band_q"]), tile["band_q"])
        ob = _diff_attn_prompt(qb, kb_bf, vb_bf, w["lam_q"], w["lam_k"], w["g_sub"], tile["diff_q"], lam_init)
        nm = mem.shape[1]
        mk, mv = _norm_proj(mem.reshape(b * nm, d), w["g_mem"], w["w_mem_kv"],
                            [(0, d_m, 1.0, F32, "by_head"), (d_m, d_m, 1.0, F32, "by_head")],
                            _token_tile(b * nm, WIN_A))
        mk = mk.reshape(b, nm * H_M, DH_M)
        mv = mv.reshape(b, nm * H_M, DH_M)
        om = _mem_attn(qm, mk, mv, tile["mem_q"])
    else:
        oa = _band_attn(qa, ka, va, _band_bias_table(w["rel_bias"], tile["band_q"]), tile["band_q"],
                        hist=(cache["a_k_t"], cache["a_v_t"]))
        ob = _diff_attn_sample(qb, cache["b_k"], cache["b_v"], kb, vb,
                               w["lam_q"], w["lam_k"], w["g_sub"], tile["cache_keys"], tile["cache_item"], lam_init)
        mk, mv = cache["m_k"], cache["m_v"]
        om = _mem_attn(qm, mk, mv, tile["mem_q"])

    flat = lambda a: a.reshape(n, a.shape[-1])
    x1 = _merge(x.reshape(n, d), flat(oa), flat(ob), flat(om), w["g_mix"], w["w_in"], w["b_gate"],
                w["w_o"], w["w_out"], tile["merge"])
    y, conv = _conv_ffn(x1.reshape(b, t, d), w["g_ffn"], conv_prev, w["w_up"], w["w_conv"], w["b_conv"],
                        w["w_down"], w["g_final"], tile["ffn"], final_norm)
    return y, dict(ka=ka_keep, va=va_keep, kb=kb, vb=vb, mk=mk, mv=mv, conv=conv)


def kernel(x_prompt, x_sample, mem_prompt, cache_a_k, cache_a_v, cache_b_k, cache_b_v, cache_mem_k, cache_mem_v, state_ffn_conv, g_mix, w_in, b_gate, rel_bias, lam_q, lam_k, g_sub, g_mem, w_mem_kv, w_oa, w_ob, w_om, w_out, g_ffn, w_up, w_conv, b_conv, w_down, g_final):
    depth, d, _ = w_in.shape
    b, s, _ = x_prompt.shape
    bd, t, _ = x_sample.shape
    past = cache_b_k.shape[2]
    a_buf = cache_a_k.shape[2]
    ff = w_down.shape[1]
    d_a, d_b, d_m = H_A * DH_A, H_B * 2 * DH_B, H_M * DH_M
    a_keep = min(WIN_A, s)
    assert a_buf == WIN_A and a_keep == WIN_A and t == CHUNK and past % CHUNK == 0 and ff % FF_CHUNK == 0
    nch = ff // FF_CHUNK

    xp, xs = x_prompt, x_sample
    outs = {k: [] for k in ("pa_k", "pa_v", "pb_k", "pb_v", "pm_k", "pm_v", "pconv",
                            "sa_k", "sa_v", "sb_k", "sb_v", "sconv")}
    for l in range(depth):
        lam_init = 0.8 - 0.6 * math.exp(-0.3 * l)
        w = dict(
            g_mix=g_mix[l][None], g_mem=g_mem[l][None], g_ffn=g_ffn[l][None], g_final=g_final[None],
            g_sub=g_sub[l][None], lam_q=lam_q[l], lam_k=lam_k[l], rel_bias=rel_bias[l],
            w_in=w_in[l].astype(BF16),
            b_gate=b_gate[l][None],
            w_mem_kv=w_mem_kv[l].astype(BF16),
            w_o=jnp.stack([w_oa[l], w_ob[l], w_om[l]]).astype(BF16),
            w_out=w_out[l].astype(BF16),
            w_up=w_up[l].astype(BF16),
            w_conv=w_conv[l], b_conv=b_conv[l][None],
            w_down=w_down[l].astype(BF16),
        )
        last = l == depth - 1
        xp, po = _layer(xp, w, lam_init=lam_init, final_norm=last,
                        conv_prev=jnp.zeros((b, CONV_W - 1, ff), F32), mem=mem_prompt)
        feat_major = lambda a: a.transpose(0, 2, 3, 1).reshape(bd, d_a, a_buf)
        cache = dict(a_k=cache_a_k[l].reshape(bd, a_buf, d_a), a_v=cache_a_v[l].reshape(bd, a_buf, d_a),
                     a_k_t=feat_major(cache_a_k[l]), a_v_t=feat_major(cache_a_v[l]),
                     b_k=cache_b_k[l].reshape(bd, past * H_B, 2 * DH_B),
                     b_v=cache_b_v[l].reshape(bd, past * H_B, 2 * DH_B),
                     m_k=cache_mem_k[l].reshape(bd, -1, DH_M), m_v=cache_mem_v[l].reshape(bd, -1, DH_M))
        xs, so = _layer(xs, w, lam_init=lam_init, final_norm=last, conv_prev=state_ffn_conv[l], cache=cache)

        tok_major = lambda a: a.reshape(b, H_A, DH_A, a_keep).transpose(0, 3, 1, 2)
        outs["pa_k"].append(tok_major(po["ka"]))
        outs["pa_v"].append(tok_major(po["va"]))
        outs["pb_k"].append(po["kb"].reshape(b, s, H_B, 2 * DH_B))
        outs["pb_v"].append(po["vb"].reshape(b, s, H_B, 2 * DH_B))
        outs["pm_k"].append(po["mk"].reshape(b, -1, H_M, DH_M))
        outs["pm_v"].append(po["mv"].reshape(b, -1, H_M, DH_M))
        outs["pconv"].append(po["conv"])
        new_k = jnp.concatenate([cache["a_k"], so["ka"]], axis=1)[:, t:]
        new_v = jnp.concatenate([cache["a_v"], so["va"]], axis=1)[:, t:]
        outs["sa_k"].append(new_k.reshape(bd, a_buf, H_A, DH_A))
        outs["sa_v"].append(new_v.reshape(bd, a_buf, H_A, DH_A))
        outs["sb_k"].append(so["kb"].reshape(bd, t, H_B, 2 * DH_B))
        outs["sb_v"].append(so["vb"].reshape(bd, t, H_B, 2 * DH_B))
        outs["sconv"].append(so["conv"])

    st = {k: jnp.stack(v) for k, v in outs.items()}
    return (xp, xs, st["pa_k"], st["pa_v"], st["pb_k"], st["pb_v"], st["pm_k"], st["pm_v"], st["pconv"],
            st["sa_k"], st["sa_v"], st["sb_k"], st["sb_v"], st["sconv"])
```

```python
import functools
import math

import jax
import jax.numpy as jnp
from jax import lax
from jax.experimental import pallas as pl
from jax.experimental.pallas import tpu as pltpu

CHUNK = 64
BAND_CHUNKS = 8
WIN_A = BAND_CHUNKS * CHUNK
REL_CLIP = 128
H_A = 8
DH_A = 64
H_B = 4
DH_B = 64
H_M = 4
DH_M = 128
N_BRANCH = 3
CONV_W = 3
EPS = 1e-6
NEG_INF = -1e30

LANES = 128
FF_CHUNK = 256
VMEM_LIMIT = 56 * 1024 * 1024

LOG2E = math.log2(math.e)

BF16 = jnp.bfloat16
F32 = jnp.float32


def _rms(x, g):
    y = x * lax.rsqrt(jnp.mean(x * x, axis=-1, keepdims=True) + EPS)
    return y * g


def _dot(a, b):
    return jnp.dot(a, b, preferred_element_type=F32)


def _dot_nt(a, b):
    return lax.dot_general(a, b, (((1,), (1,)), ((), ())), preferred_element_type=F32)


def _dot_tn(a, b):
    return lax.dot_general(a, b, (((0,), (0,)), ((), ())), preferred_element_type=F32)


def _const_spec(shape):
    nd = len(shape)
    return pl.BlockSpec(shape, lambda *_: (0,) * nd, pipeline_mode=pl.Buffered(1))


def _params(sem, flags=None):
    return pltpu.CompilerParams(dimension_semantics=sem, vmem_limit_bytes=VMEM_LIMIT, flags=flags)


def _head_rows(ref, h, n, heads, lead=(), first=0):
    return ref.at[lead + (pl.ds(first * heads + h, n, stride=heads), slice(None))]


def _norm_proj_kernel(x_ref, g_ref, w_ref, *o_refs, outs, tiles_per_seq):
    tm = x_ref.shape[0]
    last_tile = pl.program_id(0) % tiles_per_seq == tiles_per_seq - 1
    h = _rms(x_ref[...], g_ref[...]).astype(BF16)
    z_of = {}
    transposed = []
    for o_ref, (c0, wd, sc, _, mode) in zip(o_refs, outs):
        if (c0, wd) not in z_of:
            z_of[(c0, wd)] = _dot(h, w_ref[:, c0:c0 + wd])
        z = z_of[(c0, wd)]
        if sc != 1.0:
            z = z * sc
        if mode == "by_head":
            heads = wd // LANES
            for hd in range(heads):
                _head_rows(o_ref, hd, tm, heads)[...] = z[:, hd * LANES:(hd + 1) * LANES].astype(o_ref.dtype)
        elif mode == "last_t":
            transposed.append((o_ref, z))
        else:
            o_ref[...] = z.astype(o_ref.dtype)

    if transposed:
        @pl.when(last_tile)
        def _():
            for o_ref, z in transposed:
                o_ref[...] = z.T.astype(o_ref.dtype)


def _norm_proj(x, g, w, outs, tm, tiles_per_seq=1):
    n, d = x.shape
    kern = functools.partial(_norm_proj_kernel, outs=outs, tiles_per_seq=tiles_per_seq)
    specs, shapes = [], []
    for _, wd, _, dt, mode in outs:
        if mode == "by_head":
            blk, rows, imap = ((wd // LANES) * tm, LANES), (wd // LANES) * n, lambda i: (i, 0)
        elif mode == "last_t":
            blk, rows, imap = (wd, tm), (n // tm // tiles_per_seq) * wd, lambda i: (i // tiles_per_seq, 0)
        else:
            blk, rows, imap = (tm, wd), n, lambda i: (i, 0)
        specs.append(pl.BlockSpec(blk, imap))
        shapes.append(jax.ShapeDtypeStruct((rows, blk[1]), dt))
    return pl.pallas_call(
        kern,
        grid=(n // tm,),
        in_specs=[
            pl.BlockSpec((tm, d), lambda i: (i, 0)),
            _const_spec((1, d)),
            _const_spec(w.shape),
        ],
        out_specs=specs,
        out_shape=shapes,
        compiler_params=_params(("arbitrary",)),
        name="norm_proj",
    )(x, g, w)


def _band_kernel(*refs, tq, seq, use_hist):
    if use_hist:
        q_ref, k_ref, v_ref, hk_ref, hv_ref, bias_ref, o_ref, kp, vp, s_scr, qt_scr = refs
    else:
        q_ref, k_ref, v_ref, bias_ref, o_ref, kp, vp, s_scr, qt_scr = refs
    win = WIN_A + tq
    nblk = seq // tq
    npair = H_A // 2

    if use_hist:
        kp[0:WIN_A, :] = hk_ref[0].T.astype(BF16)
        vp[0:WIN_A, :] = hv_ref[0].T.astype(BF16)
    else:
        kp[0:WIN_A, :] = jnp.zeros((WIN_A, kp.shape[1]), BF16)
        vp[0:WIN_A, :] = jnp.zeros((WIN_A, vp.shape[1]), BF16)
    kp[WIN_A:WIN_A + seq, :] = k_ref[0].astype(BF16)
    vp[WIN_A:WIN_A + seq, :] = v_ref[0].astype(BF16)

    feat = lax.broadcasted_iota(jnp.int32, (LANES, tq), 0)
    lo = feat < DH_A
    slabs = [slice(hp * LANES, (hp + 1) * LANES) for hp in range(npair)]

    for blk in range(nblk):
        for hp in range(npair):
            qt_scr[hp, blk] = _split_maps_t(q_ref[0, blk * tq:(blk + 1) * tq, slabs[hp]])

    def stage_a(m, hp, mask_history, slot):
        off = m * tq if isinstance(m, int) else pl.multiple_of(m * tq, tq)
        s = _dot(kp[pl.ds(off, win), slabs[hp]], qt_scr[hp, m]) + bias_ref[hp]
        if mask_history:
            key = lax.broadcasted_iota(jnp.int32, (win, 2 * tq), 0)
            s = jnp.where(key >= (WIN_A - m * tq), s, NEG_INF)
        s_scr[slot, hp] = s
        return jnp.max(s, axis=0, keepdims=True)

    def stage_b(m, hp, mx, slot):
        off = pl.multiple_of(m * tq, tq)
        p = jnp.exp2(s_scr[slot, hp] - mx)
        l = jnp.sum(p, axis=0, keepdims=True)
        o = _dot_tn(vp[pl.ds(off, win), slabs[hp]], p.astype(BF16)) / l
        o_ref[0, pl.ds(off, tq), slabs[hp]] = jnp.where(lo, o[:, :tq], o[:, tq:]).T.astype(o_ref.dtype)

    nmask = 0 if use_hist else min(WIN_A // tq, nblk)
    wide = s_scr.shape[0]
    assert nblk % wide == 0 and nmask % wide == 0

    def group(mask_history, g, mxs):
        blocks = [g * wide + i for i in range(wide)]
        for hp in range(1, npair):
            nxt = []
            for i in range(wide):
                nxt.append(stage_a(blocks[i], hp, mask_history, i))
                stage_b(blocks[i], hp - 1, mxs[i], i)
            mxs = nxt
        nxt = []
        for i in range(wide):
            nxt.append(stage_a(jnp.minimum(blocks[i] + wide, nblk - wide + i), 0, mask_history, i))
            stage_b(blocks[i], npair - 1, mxs[i], i)
        return tuple(nxt)

    mxs = tuple(stage_a(i, 0, nmask > 0, i) for i in range(wide))
    unroll = lambda trips: 2 if trips % 2 == 0 else 1
    mxs = lax.fori_loop(0, nmask // wide, functools.partial(group, True), mxs, unroll=unroll(nmask // wide))
    lax.fori_loop(nmask // wide, nblk // wide, functools.partial(group, False), mxs,
                  unroll=unroll((nblk - nmask) // wide))


def _band_blocks_in_flight(nblk):
    return 2 if nblk % 2 == 0 else 1


def _band_attn(q, k, v, bias, tq, hist=None):
    b, seq, da = q.shape
    win = WIN_A + tq
    use_hist = hist is not None
    kern = functools.partial(_band_kernel, tq=tq, seq=seq, use_hist=use_hist)
    full = pl.BlockSpec((1, seq, da), lambda i: (i, 0, 0))
    in_specs = [full, full, full]
    args = [q, k, v]
    if use_hist:
        hspec = pl.BlockSpec((1, da, WIN_A), lambda i: (i, 0, 0))
        in_specs += [hspec, hspec]
        args += list(hist)
    in_specs.append(_const_spec(bias.shape))
    args.append(bias)
    return pl.pallas_call(
        kern,
        grid=(b,),
        in_specs=in_specs,
        out_specs=full,
        out_shape=jax.ShapeDtypeStruct((b, seq, da), BF16),
        scratch_shapes=[pltpu.VMEM((WIN_A + seq, da), BF16), pltpu.VMEM((WIN_A + seq, da), BF16),
                        pltpu.VMEM((_band_blocks_in_flight(seq // tq), H_A // 2, win, 2 * tq), F32),
                        pltpu.VMEM((H_A // 2, seq // tq, LANES, 2 * tq), BF16)],
        compiler_params=_params(("parallel",)),
        name="band_attn",
    )(*args)


BIAS_LANE0 = WIN_A + REL_CLIP
BIAS_WIDTH = 768


def _band_bias_kernel(rb_ref, o_ref, *, tq, win):
    z = lax.broadcasted_iota(jnp.int32, (1, BIAS_WIDTH), 1)
    j = lax.broadcasted_iota(jnp.int32, (win, tq), 0)
    r = lax.broadcasted_iota(jnp.int32, (win, tq), 1)
    kc = j // CHUNK
    qc = r // CHUNK
    band = (kc >= qc) & (kc <= qc + BAND_CHUNKS)
    for h in range(H_A):
        row = jnp.where(z >= 2 * REL_CLIP, rb_ref[h:h + 1, 2 * REL_CLIP:2 * REL_CLIP + 1], rb_ref[h:h + 1, :])
        skew = pltpu.roll(jnp.broadcast_to(row, (win, BIAS_WIDTH)), 0, axis=1, stride=1, stride_axis=0)
        tbl = jnp.where(band, skew[:, BIAS_LANE0:BIAS_LANE0 + tq] * LOG2E, NEG_INF)
        o_ref[h // 2, :, (h % 2) * tq:(h % 2 + 1) * tq] = tbl


def _band_bias_table(rel_bias, tq):
    win = WIN_A + tq
    assert BIAS_LANE0 + tq <= BIAS_WIDTH and win <= BIAS_WIDTH
    rb = jnp.pad(rel_bias.astype(F32), ((0, 0), (0, BIAS_WIDTH - rel_bias.shape[1])))
    return pl.pallas_call(
        functools.partial(_band_bias_kernel, tq=tq, win=win),
        out_shape=jax.ShapeDtypeStruct((H_A // 2, win, 2 * tq), F32),
        compiler_params=pltpu.CompilerParams(vmem_limit_bytes=VMEM_LIMIT),
        name="band_bias",
    )(rb)


def _split_maps_t(q):
    qt = q.astype(F32).T
    feat = lax.broadcasted_iota(jnp.int32, qt.shape, 0)
    lo = feat < DH_B
    return jnp.concatenate([jnp.where(lo, qt, 0.0), jnp.where(lo, 0.0, qt)], axis=1).astype(BF16)


def _key_query_diff(tk, tq):
    c = lax.broadcasted_iota(jnp.int32, (tk, 2 * tq), 0)
    r = lax.broadcasted_iota(jnp.int32, (tk, 2 * tq), 1)
    rr = jnp.where(r >= tq, r - tq, r)
    return rr, c


def _alibi_slope(h):
    return 2.0 ** (-8.0 * (h + 1) / H_B) * LOG2E


def _online_update(s, vb, shift, carry):
    m, l, acc = carry
    m_new = jnp.maximum(m, jnp.max(s, axis=0, keepdims=True) + shift)
    alpha = jnp.exp2(m - m_new)
    p = jnp.exp2(s - (m_new - shift))
    l = alpha * l + jnp.sum(p, axis=0, keepdims=True)
    acc = alpha * acc + _dot_tn(vb, p.astype(BF16))
    return m_new, l, acc


def _diff_finish_t(carry, tq, lam, gsub, lam_init):
    _, l, acc = carry
    o = acc / l
    od = (o[:, :tq] - lam * o[:, tq:]).T
    return _rms(od, gsub) * (1.0 - lam_init)


def _diff_lambda(lq_ref, lk_ref, lam_init):
    e = jnp.exp(jnp.sum(lq_ref[...] * lk_ref[...], axis=-1, keepdims=True))
    return e[0:1] - e[1:2] + lam_init


def _diff_init_t(tq):
    return (jnp.full((1, 2 * tq), NEG_INF, F32), jnp.zeros((1, 2 * tq), F32),
            jnp.zeros((LANES, 2 * tq), F32))


def _diff_prompt_kernel(q_ref, k_ref, v_ref, lq_ref, lk_ref, gsub_ref, o_ref,
                        bias_scr, qq_scr, s0_scr, s1_scr, acc_scr, *, tq, lam_init):
    sp = pl.program_id(1)
    slabs = [slice(h * LANES, (h + 1) * LANES) for h in range(H_B)]
    chains = [(e, bb, h) for e in range(2) for bb in range(q_ref.shape[0]) for h in range(H_B)]
    every = list(range(len(chains)))
    even = [c for c in every if chains[c][0] == 0]
    odd = [c for c in every if chains[c][0] == 1]

    @pl.when(sp == 0)
    def _():
        rr, c = _key_query_diff(tq, tq)
        d = (rr - c).astype(F32)
        allowed = (c // CHUNK) <= (rr // CHUNK)
        for h in range(H_B):
            bias_scr[0, h] = -_alibi_slope(h) * d
            bias_scr[1, h] = jnp.where(allowed, -_alibi_slope(h) * jnp.abs(d), NEG_INF)

    for c, (e, bb, h) in enumerate(chains):
        qq_scr[c] = _split_maps_t(q_ref[bb, e * tq:(e + 1) * tq, slabs[h]])
    acc_scr[...] = jnp.zeros(acc_scr.shape, F32)
    bufs = (s0_scr, s1_scr)

    def chain_a(j, buf, c, m):
        e, bb, h = chains[c]
        qi = 2 * sp + e
        off = j * tq if isinstance(j, int) else pl.multiple_of(j * tq, tq)
        diag = (j == qi).astype(jnp.int32)
        s = _dot(k_ref[bb, pl.ds(off, tq), slabs[h]], qq_scr[c]) + bias_scr[diag, h]
        bufs[buf][c] = s
        shift = -_alibi_slope(h) * ((qi - j) * tq).astype(F32)
        m_new = jnp.maximum(m, jnp.max(s, axis=0, keepdims=True) + shift)
        return m_new, jnp.exp2(m - m_new), m_new - shift

    def chain_b(j, buf, c, l, alpha, sub):
        e, bb, h = chains[c]
        off = j * tq if isinstance(j, int) else pl.multiple_of(j * tq, tq)
        p = jnp.exp2(bufs[buf][c] - sub)
        acc_scr[c] = alpha * acc_scr[c] + _dot_tn(v_ref[bb, pl.ds(off, tq), slabs[h]], p.astype(BF16))
        return alpha * l + jnp.sum(p, axis=0, keepdims=True)

    def stages(a_spec, b_spec, state):
        ms, ls, pend = [list(x) for x in state[:2]] + [[[list(x) for x in p] for p in state[2]]]
        for c in every:
            if a_spec is not None and c in a_spec[2]:
                ms[c], pend[a_spec[1]][0][c], pend[a_spec[1]][1][c] = chain_a(a_spec[0], a_spec[1], c, ms[c])
            if b_spec is not None and c in b_spec[2]:
                ls[c] = chain_b(b_spec[0], b_spec[1], c, ls[c], pend[b_spec[1]][0][c], pend[b_spec[1]][1][c])
        return tuple(ms), tuple(ls), tuple((tuple(p[0]), tuple(p[1])) for p in pend)

    def finish(which, ls):
        lam = _diff_lambda(lq_ref, lk_ref, lam_init)
        for c in which:
            e, bb, h = chains[c]
            o_ref[bb, e * tq:(e + 1) * tq, slabs[h]] = _diff_finish_t(
                (None, ls[c], acc_scr[c]), tq, lam, gsub_ref[...], lam_init).astype(o_ref.dtype)

    row = lambda val: tuple(jnp.full((1, 2 * tq), val, F32) for _ in chains)
    state = (row(NEG_INF), row(0.0), ((row(0.0), row(0.0)), (row(0.0), row(0.0))))
    state = stages((0, 0, every), None, state)

    def two_blocks(jj, state):
        j = 2 * jj
        state = stages((j + 1, 1, every), (j, 0, every), state)
        return stages((j + 2, 0, every), (j + 1, 1, every), state)

    state = lax.fori_loop(0, sp, two_blocks, state)
    state = stages((2 * sp + 1, 1, odd), (2 * sp, 0, every), state)
    finish(even, state[1])
    state = stages(None, (2 * sp + 1, 1, odd), state)
    finish(odd, state[1])


def _diff_attn_prompt(q, k, v, lam_q, lam_k, g_sub, tq, lam_init):
    b, seq, db = q.shape
    nbat = 2 if b % 2 == 0 else 1
    kern = functools.partial(_diff_prompt_kernel, tq=tq, lam_init=lam_init)
    assert seq % (2 * tq) == 0
    nchain = 2 * nbat * H_B
    kv_spec = pl.BlockSpec((nbat, seq, db), lambda i, j: (i, 0, 0))
    q_spec = pl.BlockSpec((nbat, 2 * tq, db), lambda i, j: (i, j, 0))
    return pl.pallas_call(
        kern,
        grid=(b // nbat, seq // (2 * tq)),
        in_specs=[q_spec, kv_spec, kv_spec,
                  _const_spec(lam_q.shape), _const_spec(lam_k.shape), _const_spec(g_sub.shape)],
        out_specs=q_spec,
        out_shape=jax.ShapeDtypeStruct((b, seq, db), BF16),
        scratch_shapes=[pltpu.VMEM((2, H_B, tq, 2 * tq), F32), pltpu.VMEM((nchain, LANES, 2 * tq), BF16),
                        pltpu.VMEM((nchain, tq, 2 * tq), F32), pltpu.VMEM((nchain, tq, 2 * tq), F32),
                        pltpu.VMEM((nchain, LANES, 2 * tq), F32)],
        compiler_params=_params(("parallel", "arbitrary")),
        name="diff_attn_prompt",
    )(q, k, v, lam_q, lam_k, g_sub)


def _diff_sample_kernel(q_ref, kc_ref, vc_ref, kn_ref, vn_ref, lq_ref, lk_ref, gsub_ref, o_ref,
                        m_s, l_s, acc_s, s_scr, *, t, past, tk, ts, lam_init):
    ci = pl.program_id(1)

    @pl.when(ci == 0)
    def _():
        for h in range(H_B):
            m_s[h], l_s[h], acc_s[h] = _diff_init_t(t)

    slabs = [slice(h * LANES, (h + 1) * LANES) for h in range(H_B)]
    qqts = [_split_maps_t(q_ref[0, :, cs]) for cs in slabs]
    rr, c = _key_query_diff(ts, t)
    d = (rr - c).astype(F32)
    rows = lambda ref, h, n, first=0: _head_rows(ref, h, n, H_B, lead=(0,), first=first)[...].astype(BF16)
    nstream = s_scr.shape[0]
    per = H_B // nstream
    items = [[(sub, h) for sub in range(tk // ts) for h in range(st * per, (st + 1) * per)]
             for st in range(nstream)]
    nitem = len(items[0])

    def stage_a(st, i):
        sub, h = items[st][i]
        s = _dot(rows(kc_ref, h, ts, sub * ts), qqts[h]) - _alibi_slope(h) * d
        s_scr[st, i % 2] = s
        shift = -_alibi_slope(h) * (past - ci * tk - sub * ts).astype(F32)
        m_new = jnp.maximum(m_s[h], jnp.max(s, axis=0, keepdims=True) + shift)
        alpha = jnp.exp2(m_s[h] - m_new)
        m_s[h] = m_new
        return alpha, m_new - shift

    def stage_b(st, i, alpha, ref_max):
        sub, h = items[st][i]
        p = jnp.exp2(s_scr[st, i % 2] - ref_max)
        l_s[h] = alpha * l_s[h] + jnp.sum(p, axis=0, keepdims=True)
        acc_s[h] = alpha * acc_s[h] + _dot_tn(rows(vc_ref, h, ts, sub * ts), p.astype(BF16))

    pend = [stage_a(st, 0) for st in range(nstream)]
    for i in range(1, nitem):
        nxt = [stage_a(st, i) for st in range(nstream)]
        for st in range(nstream):
            stage_b(st, i - 1, *pend[st])
        pend = nxt
    for st in range(nstream):
        stage_b(st, nitem - 1, *pend[st])

    @pl.when(ci == past // tk - 1)
    def _():
        rn, cn = _key_query_diff(t, t)
        dn = jnp.abs((rn - cn).astype(F32))
        lam = _diff_lambda(lq_ref, lk_ref, lam_init)
        for h, cs in enumerate(slabs):
            s = _dot(rows(kn_ref, h, t), qqts[h]) - _alibi_slope(h) * dn
            carry = _online_update(s, rows(vn_ref, h, t), 0.0, (m_s[h], l_s[h], acc_s[h]))
            o_ref[0, :, cs] = _diff_finish_t(carry, t, lam, gsub_ref[...], lam_init).astype(o_ref.dtype)


def _diff_attn_sample(q, kc, vc, kn, vn, lam_q, lam_k, g_sub, tk, ts, lam_init):
    b, t, db = q.shape
    past = kc.shape[1] // H_B
    kern = functools.partial(_diff_sample_kernel, t=t, past=past, tk=tk, ts=ts, lam_init=lam_init)
    new_spec = pl.BlockSpec((1, t, db), lambda i, j: (i, 0, 0))
    new_kv_spec = pl.BlockSpec((1, t * H_B, LANES), lambda i, j: (i, 0, 0))
    cache_spec = pl.BlockSpec((1, tk * H_B, LANES), lambda i, j: (i, j, 0))
    return pl.pallas_call(
        kern,
        grid=(b, past // tk),
        in_specs=[new_spec, cache_spec, cache_spec, new_kv_spec, new_kv_spec,
                  _const_spec(lam_q.shape), _const_spec(lam_k.shape), _const_spec(g_sub.shape)],
        out_specs=new_spec,
        out_shape=jax.ShapeDtypeStruct((b, t, db), BF16),
        scratch_shapes=[pltpu.VMEM((H_B, 1, 2 * t), F32), pltpu.VMEM((H_B, 1, 2 * t), F32),
                        pltpu.VMEM((H_B, LANES, 2 * t), F32), pltpu.VMEM((2, 2, ts, 2 * t), F32)],
        compiler_params=_params(("parallel", "arbitrary")),
        name="diff_attn_sample",
    )(q, kc, vc, kn, vn, lam_q, lam_k, g_sub)


def _mem_kernel(q_ref, mk_ref, mv_ref, o_ref):
    slabs = [slice(h * LANES, (h + 1) * LANES) for h in range(H_M)]
    nm = mk_ref.shape[1] // H_M
    rows = lambda ref, h: _head_rows(ref, h, nm, H_M, lead=(0,))[...].astype(BF16)
    scores = [_dot_nt(q_ref[0, :, cs], rows(mk_ref, h)) for h, cs in enumerate(slabs)]
    for h, cs in enumerate(slabs):
        mx = jnp.max(scores[h], axis=-1, keepdims=True)
        p = jnp.exp2(scores[h] - mx)
        l = jnp.sum(p, axis=-1, keepdims=True)
        o_ref[0, :, cs] = (_dot(p.astype(BF16), rows(mv_ref, h)) / l).astype(o_ref.dtype)


def _mem_attn(q, mk, mv, tq):
    b, seq, dm = q.shape
    kv_spec = pl.BlockSpec((1,) + mk.shape[1:], lambda i, j: (i, 0, 0))
    q_spec = pl.BlockSpec((1, tq, dm), lambda i, j: (i, j, 0))
    return pl.pallas_call(
        _mem_kernel,
        grid=(b, seq // tq),
        in_specs=[q_spec, kv_spec, kv_spec],
        out_specs=q_spec,
        out_shape=jax.ShapeDtypeStruct((b, seq, dm), BF16),
        compiler_params=_params(("parallel", "parallel")),
        name="mem_attn",
    )(q, mk, mv)


def _merge_kernel(x_ref, oa_ref, ob_ref, om_ref, g_ref, win_ref, bg_ref, wo_ref, wout_ref, o_ref, m_scr,
                  *, tn):
    x = x_ref[...]
    h = _rms(x, g_ref[...]).astype(BF16)
    o_refs = (oa_ref, ob_ref, om_ref)
    d = x.shape[1]
    for c0 in range(0, d, tn):
        cs = slice(c0, c0 + tn)
        m = None
        for br in range(N_BRANCH):
            g0 = br * d + c0
            goff = win_ref.shape[1] - N_BRANCH * d
            gate = jax.nn.sigmoid(_dot(h, win_ref[:, goff + g0:goff + g0 + tn]) + bg_ref[:, g0:g0 + tn])
            term = gate * _dot(o_refs[br][...], wo_ref[br, :, cs])
            m = term if m is None else m + term
        m_scr[:, cs] = m.astype(BF16)
    o_ref[...] = x + _dot(m_scr[...], wout_ref[...])


def _merge(x, oa, ob, om, g, wg, bg, wo, wout, tm, tn=256):
    n, d = x.shape
    kern = functools.partial(_merge_kernel, tn=tn)
    tok = lambda wd: pl.BlockSpec((tm, wd), lambda i: (i, 0))
    return pl.pallas_call(
        kern,
        grid=(n // tm,),
        in_specs=[tok(d), tok(oa.shape[1]), tok(ob.shape[1]), tok(om.shape[1]), _const_spec(g.shape),
                  _const_spec(wg.shape), _const_spec(bg.shape), _const_spec(wo.shape),
                  _const_spec(wout.shape)],
        out_specs=tok(d),
        out_shape=jax.ShapeDtypeStruct((n, d), F32),
        scratch_shapes=[pltpu.VMEM((tm, d), BF16)],
        compiler_params=_params(("parallel",)),
        name="merge",
    )(x, oa, ob, om, g, wg, bg, wo, wout)


def _ffn_kernel(x_ref, g_ref, prev_ref, wup_ref, wconv_ref, bconv_ref, wdown_ref, gfin_ref,
                y_ref, conv_ref, carry, u_scr, *, tm, nch, nseq, final_norm):
    if nseq == 1:
        @pl.when(pl.program_id(1) == 0)
        def _():
            carry[...] = prev_ref[0, 0]

    x = x_ref[0]
    h = _rms(x, g_ref[...]).astype(BF16)
    row = lax.broadcasted_iota(jnp.int32, (tm, FF_CHUNK), 0)
    seq_rows = tm // nseq
    ff = nch * FF_CHUNK
    for ch in range(nch):
        cs = slice(ch * FF_CHUNK, (ch + 1) * FF_CHUNK)
        a = _dot(h, wup_ref[:, cs])
        gate = _dot(h, wup_ref[:, ff + ch * FF_CHUNK:ff + (ch + 1) * FF_CHUNK])
        a1 = pltpu.roll(a, 1, axis=0)
        a2 = pltpu.roll(a, 2, axis=0)
        for k in range(nseq):
            before = carry if nseq == 1 else prev_ref.at[0, k]
            p2 = before[0:1, cs]
            p1 = before[1:2, cs]
            a1 = jnp.where(row == k * seq_rows, p1, a1)
            a2 = jnp.where(row == k * seq_rows, p2, jnp.where(row == k * seq_rows + 1, p1, a2))
        conv = bconv_ref[:, cs] + a2 * wconv_ref[0:1, cs]
        conv = conv + a1 * wconv_ref[1:2, cs]
        conv = conv + a * wconv_ref[2:3, cs]
        if nseq == 1:
            carry[:, cs] = a[tm - (CONV_W - 1):, :]
        else:
            for k in range(nseq):
                conv_ref[0, k, :, cs] = a[(k + 1) * seq_rows - (CONV_W - 1):(k + 1) * seq_rows, :]
        u_scr[:, cs] = (jax.nn.gelu(conv) * gate).astype(BF16)
    y = x + _dot(u_scr[...], wdown_ref[...])
    if final_norm:
        y = _rms(y, gfin_ref[...])
    y_ref[0] = y
    if nseq == 1:
        conv_ref[0, 0] = carry[...]


def _conv_ffn(x, g, prev, wup, wconv, bconv, wdown, gfin, tm, final_norm):
    b, seq, d = x.shape
    ff = wdown.shape[0]
    nch = ff // FF_CHUNK
    nseq = max(tm // seq, 1)
    groups = b // nseq
    assert (seq % tm == 0) if nseq == 1 else (tm % seq == 0 and b % nseq == 0)
    kern = functools.partial(_ffn_kernel, tm=tm, nch=nch, nseq=nseq, final_norm=final_norm)
    tok = pl.BlockSpec((1, tm, d), lambda i, j: (i, j, 0))
    state = pl.BlockSpec((1, nseq, CONV_W - 1, ff), lambda i, j: (i, 0, 0, 0))
    y, conv = pl.pallas_call(
        kern,
        grid=(groups, nseq * seq // tm),
        in_specs=[tok, _const_spec(g.shape), state, _const_spec(wup.shape), _const_spec(wconv.shape),
                  _const_spec(bconv.shape), _const_spec(wdown.shape), _const_spec(gfin.shape)],
        out_specs=[tok, state],
        out_shape=[jax.ShapeDtypeStruct((groups, nseq * seq, d), F32),
                   jax.ShapeDtypeStruct((groups, nseq, CONV_W - 1, ff), F32)],
        scratch_shapes=[pltpu.VMEM((CONV_W - 1, ff), F32), pltpu.VMEM((tm, ff), BF16)],
        compiler_params=_params(("parallel", "arbitrary")),
        name="conv_ffn",
    )(x.reshape(groups, nseq * seq, d), g, prev.reshape(groups, nseq, CONV_W - 1, ff),
      wup, wconv, bconv, wdown, gfin)
    return y.reshape(b, seq, d), conv.reshape(b, CONV_W - 1, ff)


def _token_tile(n, pref):
    return pref if n % pref == 0 else n


def _tiles(b, t, prompt):
    n = b * t
    if t % 1024 == 0:
        ffn = 1024
    elif t % 512 == 0 or (512 % t == 0 and n % 512 == 0):
        ffn = 512
    else:
        ffn = t
    return dict(
        proj=_token_tile(n, WIN_A),
        merge=_token_tile(n, 1024),
        ffn=ffn,
        band_q=128 if prompt else t,
        diff_q=256,
        mem_q=_token_tile(t, 1024),
        cache_keys=4096,
        cache_item=512,
    )


def _layer(x, w, *, lam_init, final_norm, conv_prev, mem=None, cache=None):
    b, t, d = x.shape
    n = b * t
    tile = _tiles(b, t, cache is None)
    tm = tile["proj"]
    d_a, d_b, d_m = H_A * DH_A, H_B * 2 * DH_B, H_M * DH_M
    prompt = cache is None
    assert not prompt or (tm == WIN_A and t % tm == 0)
    kv_a = [(BF16, "rows"), (F32, "last_t")] if prompt else [(F32, "rows")]
    kv_b = [(BF16, "rows"), (F32, "by_head")] if prompt else [(F32, "by_head")]
    outs = ([(0, d_a, DH_A ** -0.5 * LOG2E, BF16, "rows")]
            + [(d_a, d_a, 1.0, dt, mode) for dt, mode in kv_a]
            + [(2 * d_a, d_a, 1.0, dt, mode) for dt, mode in kv_a]
            + [(3 * d_a, d_b, DH_B ** -0.5 * LOG2E, BF16, "rows")]
            + [(3 * d_a + d_b, d_b, 1.0, dt, mode) for dt, mode in kv_b]
            + [(3 * d_a + 2 * d_b, d_b, 1.0, dt, mode) for dt, mode in kv_b]
            + [(3 * d_a + 3 * d_b, d_m, DH_M ** -0.5 * LOG2E, BF16, "rows")])
    res = _norm_proj(x.reshape(n, d), w["g_mix"], w["w_in"], outs, tm, tiles_per_seq=max(t // tm, 1))
    res = [a.reshape(b, -1, a.shape[-1]) for a in res]
    if prompt:
        qa, ka, ka_keep, va, va_keep, qb, kb_bf, kb, vb_bf, vb, qm = res
    else:
        qa, ka, va, qb, kb, vb, qm = res
        ka_keep, va_keep = ka, va

    if cache is None:
        oa = _band_attn(qa, ka, va, _band_bias_table(w["rel_bias"], tile["band_q"]), tile["band_q"])
        ob = _diff_attn_prompt(qb, kb_bf, vb_bf, w["lam_q"], w["lam_k"], w["g_sub"], tile["diff_q"], lam_init)
        nm = mem.shape[1]
        mk, mv = _norm_proj(mem.reshape(b * nm, d), w["g_mem"], w["w_mem_kv"],
                            [(0, d_m, 1.0, F32, "by_head"), (d_m, d_m, 1.0, F32, "by_head")],
                            _token_tile(b * nm, WIN_A))
        mk = mk.reshape(b, nm * H_M, DH_M)
        mv = mv.reshape(b, nm * H_M, DH_M)
        om = _mem_attn(qm, mk, mv, tile["mem_q"])
    else:
        oa = _band_attn(qa, ka, va, _band_bias_table(w["rel_bias"], tile["band_q"]), tile["band_q"],
                        hist=(cache["a_k_t"], cache["a_v_t"]))
        ob = _diff_attn_sample(qb, cache["b_k"], cache["b_v"], kb, vb,
                               w["lam_q"], w["lam_k"], w["g_sub"], tile["cache_keys"], tile["cache_item"], lam_init)
        mk, mv = cache["m_k"], cache["m_v"]
        om = _mem_attn(qm, mk, mv, tile["mem_q"])

    flat = lambda a: a.reshape(n, a.shape[-1])
    x1 = _merge(x.reshape(n, d), flat(oa), flat(ob), flat(om), w["g_mix"], w["w_in"], w["b_gate"],
                w["w_o"], w["w_out"], tile["merge"])
    y, conv = _conv_ffn(x1.reshape(b, t, d), w["g_ffn"], conv_prev, w["w_up"], w["w_conv"], w["b_conv"],
                        w["w_down"], w["g_final"], tile["ffn"], final_norm)
    return y, dict(ka=ka_keep, va=va_keep, kb=kb, vb=vb, mk=mk, mv=mv, conv=conv)


def kernel(x_prompt, x_sample, mem_prompt, cache_a_k, cache_a_v, cache_b_k, cache_b_v, cache_mem_k, cache_mem_v, state_ffn_conv, g_mix, w_in, b_gate, rel_bias, lam_q, lam_k, g_sub, g_mem, w_mem_kv, w_oa, w_ob, w_om, w_out, g_ffn, w_up, w_conv, b_conv, w_down, g_final):
    depth, d, _ = w_in.shape
    b, s, _ = x_prompt.shape
    bd, t, _ = x_sample.shape
    past = cache_b_k.shape[2]
    a_buf = cache_a_k.shape[2]
    ff = w_down.shape[1]
    d_a, d_b, d_m = H_A * DH_A, H_B * 2 * DH_B, H_M * DH_M
    a_keep = min(WIN_A, s)
    assert a_buf == WIN_A and a_keep == WIN_A and t == CHUNK and past % CHUNK == 0 and ff % FF_CHUNK == 0
    nch = ff // FF_CHUNK

    xp, xs = x_prompt, x_sample
    outs = {k: [] for k in ("pa_k", "pa_v", "pb_k", "pb_v", "pm_k", "pm_v", "pconv",
                            "sa_k", "sa_v", "sb_k", "sb_v", "sconv")}
    for l in range(depth):
        lam_init = 0.8 - 0.6 * math.exp(-0.3 * l)
        w = dict(
            g_mix=g_mix[l][None], g_mem=g_mem[l][None], g_ffn=g_ffn[l][None], g_final=g_final[None],
            g_sub=g_sub[l][None], lam_q=lam_q[l], lam_k=lam_k[l], rel_bias=rel_bias[l],
            w_in=w_in[l].astype(BF16),
            b_gate=b_gate[l][None],
            w_mem_kv=w_mem_kv[l].astype(BF16),
            w_o=jnp.stack([w_oa[l], w_ob[l], w_om[l]]).astype(BF16),
            w_out=w_out[l].astype(BF16),
            w_up=w_up[l].astype(BF16),
            w_conv=w_conv[l], b_conv=b_conv[l][None],
            w_down=w_down[l].astype(BF16),
        )
        last = l == depth - 1
        xp, po = _layer(xp, w, lam_init=lam_init, final_norm=last,
                        conv_prev=jnp.zeros((b, CONV_W - 1, ff), F32), mem=mem_prompt)
        feat_major = lambda a: a.transpose(0, 2, 3, 1).reshape(bd, d_a, a_buf)
        cache = dict(a_k=cache_a_k[l].reshape(bd, a_buf, d_a), a_v=cache_a_v[l].reshape(bd, a_buf, d_a),
                     a_k_t=feat_major(cache_a_k[l]), a_v_t=feat_major(cache_a_v[l]),
                     b_k=cache_b_k[l].reshape(bd, past * H_B, 2 * DH_B),
                     b_v=cache_b_v[l].reshape(bd, past * H_B, 2 * DH_B),
                     m_k=cache_mem_k[l].reshape(bd, -1, DH_M), m_v=cache_mem_v[l].reshape(bd, -1, DH_M))
        xs, so = _layer(xs, w, lam_init=lam_init, final_norm=last, conv_prev=state_ffn_conv[l], cache=cache)

        tok_major = lambda a: a.reshape(b, H_A, DH_A, a_keep).transpose(0, 3, 1, 2)
        outs["pa_k"].append(tok_major(po["ka"]))
        outs["pa_v"].append(tok_major(po["va"]))
        outs["pb_k"].append(po["kb"].reshape(b, s, H_B, 2 * DH_B))
        outs["pb_v"].append(po["vb"].reshape(b, s, H_B, 2 * DH_B))
        outs["pm_k"].append(po["mk"].reshape(b, -1, H_M, DH_M))
        outs["pm_v"].append(po["mv"].reshape(b, -1, H_M, DH_M))
        outs["pconv"].append(po["conv"])
        new_k = jnp.concatenate([cache["a_k"], so["ka"]], axis=1)[:, t:]
        new_v = jnp.concatenate([cache["a_v"], so["va"]], axis=1)[:, t:]
        outs["sa_k"].append(new_k.reshape(bd, a_buf, H_A, DH_A))
        outs["sa_v"].append(new_v.reshape(bd, a_buf, H_A, DH_A))
        outs["sb_k"].append(so["kb"].reshape(bd, t, H_B, 2 * DH_B))
        outs["sb_v"].append(so["vb"].reshape(bd, t, H_B, 2 * DH_B))
        outs["sconv"].append(so["conv"])

    st = {k: jnp.stack(v) for k, v in outs.items()}
    return (xp, xs, st["pa_k"], st["pa_v"], st["pb_k"], st["pb_v"], st["pm_k"], st["pm_v"], st["pconv"],
            st["sa_k"], st["sa_v"], st["sb_k"], st["sb_v"], st["sconv"])
```

```python
import functools
import math

import jax
import jax.numpy as jnp
from jax import lax
from jax.experimental import pallas as pl
from jax.experimental.pallas import tpu as pltpu

CHUNK = 64
BAND_CHUNKS = 8
WIN_A = BAND_CHUNKS * CHUNK
REL_CLIP = 128
H_A = 8
DH_A = 64
H_B = 4
DH_B = 64
H_M = 4
DH_M = 128
N_BRANCH = 3
CONV_W = 3
EPS = 1e-6
NEG_INF = -1e30

LANES = 128
FF_CHUNK = 256
VMEM_LIMIT = 56 * 1024 * 1024

LOG2E = math.log2(math.e)

BF16 = jnp.bfloat16
F32 = jnp.float32


def _rms(x, g):
    y = x * lax.rsqrt(jnp.mean(x * x, axis=-1, keepdims=True) + EPS)
    return y * g


def _dot(a, b):
    return jnp.dot(a, b, preferred_element_type=F32)


def _dot_nt(a, b):
    return lax.dot_general(a, b, (((1,), (1,)), ((), ())), preferred_element_type=F32)


def _dot_tn(a, b):
    return lax.dot_general(a, b, (((0,), (0,)), ((), ())), preferred_element_type=F32)


def _const_spec(shape):
    nd = len(shape)
    return pl.BlockSpec(shape, lambda *_: (0,) * nd, pipeline_mode=pl.Buffered(1))


def _params(sem, flags=None):
    return pltpu.CompilerParams(dimension_semantics=sem, vmem_limit_bytes=VMEM_LIMIT, flags=flags)


def _head_rows(ref, h, n, heads, lead=(), first=0):
    return ref.at[lead + (pl.ds(first * heads + h, n, stride=heads), slice(None))]


def _norm_proj_kernel(x_ref, g_ref, w_ref, *o_refs, outs, tiles_per_seq):
    tm = x_ref.shape[0]
    last_tile = pl.program_id(0) % tiles_per_seq == tiles_per_seq - 1
    h = _rms(x_ref[...], g_ref[...]).astype(BF16)
    z_of = {}
    transposed = []
    for o_ref, (c0, wd, sc, _, mode) in zip(o_refs, outs):
        if (c0, wd) not in z_of:
            z_of[(c0, wd)] = _dot(h, w_ref[:, c0:c0 + wd])
        z = z_of[(c0, wd)]
        if sc != 1.0:
            z = z * sc
        if mode == "by_head":
            heads = wd // LANES
            for hd in range(heads):
                _head_rows(o_ref, hd, tm, heads)[...] = z[:, hd * LANES:(hd + 1) * LANES].astype(o_ref.dtype)
        elif mode == "last_t":
            transposed.append((o_ref, z))
        else:
            o_ref[...] = z.astype(o_ref.dtype)

    if transposed:
        @pl.when(last_tile)
        def _():
            for o_ref, z in transposed:
                o_ref[...] = z.T.astype(o_ref.dtype)


def _norm_proj(x, g, w, outs, tm, tiles_per_seq=1):
    n, d = x.shape
    kern = functools.partial(_norm_proj_kernel, outs=outs, tiles_per_seq=tiles_per_seq)
    specs, shapes = [], []
    for _, wd, _, dt, mode in outs:
        if mode == "by_head":
            blk, rows, imap = ((wd // LANES) * tm, LANES), (wd // LANES) * n, lambda i: (i, 0)
        elif mode == "last_t":
            blk, rows, imap = (wd, tm), (n // tm // tiles_per_seq) * wd, lambda i: (i // tiles_per_seq, 0)
        else:
            blk, rows, imap = (tm, wd), n, lambda i: (i, 0)
        specs.append(pl.BlockSpec(blk, imap))
        shapes.append(jax.ShapeDtypeStruct((rows, blk[1]), dt))
    return pl.pallas_call(
        kern,
        grid=(n // tm,),
        in_specs=[
            pl.BlockSpec((tm, d), lambda i: (i, 0)),
            _const_spec((1, d)),
            _const_spec(w.shape),
        ],
        out_specs=specs,
        out_shape=shapes,
        compiler_params=_params(("arbitrary",)),
        name="norm_proj",
    )(x, g, w)


def _band_kernel(*refs, tq, seq, use_hist):
    if use_hist:
        q_ref, k_ref, v_ref, hk_ref, hv_ref, bias_ref, o_ref, kp, vp, s_scr, qt_scr = refs
    else:
        q_ref, k_ref, v_ref, bias_ref, o_ref, kp, vp, s_scr, qt_scr = refs
    win = WIN_A + tq
    nblk = seq // tq
    npair = H_A // 2

    if use_hist:
        kp[0:WIN_A, :] = hk_ref[0].T.astype(BF16)
        vp[0:WIN_A, :] = hv_ref[0].T.astype(BF16)
    else:
        kp[0:WIN_A, :] = jnp.zeros((WIN_A, kp.shape[1]), BF16)
        vp[0:WIN_A, :] = jnp.zeros((WIN_A, vp.shape[1]), BF16)
    kp[WIN_A:WIN_A + seq, :] = k_ref[0].astype(BF16)
    vp[WIN_A:WIN_A + seq, :] = v_ref[0].astype(BF16)

    feat = lax.broadcasted_iota(jnp.int32, (LANES, tq), 0)
    lo = feat < DH_A
    slabs = [slice(hp * LANES, (hp + 1) * LANES) for hp in range(npair)]

    for blk in range(nblk):
        for hp in range(npair):
            qt_scr[hp, blk] = _split_maps_t(q_ref[0, blk * tq:(blk + 1) * tq, slabs[hp]])

    def stage_a(m, hp, mask_history, slot):
        off = m * tq if isinstance(m, int) else pl.multiple_of(m * tq, tq)
        s = _dot(kp[pl.ds(off, win), slabs[hp]], qt_scr[hp, m]) + bias_ref[hp]
        if mask_history:
            key = lax.broadcasted_iota(jnp.int32, (win, 2 * tq), 0)
            s = jnp.where(key >= (WIN_A - m * tq), s, NEG_INF)
        s_scr[slot, hp] = s
        return jnp.max(s, axis=0, keepdims=True)

    def stage_b(m, hp, mx, slot):
        off = pl.multiple_of(m * tq, tq)
        p = jnp.exp2(s_scr[slot, hp] - mx)
        l = jnp.sum(p, axis=0, keepdims=True)
        o = _dot_tn(vp[pl.ds(off, win), slabs[hp]], p.astype(BF16)) / l
        o_ref[0, pl.ds(off, tq), slabs[hp]] = jnp.where(lo, o[:, :tq], o[:, tq:]).T.astype(o_ref.dtype)

    nmask = 0 if use_hist else min(WIN_A // tq, nblk)
    wide = s_scr.shape[0]
    assert nblk % wide == 0 and nmask % wide == 0

    def group(mask_history, g, mxs):
        blocks = [g * wide + i for i in range(wide)]
        for hp in range(1, npair):
            nxt = []
            for i in range(wide):
                nxt.append(stage_a(blocks[i], hp, mask_history, i))
                stage_b(blocks[i], hp - 1, mxs[i], i)
            mxs = nxt
        nxt = []
        for i in range(wide):
            nxt.append(stage_a(jnp.minimum(blocks[i] + wide, nblk - wide + i), 0, mask_history, i))
            stage_b(blocks[i], npair - 1, mxs[i], i)
        return tuple(nxt)

    mxs = tuple(stage_a(i, 0, nmask > 0, i) for i in range(wide))
    unroll = lambda trips: 2 if trips % 2 == 0 else 1
    mxs = lax.fori_loop(0, nmask // wide, functools.partial(group, True), mxs, unroll=unroll(nmask // wide))
    lax.fori_loop(nmask // wide, nblk // wide, functools.partial(group, False), mxs,
                  unroll=unroll((nblk - nmask) // wide))


def _band_blocks_in_flight(nblk):
    return 2 if nblk % 2 == 0 else 1


def _band_attn(q, k, v, bias, tq, hist=None):
    b, seq, da = q.shape
    win = WIN_A + tq
    use_hist = hist is not None
    kern = functools.partial(_band_kernel, tq=tq, seq=seq, use_hist=use_hist)
    full = pl.BlockSpec((1, seq, da), lambda i: (i, 0, 0))
    in_specs = [full, full, full]
    args = [q, k, v]
    if use_hist:
        hspec = pl.BlockSpec((1, da, WIN_A), lambda i: (i, 0, 0))
        in_specs += [hspec, hspec]
        args += list(hist)
    in_specs.append(_const_spec(bias.shape))
    args.append(bias)
    return pl.pallas_call(
        kern,
        grid=(b,),
        in_specs=in_specs,
        out_specs=full,
        out_shape=jax.ShapeDtypeStruct((b, seq, da), BF16),
        scratch_shapes=[pltpu.VMEM((WIN_A + seq, da), BF16), pltpu.VMEM((WIN_A + seq, da), BF16),
                        pltpu.VMEM((_band_blocks_in_flight(seq // tq), H_A // 2, win, 2 * tq), F32),
                        pltpu.VMEM((H_A // 2, seq // tq, LANES, 2 * tq), BF16)],
        compiler_params=_params(("parallel",)),
        name="band_attn",
    )(*args)


BIAS_LANE0 = WIN_A + REL_CLIP
BIAS_WIDTH = 768


def _band_bias_kernel(rb_ref, o_ref, *, tq, win):
    z = lax.broadcasted_iota(jnp.int32, (1, BIAS_WIDTH), 1)
    j = lax.broadcasted_iota(jnp.int32, (win, tq), 0)
    r = lax.broadcasted_iota(jnp.int32, (win, tq), 1)
    kc = j // CHUNK
    qc = r // CHUNK
    band = (kc >= qc) & (kc <= qc + BAND_CHUNKS)
    for h in range(H_A):
        row = jnp.where(z >= 2 * REL_CLIP, rb_ref[h:h + 1, 2 * REL_CLIP:2 * REL_CLIP + 1], rb_ref[h:h + 1, :])
        skew = pltpu.roll(jnp.broadcast_to(row, (win, BIAS_WIDTH)), 0, axis=1, stride=1, stride_axis=0)
        tbl = jnp.where(band, skew[:, BIAS_LANE0:BIAS_LANE0 + tq] * LOG2E, NEG_INF)
        o_ref[h // 2, :, (h % 2) * tq:(h % 2 + 1) * tq] = tbl


def _band_bias_table(rel_bias, tq):
    win = WIN_A + tq
    assert BIAS_LANE0 + tq <= BIAS_WIDTH and win <= BIAS_WIDTH
    rb = jnp.pad(rel_bias.astype(F32), ((0, 0), (0, BIAS_WIDTH - rel_bias.shape[1])))
    return pl.pallas_call(
        functools.partial(_band_bias_kernel, tq=tq, win=win),
        out_shape=jax.ShapeDtypeStruct((H_A // 2, win, 2 * tq), F32),
        compiler_params=pltpu.CompilerParams(vmem_limit_bytes=VMEM_LIMIT),
        name="band_bias",
    )(rb)


def _split_maps_t(q):
    qt = q.astype(F32).T
    feat = lax.broadcasted_iota(jnp.int32, qt.shape, 0)
    lo = feat < DH_B
    return jnp.concatenate([jnp.where(lo, qt, 0.0), jnp.where(lo, 0.0, qt)], axis=1).astype(BF16)


def _key_query_diff(tk, tq):
    c = lax.broadcasted_iota(jnp.int32, (tk, 2 * tq), 0)
    r = lax.broadcasted_iota(jnp.int32, (tk, 2 * tq), 1)
    rr = jnp.where(r >= tq, r - tq, r)
    return rr, c


def _alibi_slope(h):
    return 2.0 ** (-8.0 * (h + 1) / H_B) * LOG2E


def _online_update(s, vb, shift, carry):
    m, l, acc = carry
    m_new = jnp.maximum(m, jnp.max(s, axis=0, keepdims=True) + shift)
    alpha = jnp.exp2(m - m_new)
    p = jnp.exp2(s - (m_new - shift))
    l = alpha * l + jnp.sum(p, axis=0, keepdims=True)
    acc = alpha * acc + _dot_tn(vb, p.astype(BF16))
    return m_new, l, acc


def _diff_finish_t(carry, tq, lam, gsub, lam_init):
    _, l, acc = carry
    o = acc / l
    od = (o[:, :tq] - lam * o[:, tq:]).T
    return _rms(od, gsub) * (1.0 - lam_init)


def _diff_lambda(lq_ref, lk_ref, lam_init):
    e = jnp.exp(jnp.sum(lq_ref[...] * lk_ref[...], axis=-1, keepdims=True))
    return e[0:1] - e[1:2] + lam_init


def _diff_init_t(tq):
    return (jnp.full((1, 2 * tq), NEG_INF, F32), jnp.zeros((1, 2 * tq), F32),
            jnp.zeros((LANES, 2 * tq), F32))


def _diff_prompt_kernel(q_ref, k_ref, v_ref, lq_ref, lk_ref, gsub_ref, o_ref,
                        bias_scr, qq_scr, s0_scr, s1_scr, acc_scr, *, tq, lam_init):
    sp = pl.program_id(1)
    slabs = [slice(h * LANES, (h + 1) * LANES) for h in range(H_B)]
    chains = [(e, bb, h) for e in range(2) for bb in range(q_ref.shape[0]) for h in range(H_B)]
    every = list(range(len(chains)))
    even = [c for c in every if chains[c][0] == 0]
    odd = [c for c in every if chains[c][0] == 1]

    @pl.when(sp == 0)
    def _():
        rr, c = _key_query_diff(tq, tq)
        d = (rr - c).astype(F32)
        allowed = (c // CHUNK) <= (rr // CHUNK)
        for h in range(H_B):
            bias_scr[0, h] = -_alibi_slope(h) * d
            bias_scr[1, h] = jnp.where(allowed, -_alibi_slope(h) * jnp.abs(d), NEG_INF)

    for c, (e, bb, h) in enumerate(chains):
        qq_scr[c] = _split_maps_t(q_ref[bb, e * tq:(e + 1) * tq, slabs[h]])
    acc_scr[...] = jnp.zeros(acc_scr.shape, F32)
    bufs = (s0_scr, s1_scr)

    def chain_a(j, buf, c, m):
        e, bb, h = chains[c]
        qi = 2 * sp + e
        off = j * tq if isinstance(j, int) else pl.multiple_of(j * tq, tq)
        diag = (j == qi).astype(jnp.int32)
        s = _dot(k_ref[bb, pl.ds(off, tq), slabs[h]], qq_scr[c]) + bias_scr[diag, h]
        bufs[buf][c] = s
        shift = -_alibi_slope(h) * ((qi - j) * tq).astype(F32)
        m_new = jnp.maximum(m, jnp.max(s, axis=0, keepdims=True) + shift)
        return m_new, jnp.exp2(m - m_new), m_new - shift

    def chain_b(j, buf, c, l, alpha, sub):
        e, bb, h = chains[c]
        off = j * tq if isinstance(j, int) else pl.multiple_of(j * tq, tq)
        p = jnp.exp2(bufs[buf][c] - sub)
        acc_scr[c] = alpha * acc_scr[c] + _dot_tn(v_ref[bb, pl.ds(off, tq), slabs[h]], p.astype(BF16))
        return alpha * l + jnp.sum(p, axis=0, keepdims=True)

    def stages(a_spec, b_spec, state):
        ms, ls, pend = [list(x) for x in state[:2]] + [[[list(x) for x in p] for p in state[2]]]
        for c in every:
            if a_spec is not None and c in a_spec[2]:
                ms[c], pend[a_spec[1]][0][c], pend[a_spec[1]][1][c] = chain_a(a_spec[0], a_spec[1], c, ms[c])
            if b_spec is not None and c in b_spec[2]:
                ls[c] = chain_b(b_spec[0], b_spec[1], c, ls[c], pend[b_spec[1]][0][c], pend[b_spec[1]][1][c])
        return tuple(ms), tuple(ls), tuple((tuple(p[0]), tuple(p[1])) for p in pend)

    def finish(which, ls):
        lam = _diff_lambda(lq_ref, lk_ref, lam_init)
        for c in which:
            e, bb, h = chains[c]
            o_ref[bb, e * tq:(e + 1) * tq, slabs[h]] = _diff_finish_t(
                (None, ls[c], acc_scr[c]), tq, lam, gsub_ref[...], lam_init).astype(o_ref.dtype)

    row = lambda val: tuple(jnp.full((1, 2 * tq), val, F32) for _ in chains)
    state = (row(NEG_INF), row(0.0), ((row(0.0), row(0.0)), (row(0.0), row(0.0))))
    state = stages((0, 0, every), None, state)

    def two_blocks(jj, state):
        j = 2 * jj
        state = stages((j + 1, 1, every), (j, 0, every), state)
        return stages((j + 2, 0, every), (j + 1, 1, every), state)

    state = lax.fori_loop(0, sp, two_blocks, state)
    state = stages((2 * sp + 1, 1, odd), (2 * sp, 0, every), state)
    finish(even, state[1])
    state = stages(None, (2 * sp + 1, 1, odd), state)
    finish(odd, state[1])


def _diff_attn_prompt(q, k, v, lam_q, lam_k, g_sub, tq, lam_init):
    b, seq, db = q.shape
    nbat = 2 if b % 2 == 0 else 1
    kern = functools.partial(_diff_prompt_kernel, tq=tq, lam_init=lam_init)
    assert seq % (2 * tq) == 0
    nchain = 2 * nbat * H_B
    kv_spec = pl.BlockSpec((nbat, seq, db), lambda i, j: (i, 0, 0))
    q_spec = pl.BlockSpec((nbat, 2 * tq, db), lambda i, j: (i, j, 0))
    return pl.pallas_call(
        kern,
        grid=(b // nbat, seq // (2 * tq)),
        in_specs=[q_spec, kv_spec, kv_spec,
                  _const_spec(lam_q.shape), _const_spec(lam_k.shape), _const_spec(g_sub.shape)],
        out_specs=q_spec,
        out_shape=jax.ShapeDtypeStruct((b, seq, db), BF16),
        scratch_shapes=[pltpu.VMEM((2, H_B, tq, 2 * tq), F32), pltpu.VMEM((nchain, LANES, 2 * tq), BF16),
                        pltpu.VMEM((nchain, tq, 2 * tq), F32), pltpu.VMEM((nchain, tq, 2 * tq), F32),
                        pltpu.VMEM((nchain, LANES, 2 * tq), F32)],
        compiler_params=_params(("parallel", "arbitrary")),
        name="diff_attn_prompt",
    )(q, k, v, lam_q, lam_k, g_sub)


def _diff_sample_kernel(q_ref, kc_ref, vc_ref, kn_ref, vn_ref, lq_ref, lk_ref, gsub_ref, o_ref,
                        m_s, l_s, acc_s, s_scr, *, t, past, tk, ts, lam_init):
    ci = pl.program_id(1)

    @pl.when(ci == 0)
    def _():
        for h in range(H_B):
            m_s[h], l_s[h], acc_s[h] = _diff_init_t(t)

    slabs = [slice(h * LANES, (h + 1) * LANES) for h in range(H_B)]
    qqts = [_split_maps_t(q_ref[0, :, cs]) for cs in slabs]
    rr, c = _key_query_diff(ts, t)
    d = (rr - c).astype(F32)
    rows = lambda ref, h, n, first=0: _head_rows(ref, h, n, H_B, lead=(0,), first=first)[...].astype(BF16)
    nstream = s_scr.shape[0]
    per = H_B // nstream
    items = [[(sub, h) for sub in range(tk // ts) for h in range(st * per, (st + 1) * per)]
             for st in range(nstream)]
    nitem = len(items[0])

    def stage_a(st, i):
        sub, h = items[st][i]
        s = _dot(rows(kc_ref, h, ts, sub * ts), qqts[h]) - _alibi_slope(h) * d
        s_scr[st, i % 2] = s
        shift = -_alibi_slope(h) * (past - ci * tk - sub * ts).astype(F32)
        m_new = jnp.maximum(m_s[h], jnp.max(s, axis=0, keepdims=True) + shift)
        alpha = jnp.exp2(m_s[h] - m_new)
        m_s[h] = m_new
        return alpha, m_new - shift

    def stage_b(st, i, alpha, ref_max):
        sub, h = items[st][i]
        p = jnp.exp2(s_scr[st, i % 2] - ref_max)
        l_s[h] = alpha * l_s[h] + jnp.sum(p, axis=0, keepdims=True)
        acc_s[h] = alpha * acc_s[h] + _dot_tn(rows(vc_ref, h, ts, sub * ts), p.astype(BF16))

    pend = [stage_a(st, 0) for st in range(nstream)]
    for i in range(1, nitem):
        nxt = [stage_a(st, i) for st in range(nstream)]
        for st in range(nstream):
            stage_b(st, i - 1, *pend[st])
        pend = nxt
    for st in range(nstream):
        stage_b(st, nitem - 1, *pend[st])

    @pl.when(ci == past // tk - 1)
    def _():
        rn, cn = _key_query_diff(t, t)
        dn = jnp.abs((rn - cn).astype(F32))
        lam = _diff_lambda(lq_ref, lk_ref, lam_init)
        for h, cs in enumerate(slabs):
            s = _dot(rows(kn_ref, h, t), qqts[h]) - _alibi_slope(h) * dn
            carry = _online_update(s, rows(vn_ref, h, t), 0.0, (m_s[h], l_s[h], acc_s[h]))
            o_ref[0, :, cs] = _diff_finish_t(carry, t, lam, gsub_ref[...], lam_init).astype(o_ref.dtype)


def _diff_attn_sample(q, kc, vc, kn, vn, lam_q, lam_k, g_sub, tk, ts, lam_init):
    b, t, db = q.shape
    past = kc.shape[1] // H_B
    tk = min(tk, past)
    assert past % tk == 0 and tk % ts == 0
    kern = functools.partial(_diff_sample_kernel, t=t, past=past, tk=tk, ts=ts, lam_init=lam_init)
    new_spec = pl.BlockSpec((1, t, db), lambda i, j: (i, 0, 0))
    new_kv_spec = pl.BlockSpec((1, t * H_B, LANES), lambda i, j: (i, 0, 0))
    cache_spec = pl.BlockSpec((1, tk * H_B, LANES), lambda i, j: (i, j, 0))
    return pl.pallas_call(
        kern,
        grid=(b, past // tk),
        in_specs=[new_spec, cache_spec, cache_spec, new_kv_spec, new_kv_spec,
                  _const_spec(lam_q.shape), _const_spec(lam_k.shape), _const_spec(g_sub.shape)],
        out_specs=new_spec,
        out_shape=jax.ShapeDtypeStruct((b, t, db), BF16),
        scratch_shapes=[pltpu.VMEM((H_B, 1, 2 * t), F32), pltpu.VMEM((H_B, 1, 2 * t), F32),
                        pltpu.VMEM((H_B, LANES, 2 * t), F32), pltpu.VMEM((2, 2, ts, 2 * t), F32)],
        compiler_params=_params(("parallel", "arbitrary")),
        name="diff_attn_sample",
    )(q, kc, vc, kn, vn, lam_q, lam_k, g_sub)


def _mem_kernel(q_ref, mk_ref, mv_ref, o_ref):
    slabs = [slice(h * LANES, (h + 1) * LANES) for h in range(H_M)]
    nm = mk_ref.shape[1] // H_M
    rows = lambda ref, h: _head_rows(ref, h, nm, H_M, lead=(0,))[...].astype(BF16)
    scores = [_dot_nt(q_ref[0, :, cs], rows(mk_ref, h)) for h, cs in enumerate(slabs)]
    for h, cs in enumerate(slabs):
        mx = jnp.max(scores[h], axis=-1, keepdims=True)
        p = jnp.exp2(scores[h] - mx)
        l = jnp.sum(p, axis=-1, keepdims=True)
        o_ref[0, :, cs] = (_dot(p.astype(BF16), rows(mv_ref, h)) / l).astype(o_ref.dtype)


def _mem_attn(q, mk, mv, tq):
    b, seq, dm = q.shape
    kv_spec = pl.BlockSpec((1,) + mk.shape[1:], lambda i, j: (i, 0, 0))
    q_spec = pl.BlockSpec((1, tq, dm), lambda i, j: (i, j, 0))
    return pl.pallas_call(
        _mem_kernel,
        grid=(b, seq // tq),
        in_specs=[q_spec, kv_spec, kv_spec],
        out_specs=q_spec,
        out_shape=jax.ShapeDtypeStruct((b, seq, dm), BF16),
        compiler_params=_params(("parallel", "parallel")),
        name="mem_attn",
    )(q, mk, mv)


def _merge_kernel(x_ref, oa_ref, ob_ref, om_ref, g_ref, win_ref, bg_ref, wo_ref, wout_ref, o_ref, m_scr,
                  *, tn):
    x = x_ref[...]
    h = _rms(x, g_ref[...]).astype(BF16)
    o_refs = (oa_ref, ob_ref, om_ref)
    d = x.shape[1]
    for c0 in range(0, d, tn):
        cs = slice(c0, c0 + tn)
        m = None
        for br in range(N_BRANCH):
            g0 = br * d + c0
            goff = win_ref.shape[1] - N_BRANCH * d
            gate = jax.nn.sigmoid(_dot(h, win_ref[:, goff + g0:goff + g0 + tn]) + bg_ref[:, g0:g0 + tn])
            term = gate * _dot(o_refs[br][...], wo_ref[br, :, cs])
            m = term if m is None else m + term
        m_scr[:, cs] = m.astype(BF16)
    o_ref[...] = x + _dot(m_scr[...], wout_ref[...])


def _merge(x, oa, ob, om, g, wg, bg, wo, wout, tm, tn=256):
    n, d = x.shape
    kern = functools.partial(_merge_kernel, tn=tn)
    tok = lambda wd: pl.BlockSpec((tm, wd), lambda i: (i, 0))
    return pl.pallas_call(
        kern,
        grid=(n // tm,),
        in_specs=[tok(d), tok(oa.shape[1]), tok(ob.shape[1]), tok(om.shape[1]), _const_spec(g.shape),
                  _const_spec(wg.shape), _const_spec(bg.shape), _const_spec(wo.shape),
                  _const_spec(wout.shape)],
        out_specs=tok(d),
        out_shape=jax.ShapeDtypeStruct((n, d), F32),
        scratch_shapes=[pltpu.VMEM((tm, d), BF16)],
        compiler_params=_params(("parallel",)),
        name="merge",
    )(x, oa, ob, om, g, wg, bg, wo, wout)


def _ffn_kernel(x_ref, g_ref, prev_ref, wup_ref, wconv_ref, bconv_ref, wdown_ref, gfin_ref,
                y_ref, conv_ref, carry, u_scr, *, tm, nch, nseq, final_norm):
    if nseq == 1:
        @pl.when(pl.program_id(1) == 0)
        def _():
            carry[...] = prev_ref[0, 0]

    x = x_ref[0]
    h = _rms(x, g_ref[...]).astype(BF16)
    row = lax.broadcasted_iota(jnp.int32, (tm, FF_CHUNK), 0)
    seq_rows = tm // nseq
    ff = nch * FF_CHUNK
    for ch in range(nch):
        cs = slice(ch * FF_CHUNK, (ch + 1) * FF_CHUNK)
        a = _dot(h, wup_ref[:, cs])
        gate = _dot(h, wup_ref[:, ff + ch * FF_CHUNK:ff + (ch + 1) * FF_CHUNK])
        a1 = pltpu.roll(a, 1, axis=0)
        a2 = pltpu.roll(a, 2, axis=0)
        for k in range(nseq):
            before = carry if nseq == 1 else prev_ref.at[0, k]
            p2 = before[0:1, cs]
            p1 = before[1:2, cs]
            a1 = jnp.where(row == k * seq_rows, p1, a1)
            a2 = jnp.where(row == k * seq_rows, p2, jnp.where(row == k * seq_rows + 1, p1, a2))
        conv = bconv_ref[:, cs] + a2 * wconv_ref[0:1, cs]
        conv = conv + a1 * wconv_ref[1:2, cs]
        conv = conv + a * wconv_ref[2:3, cs]
        if nseq == 1:
            carry[:, cs] = a[tm - (CONV_W - 1):, :]
        else:
            for k in range(nseq):
                conv_ref[0, k, :, cs] = a[(k + 1) * seq_rows - (CONV_W - 1):(k + 1) * seq_rows, :]
        u_scr[:, cs] = (jax.nn.gelu(conv) * gate).astype(BF16)
    y = x + _dot(u_scr[...], wdown_ref[...])
    if final_norm:
        y = _rms(y, gfin_ref[...])
    y_ref[0] = y
    if nseq == 1:
        conv_ref[0, 0] = carry[...]


def _conv_ffn(x, g, prev, wup, wconv, bconv, wdown, gfin, tm, final_norm):
    b, seq, d = x.shape
    ff = wdown.shape[0]
    nch = ff // FF_CHUNK
    nseq = max(tm // seq, 1)
    groups = b // nseq
    assert (seq % tm == 0) if nseq == 1 else (tm % seq == 0 and b % nseq == 0)
    kern = functools.partial(_ffn_kernel, tm=tm, nch=nch, nseq=nseq, final_norm=final_norm)
    tok = pl.BlockSpec((1, tm, d), lambda i, j: (i, j, 0))
    state = pl.BlockSpec((1, nseq, CONV_W - 1, ff), lambda i, j: (i, 0, 0, 0))
    y, conv = pl.pallas_call(
        kern,
        grid=(groups, nseq * seq // tm),
        in_specs=[tok, _const_spec(g.shape), state, _const_spec(wup.shape), _const_spec(wconv.shape),
                  _const_spec(bconv.shape), _const_spec(wdown.shape), _const_spec(gfin.shape)],
        out_specs=[tok, state],
        out_shape=[jax.ShapeDtypeStruct((groups, nseq * seq, d), F32),
                   jax.ShapeDtypeStruct((groups, nseq, CONV_W - 1, ff), F32)],
        scratch_shapes=[pltpu.VMEM((CONV_W - 1, ff), F32), pltpu.VMEM((tm, ff), BF16)],
        compiler_params=_params(("parallel", "arbitrary")),
        name="conv_ffn",
    )(x.reshape(groups, nseq * seq, d), g, prev.reshape(groups, nseq, CONV_W - 1, ff),
      wup, wconv, bconv, wdown, gfin)
    return y.reshape(b, seq, d), conv.reshape(b, CONV_W - 1, ff)


def _token_tile(n, pref):
    return pref if n % pref == 0 else n


def _tiles(b, t, prompt):
    n = b * t
    if t % 1024 == 0:
        ffn = 1024
    elif t % 512 == 0 or (512 % t == 0 and n % 512 == 0):
        ffn = 512
    else:
        ffn = t
    return dict(
        proj=_token_tile(n, WIN_A),
        merge=_token_tile(n, 1024),
        ffn=ffn,
        band_q=128 if prompt else t,
        diff_q=256,
        mem_q=_token_tile(t, 2048),
        cache_keys=4096,
        cache_item=512,
    )


def _layer(x, w, *, lam_init, final_norm, conv_prev, mem=None, cache=None):
    b, t, d = x.shape
    n = b * t
    tile = _tiles(b, t, cache is None)
    tm = tile["proj"]
    d_a, d_b, d_m = H_A * DH_A, H_B * 2 * DH_B, H_M * DH_M
    prompt = cache is None
    assert not prompt or (tm == WIN_A and t % tm == 0)
    kv_a = [(BF16, "rows"), (F32, "last_t")] if prompt else [(F32, "rows")]
    kv_b = [(BF16, "rows"), (F32, "by_head")] if prompt else [(F32, "by_head")]
    outs = ([(0, d_a, DH_A ** -0.5 * LOG2E, BF16, "rows")]
            + [(d_a, d_a, 1.0, dt, mode) for dt, mode in kv_a]
            + [(2 * d_a, d_a, 1.0, dt, mode) for dt, mode in kv_a]
            + [(3 * d_a, d_b, DH_B ** -0.5 * LOG2E, BF16, "rows")]
            + [(3 * d_a + d_b, d_b, 1.0, dt, mode) for dt, mode in kv_b]
            + [(3 * d_a + 2 * d_b, d_b, 1.0, dt, mode) for dt, mode in kv_b]
            + [(3 * d_a + 3 * d_b, d_m, DH_M ** -0.5 * LOG2E, BF16, "rows")])
    res = _norm_proj(x.reshape(n, d), w["g_mix"], w["w_in"], outs, tm, tiles_per_seq=max(t // tm, 1))
    res = [a.reshape(b, -1, a.shape[-1]) for a in res]
    if prompt:
        qa, ka, ka_keep, va, va_keep, qb, kb_bf, kb, vb_bf, vb, qm = res
    else:
        qa, ka, va, qb, kb, vb, qm = res
        ka_keep, va_keep = ka, va

    if cache is None:
        oa = _band_attn(qa, ka, va, _band_bias_table(w["rel_bias"], tile["band_q"]), tile["band_q"])
        ob = _diff_attn_prompt(qb, kb_bf, vb_bf, w["lam_q"], w["lam_k"], w["g_sub"], tile["diff_q"], lam_init)
        nm = mem.shape[1]
        mk, mv = _norm_proj(mem.reshape(b * nm, d), w["g_mem"], w["w_mem_kv"],
                            [(0, d_m, 1.0, F32, "by_head"), (d_m, d_m, 1.0, F32, "by_head")],
                            _token_tile(b * nm, WIN_A))
        mk = mk.reshape(b, nm * H_M, DH_M)
        mv = mv.reshape(b, nm * H_M, DH_M)
        om = _mem_attn(qm, mk, mv, tile["mem_q"])
    else:
        oa = _band_attn(qa, ka, va, _band_bias_table(w["rel_bias"], tile["band_q"]), tile["band_q"],
                        hist=(cache["a_k_t"], cache["a_v_t"]))
        ob = _diff_attn_sample(qb, cache["b_k"], cache["b_v"], kb, vb,
                               w["lam_q"], w["lam_k"], w["g_sub"], tile["cache_keys"], tile["cache_item"], lam_init)
        mk, mv = cache["m_k"], cache["m_v"]
        om = _mem_attn(qm, mk, mv, tile["mem_q"])

    flat = lambda a: a.reshape(n, a.shape[-1])
    x1 = _merge(x.reshape(n, d), flat(oa), flat(ob), flat(om), w["g_mix"], w["w_in"], w["b_gate"],
                w["w_o"], w["w_out"], tile["merge"])
    y, conv = _conv_ffn(x1.reshape(b, t, d), w["g_ffn"], conv_prev, w["w_up"], w["w_conv"], w["b_conv"],
                        w["w_down"], w["g_final"], tile["ffn"], final_norm)
    return y, dict(ka=ka_keep, va=va_keep, kb=kb, vb=vb, mk=mk, mv=mv, conv=conv)


def kernel(x_prompt, x_sample, mem_prompt, cache_a_k, cache_a_v, cache_b_k, cache_b_v, cache_mem_k, cache_mem_v, state_ffn_conv, g_mix, w_in, b_gate, rel_bias, lam_q, lam_k, g_sub, g_mem, w_mem_kv, w_oa, w_ob, w_om, w_out, g_ffn, w_up, w_conv, b_conv, w_down, g_final):
    depth, d, _ = w_in.shape
    b, s, _ = x_prompt.shape
    bd, t, _ = x_sample.shape
    past = cache_b_k.shape[2]
    a_buf = cache_a_k.shape[2]
    ff = w_down.shape[1]
    d_a, d_b, d_m = H_A * DH_A, H_B * 2 * DH_B, H_M * DH_M
    a_keep = min(WIN_A, s)
    assert a_buf == WIN_A and a_keep == WIN_A and t == CHUNK and past % CHUNK == 0 and ff % FF_CHUNK == 0
    nch = ff // FF_CHUNK

    xp, xs = x_prompt, x_sample
    outs = {k: [] for k in ("pa_k", "pa_v", "pb_k", "pb_v", "pm_k", "pm_v", "pconv",
                            "sa_k", "sa_v", "sb_k", "sb_v", "sconv")}
    for l in range(depth):
        lam_init = 0.8 - 0.6 * math.exp(-0.3 * l)
        w = dict(
            g_mix=g_mix[l][None], g_mem=g_mem[l][None], g_ffn=g_ffn[l][None], g_final=g_final[None],
            g_sub=g_sub[l][None], lam_q=lam_q[l], lam_k=lam_k[l], rel_bias=rel_bias[l],
            w_in=w_in[l].astype(BF16),
            b_gate=b_gate[l][None],
            w_mem_kv=w_mem_kv[l].astype(BF16),
            w_o=jnp.stack([w_oa[l], w_ob[l], w_om[l]]).astype(BF16),
            w_out=w_out[l].astype(BF16),
            w_up=w_up[l].astype(BF16),
            w_conv=w_conv[l], b_conv=b_conv[l][None],
            w_down=w_down[l].astype(BF16),
        )
        last = l == depth - 1
        xp, po = _layer(xp, w, lam_init=lam_init, final_norm=last,
                        conv_prev=jnp.zeros((b, CONV_W - 1, ff), F32), mem=mem_prompt)
        feat_major = lambda a: a.transpose(0, 2, 3, 1).reshape(bd, d_a, a_buf)
        cache = dict(a_k=cache_a_k[l].reshape(bd, a_buf, d_a), a_v=cache_a_v[l].reshape(bd, a_buf, d_a),
                     a_k_t=feat_major(cache_a_k[l]), a_v_t=feat_major(cache_a_v[l]),
                     b_k=cache_b_k[l].reshape(bd, past * H_B, 2 * DH_B),
                     b_v=cache_b_v[l].reshape(bd, past * H_B, 2 * DH_B),
                     m_k=cache_mem_k[l].reshape(bd, -1, DH_M), m_v=cache_mem_v[l].reshape(bd, -1, DH_M))
        xs, so = _layer(xs, w, lam_init=lam_init, final_norm=last, conv_prev=state_ffn_conv[l], cache=cache)

        tok_major = lambda a: a.reshape(b, H_A, DH_A, a_keep).transpose(0, 3, 1, 2)
        outs["pa_k"].append(tok_major(po["ka"]))
        outs["pa_v"].append(tok_major(po["va"]))
        outs["pb_k"].append(po["kb"].reshape(b, s, H_B, 2 * DH_B))
        outs["pb_v"].append(po["vb"].reshape(b, s, H_B, 2 * DH_B))
        outs["pm_k"].append(po["mk"].reshape(b, -1, H_M, DH_M))
        outs["pm_v"].append(po["mv"].reshape(b, -1, H_M, DH_M))
        outs["pconv"].append(po["conv"])
        new_k = jnp.concatenate([cache["a_k"], so["ka"]], axis=1)[:, t:]
        new_v = jnp.concatenate([cache["a_v"], so["va"]], axis=1)[:, t:]
        outs["sa_k"].append(new_k.reshape(bd, a_buf, H_A, DH_A))
        outs["sa_v"].append(new_v.reshape(bd, a_buf, H_A, DH_A))
        outs["sb_k"].append(so["kb"].reshape(bd, t, H_B, 2 * DH_B))
        outs["sb_v"].append(so["vb"].reshape(bd, t, H_B, 2 * DH_B))
        outs["sconv"].append(so["conv"])

    st = {k: jnp.stack(v) for k, v in outs.items()}
    return (xp, xs, st["pa_k"], st["pa_v"], st["pb_k"], st["pb_v"], st["pm_k"], st["pm_v"], st["pconv"],
            st["sa_k"], st["sa_v"], st["sb_k"], st["sb_v"], st["sconv"])
```

```python
import functools
import math

import jax
import jax.numpy as jnp
from jax import lax
from jax.experimental import pallas as pl
from jax.experimental.pallas import tpu as pltpu

CHUNK = 64
BAND_CHUNKS = 8
WIN_A = BAND_CHUNKS * CHUNK
REL_CLIP = 128
H_A = 8
DH_A = 64
H_B = 4
DH_B = 64
H_M = 4
DH_M = 128
N_BRANCH = 3
CONV_W = 3
EPS = 1e-6
NEG_INF = -1e30

LANES = 128
FF_CHUNK = 256
VMEM_LIMIT = 56 * 1024 * 1024

LOG2E = math.log2(math.e)

BF16 = jnp.bfloat16
F32 = jnp.float32


def _rms(x, g):
    y = x * lax.rsqrt(jnp.mean(x * x, axis=-1, keepdims=True) + EPS)
    return y * g


def _dot(a, b):
    return jnp.dot(a, b, preferred_element_type=F32)


def _dot_nt(a, b):
    return lax.dot_general(a, b, (((1,), (1,)), ((), ())), preferred_element_type=F32)


def _dot_tn(a, b):
    return lax.dot_general(a, b, (((0,), (0,)), ((), ())), preferred_element_type=F32)


def _const_spec(shape):
    nd = len(shape)
    return pl.BlockSpec(shape, lambda *_: (0,) * nd, pipeline_mode=pl.Buffered(1))


def _params(sem, flags=None):
    return pltpu.CompilerParams(dimension_semantics=sem, vmem_limit_bytes=VMEM_LIMIT, flags=flags)


def _head_rows(ref, h, n, heads, lead=(), first=0):
    return ref.at[lead + (pl.ds(first * heads + h, n, stride=heads), slice(None))]


def _norm_proj_kernel(x_ref, g_ref, w_ref, *o_refs, outs, tiles_per_seq):
    tm = x_ref.shape[0]
    last_tile = pl.program_id(0) % tiles_per_seq == tiles_per_seq - 1
    h = _rms(x_ref[...], g_ref[...]).astype(BF16)
    z_of = {}
    transposed = []
    for o_ref, (c0, wd, sc, _, mode) in zip(o_refs, outs):
        if (c0, wd) not in z_of:
            z_of[(c0, wd)] = _dot(h, w_ref[:, c0:c0 + wd])
        z = z_of[(c0, wd)]
        if sc != 1.0:
            z = z * sc
        if mode == "by_head":
            heads = wd // LANES
            for hd in range(heads):
                _head_rows(o_ref, hd, tm, heads)[...] = z[:, hd * LANES:(hd + 1) * LANES].astype(o_ref.dtype)
        elif mode == "last_t":
            transposed.append((o_ref, z))
        else:
            o_ref[...] = z.astype(o_ref.dtype)

    if transposed:
        @pl.when(last_tile)
        def _():
            for o_ref, z in transposed:
                o_ref[...] = z.T.astype(o_ref.dtype)


def _norm_proj(x, g, w, outs, tm, tiles_per_seq=1):
    n, d = x.shape
    kern = functools.partial(_norm_proj_kernel, outs=outs, tiles_per_seq=tiles_per_seq)
    specs, shapes = [], []
    for _, wd, _, dt, mode in outs:
        if mode == "by_head":
            blk, rows, imap = ((wd // LANES) * tm, LANES), (wd // LANES) * n, lambda i: (i, 0)
        elif mode == "last_t":
            blk, rows, imap = (wd, tm), (n // tm // tiles_per_seq) * wd, lambda i: (i // tiles_per_seq, 0)
        else:
            blk, rows, imap = (tm, wd), n, lambda i: (i, 0)
        specs.append(pl.BlockSpec(blk, imap))
        shapes.append(jax.ShapeDtypeStruct((rows, blk[1]), dt))
    return pl.pallas_call(
        kern,
        grid=(n // tm,),
        in_specs=[
            pl.BlockSpec((tm, d), lambda i: (i, 0)),
            _const_spec((1, d)),
            _const_spec(w.shape),
        ],
        out_specs=specs,
        out_shape=shapes,
        compiler_params=_params(("arbitrary",)),
        name="norm_proj",
    )(x, g, w)


def _band_kernel(*refs, tq, seq, use_hist):
    if use_hist:
        q_ref, k_ref, v_ref, hk_ref, hv_ref, bias_ref, o_ref, kp, vp, s_scr, qt_scr = refs
    else:
        q_ref, k_ref, v_ref, bias_ref, o_ref, kp, vp, s_scr, qt_scr = refs
    win = WIN_A + tq
    nblk = seq // tq
    npair = H_A // 2

    if use_hist:
        kp[0:WIN_A, :] = hk_ref[0].T.astype(BF16)
        vp[0:WIN_A, :] = hv_ref[0].T.astype(BF16)
    else:
        kp[0:WIN_A, :] = jnp.zeros((WIN_A, kp.shape[1]), BF16)
        vp[0:WIN_A, :] = jnp.zeros((WIN_A, vp.shape[1]), BF16)
    kp[WIN_A:WIN_A + seq, :] = k_ref[0].astype(BF16)
    vp[WIN_A:WIN_A + seq, :] = v_ref[0].astype(BF16)

    feat = lax.broadcasted_iota(jnp.int32, (LANES, tq), 0)
    lo = feat < DH_A
    slabs = [slice(hp * LANES, (hp + 1) * LANES) for hp in range(npair)]

    for blk in range(nblk):
        for hp in range(npair):
            qt_scr[hp, blk] = _split_maps_t(q_ref[0, blk * tq:(blk + 1) * tq, slabs[hp]])

    def stage_a(m, hp, mask_history, slot):
        off = m * tq if isinstance(m, int) else pl.multiple_of(m * tq, tq)
        s = _dot(kp[pl.ds(off, win), slabs[hp]], qt_scr[hp, m]) + bias_ref[hp]
        if mask_history:
            key = lax.broadcasted_iota(jnp.int32, (win, 2 * tq), 0)
            s = jnp.where(key >= (WIN_A - m * tq), s, NEG_INF)
        s_scr[slot, hp] = s
        return jnp.max(s, axis=0, keepdims=True)

    def stage_b(m, hp, mx, slot):
        off = pl.multiple_of(m * tq, tq)
        p = jnp.exp2(s_scr[slot, hp] - mx)
        l = jnp.sum(p, axis=0, keepdims=True)
        o = _dot_tn(vp[pl.ds(off, win), slabs[hp]], p.astype(BF16)) / l
        o_ref[0, pl.ds(off, tq), slabs[hp]] = jnp.where(lo, o[:, :tq], o[:, tq:]).T.astype(o_ref.dtype)

    nmask = 0 if use_hist else min(WIN_A // tq, nblk)
    wide = s_scr.shape[0]
    assert nblk % wide == 0 and nmask % wide == 0

    def group(mask_history, g, mxs):
        blocks = [g * wide + i for i in range(wide)]
        for hp in range(1, npair):
            nxt = []
            for i in range(wide):
                nxt.append(stage_a(blocks[i], hp, mask_history, i))
                stage_b(blocks[i], hp - 1, mxs[i], i)
            mxs = nxt
        nxt = []
        for i in range(wide):
            nxt.append(stage_a(jnp.minimum(blocks[i] + wide, nblk - wide + i), 0, mask_history, i))
            stage_b(blocks[i], npair - 1, mxs[i], i)
        return tuple(nxt)

    mxs = tuple(stage_a(i, 0, nmask > 0, i) for i in range(wide))
    unroll = lambda trips: 2 if trips % 2 == 0 else 1
    mxs = lax.fori_loop(0, nmask // wide, functools.partial(group, True), mxs, unroll=unroll(nmask // wide))
    lax.fori_loop(nmask // wide, nblk // wide, functools.partial(group, False), mxs,
                  unroll=unroll((nblk - nmask) // wide))


def _band_blocks_in_flight(nblk):
    return 2 if nblk % 2 == 0 else 1


def _band_attn(q, k, v, bias, tq, hist=None):
    b, seq, da = q.shape
    win = WIN_A + tq
    use_hist = hist is not None
    kern = functools.partial(_band_kernel, tq=tq, seq=seq, use_hist=use_hist)
    full = pl.BlockSpec((1, seq, da), lambda i: (i, 0, 0))
    in_specs = [full, full, full]
    args = [q, k, v]
    if use_hist:
        hspec = pl.BlockSpec((1, da, WIN_A), lambda i: (i, 0, 0))
        in_specs += [hspec, hspec]
        args += list(hist)
    in_specs.append(_const_spec(bias.shape))
    args.append(bias)
    return pl.pallas_call(
        kern,
        grid=(b,),
        in_specs=in_specs,
        out_specs=full,
        out_shape=jax.ShapeDtypeStruct((b, seq, da), BF16),
        scratch_shapes=[pltpu.VMEM((WIN_A + seq, da), BF16), pltpu.VMEM((WIN_A + seq, da), BF16),
                        pltpu.VMEM((_band_blocks_in_flight(seq // tq), H_A // 2, win, 2 * tq), F32),
                        pltpu.VMEM((H_A // 2, seq // tq, LANES, 2 * tq), BF16)],
        compiler_params=_params(("parallel",)),
        name="band_attn",
    )(*args)


BIAS_LANE0 = WIN_A + REL_CLIP
BIAS_WIDTH = 768


def _band_bias_kernel(rb_ref, o_ref, *, tq, win):
    z = lax.broadcasted_iota(jnp.int32, (1, BIAS_WIDTH), 1)
    j = lax.broadcasted_iota(jnp.int32, (win, tq), 0)
    r = lax.broadcasted_iota(jnp.int32, (win, tq), 1)
    kc = j // CHUNK
    qc = r // CHUNK
    band = (kc >= qc) & (kc <= qc + BAND_CHUNKS)
    for h in range(H_A):
        row = jnp.where(z >= 2 * REL_CLIP, rb_ref[h:h + 1, 2 * REL_CLIP:2 * REL_CLIP + 1], rb_ref[h:h + 1, :])
        skew = pltpu.roll(jnp.broadcast_to(row, (win, BIAS_WIDTH)), 0, axis=1, stride=1, stride_axis=0)
        tbl = jnp.where(band, skew[:, BIAS_LANE0:BIAS_LANE0 + tq] * LOG2E, NEG_INF)
        o_ref[h // 2, :, (h % 2) * tq:(h % 2 + 1) * tq] = tbl


def _band_bias_table(rel_bias, tq):
    win = WIN_A + tq
    assert BIAS_LANE0 + tq <= BIAS_WIDTH and win <= BIAS_WIDTH
    rb = jnp.pad(rel_bias.astype(F32), ((0, 0), (0, BIAS_WIDTH - rel_bias.shape[1])))
    return pl.pallas_call(
        functools.partial(_band_bias_kernel, tq=tq, win=win),
        out_shape=jax.ShapeDtypeStruct((H_A // 2, win, 2 * tq), F32),
        compiler_params=pltpu.CompilerParams(vmem_limit_bytes=VMEM_LIMIT),
        name="band_bias",
    )(rb)


def _split_maps_t(q):
    qt = q.astype(F32).T
    feat = lax.broadcasted_iota(jnp.int32, qt.shape, 0)
    lo = feat < DH_B
    return jnp.concatenate([jnp.where(lo, qt, 0.0), jnp.where(lo, 0.0, qt)], axis=1).astype(BF16)


def _key_query_diff(tk, tq):
    c = lax.broadcasted_iota(jnp.int32, (tk, 2 * tq), 0)
    r = lax.broadcasted_iota(jnp.int32, (tk, 2 * tq), 1)
    rr = jnp.where(r >= tq, r - tq, r)
    return rr, c


def _alibi_slope(h):
    return 2.0 ** (-8.0 * (h + 1) / H_B) * LOG2E


def _online_update(s, vb, shift, carry):
    m, l, acc = carry
    m_new = jnp.maximum(m, jnp.max(s, axis=0, keepdims=True) + shift)
    alpha = jnp.exp2(m - m_new)
    p = jnp.exp2(s - (m_new - shift))
    l = alpha * l + jnp.sum(p, axis=0, keepdims=True)
    acc = alpha * acc + _dot_tn(vb, p.astype(BF16))
    return m_new, l, acc


def _diff_finish_t(carry, tq, lam, gsub, lam_init):
    _, l, acc = carry
    o = acc / l
    od = (o[:, :tq] - lam * o[:, tq:]).T
    return _rms(od, gsub) * (1.0 - lam_init)


def _diff_lambda(lq_ref, lk_ref, lam_init):
    e = jnp.exp(jnp.sum(lq_ref[...] * lk_ref[...], axis=-1, keepdims=True))
    return e[0:1] - e[1:2] + lam_init


def _diff_init_t(tq):
    return (jnp.full((1, 2 * tq), NEG_INF, F32), jnp.zeros((1, 2 * tq), F32),
            jnp.zeros((LANES, 2 * tq), F32))


def _diff_prompt_kernel(q_ref, k_ref, v_ref, lq_ref, lk_ref, gsub_ref, o_ref,
                        bias_scr, qq_scr, s0_scr, s1_scr, acc_scr, *, tq, lam_init):
    sp = pl.program_id(1)
    slabs = [slice(h * LANES, (h + 1) * LANES) for h in range(H_B)]
    chains = [(e, bb, h) for e in range(2) for bb in range(q_ref.shape[0]) for h in range(H_B)]
    every = list(range(len(chains)))
    even = [c for c in every if chains[c][0] == 0]
    odd = [c for c in every if chains[c][0] == 1]

    @pl.when(sp == 0)
    def _():
        rr, c = _key_query_diff(tq, tq)
        d = (rr - c).astype(F32)
        allowed = (c // CHUNK) <= (rr // CHUNK)
        for h in range(H_B):
            bias_scr[0, h] = -_alibi_slope(h) * d
            bias_scr[1, h] = jnp.where(allowed, -_alibi_slope(h) * jnp.abs(d), NEG_INF)

    for c, (e, bb, h) in enumerate(chains):
        qq_scr[c] = _split_maps_t(q_ref[bb, e * tq:(e + 1) * tq, slabs[h]])
    acc_scr[...] = jnp.zeros(acc_scr.shape, F32)
    bufs = (s0_scr, s1_scr)

    def chain_a(j, buf, c, m):
        e, bb, h = chains[c]
        qi = 2 * sp + e
        off = j * tq if isinstance(j, int) else pl.multiple_of(j * tq, tq)
        diag = (j == qi).astype(jnp.int32)
        s = _dot(k_ref[bb, pl.ds(off, tq), slabs[h]], qq_scr[c]) + bias_scr[diag, h]
        bufs[buf][c] = s
        shift = -_alibi_slope(h) * ((qi - j) * tq).astype(F32)
        m_new = jnp.maximum(m, jnp.max(s, axis=0, keepdims=True) + shift)
        return m_new, jnp.exp2(m - m_new), m_new - shift

    def chain_b(j, buf, c, l, alpha, sub):
        e, bb, h = chains[c]
        off = j * tq if isinstance(j, int) else pl.multiple_of(j * tq, tq)
        p = jnp.exp2(bufs[buf][c] - sub)
        acc_scr[c] = alpha * acc_scr[c] + _dot_tn(v_ref[bb, pl.ds(off, tq), slabs[h]], p.astype(BF16))
        return alpha * l + jnp.sum(p, axis=0, keepdims=True)

    def stages(a_spec, b_spec, state):
        ms, ls, pend = [list(x) for x in state[:2]] + [[[list(x) for x in p] for p in state[2]]]
        for c in every:
            if a_spec is not None and c in a_spec[2]:
                ms[c], pend[a_spec[1]][0][c], pend[a_spec[1]][1][c] = chain_a(a_spec[0], a_spec[1], c, ms[c])
            if b_spec is not None and c in b_spec[2]:
                ls[c] = chain_b(b_spec[0], b_spec[1], c, ls[c], pend[b_spec[1]][0][c], pend[b_spec[1]][1][c])
        return tuple(ms), tuple(ls), tuple((tuple(p[0]), tuple(p[1])) for p in pend)

    def finish(which, ls):
        lam = _diff_lambda(lq_ref, lk_ref, lam_init)
        for c in which:
            e, bb, h = chains[c]
            o_ref[bb, e * tq:(e + 1) * tq, slabs[h]] = _diff_finish_t(
                (None, ls[c], acc_scr[c]), tq, lam, gsub_ref[...], lam_init).astype(o_ref.dtype)

    row = lambda val: tuple(jnp.full((1, 2 * tq), val, F32) for _ in chains)
    state = (row(NEG_INF), row(0.0), ((row(0.0), row(0.0)), (row(0.0), row(0.0))))
    state = stages((0, 0, every), None, state)

    def two_blocks(jj, state):
        j = 2 * jj
        state = stages((j + 1, 1, every), (j, 0, every), state)
        return stages((j + 2, 0, every), (j + 1, 1, every), state)

    state = lax.fori_loop(0, sp, two_blocks, state)
    state = stages((2 * sp + 1, 1, odd), (2 * sp, 0, every), state)
    finish(even, state[1])
    state = stages(None, (2 * sp + 1, 1, odd), state)
    finish(odd, state[1])


def _diff_attn_prompt(q, k, v, lam_q, lam_k, g_sub, tq, lam_init):
    b, seq, db = q.shape
    nbat = 2 if b % 2 == 0 else 1
    kern = functools.partial(_diff_prompt_kernel, tq=tq, lam_init=lam_init)
    assert seq % (2 * tq) == 0
    nchain = 2 * nbat * H_B
    kv_spec = pl.BlockSpec((nbat, seq, db), lambda i, j: (i, 0, 0))
    q_spec = pl.BlockSpec((nbat, 2 * tq, db), lambda i, j: (i, j, 0))
    return pl.pallas_call(
        kern,
        grid=(b // nbat, seq // (2 * tq)),
        in_specs=[q_spec, kv_spec, kv_spec,
                  _const_spec(lam_q.shape), _const_spec(lam_k.shape), _const_spec(g_sub.shape)],
        out_specs=q_spec,
        out_shape=jax.ShapeDtypeStruct((b, seq, db), BF16),
        scratch_shapes=[pltpu.VMEM((2, H_B, tq, 2 * tq), F32), pltpu.VMEM((nchain, LANES, 2 * tq), BF16),
                        pltpu.VMEM((nchain, tq, 2 * tq), F32), pltpu.VMEM((nchain, tq, 2 * tq), F32),
                        pltpu.VMEM((nchain, LANES, 2 * tq), F32)],
        compiler_params=_params(("parallel", "arbitrary")),
        name="diff_attn_prompt",
    )(q, k, v, lam_q, lam_k, g_sub)


def _diff_sample_kernel(q_ref, kc_ref, vc_ref, kn_ref, vn_ref, lq_ref, lk_ref, gsub_ref, o_ref,
                        m_s, l_s, acc_s, s_scr, *, t, past, tk, ts, lam_init):
    ci = pl.program_id(1)

    @pl.when(ci == 0)
    def _():
        for h in range(H_B):
            m_s[h], l_s[h], acc_s[h] = _diff_init_t(t)

    slabs = [slice(h * LANES, (h + 1) * LANES) for h in range(H_B)]
    qqts = [_split_maps_t(q_ref[0, :, cs]) for cs in slabs]
    rr, c = _key_query_diff(ts, t)
    d = (rr - c).astype(F32)
    rows = lambda ref, h, n, first=0: _head_rows(ref, h, n, H_B, lead=(0,), first=first)[...].astype(BF16)
    nstream = s_scr.shape[0]
    per = H_B // nstream
    items = [[(sub, h) for sub in range(tk // ts) for h in range(st * per, (st + 1) * per)]
             for st in range(nstream)]
    nitem = len(items[0])

    def stage_a(st, i):
        sub, h = items[st][i]
        s = _dot(rows(kc_ref, h, ts, sub * ts), qqts[h]) - _alibi_slope(h) * d
        s_scr[st, i % 2] = s
        shift = -_alibi_slope(h) * (past - ci * tk - sub * ts).astype(F32)
        m_new = jnp.maximum(m_s[h], jnp.max(s, axis=0, keepdims=True) + shift)
        alpha = jnp.exp2(m_s[h] - m_new)
        m_s[h] = m_new
        return alpha, m_new - shift

    def stage_b(st, i, alpha, ref_max):
        sub, h = items[st][i]
        p = jnp.exp2(s_scr[st, i % 2] - ref_max)
        l_s[h] = alpha * l_s[h] + jnp.sum(p, axis=0, keepdims=True)
        acc_s[h] = alpha * acc_s[h] + _dot_tn(rows(vc_ref, h, ts, sub * ts), p.astype(BF16))

    pend = [stage_a(st, 0) for st in range(nstream)]
    for i in range(1, nitem):
        nxt = [stage_a(st, i) for st in range(nstream)]
        for st in range(nstream):
            stage_b(st, i - 1, *pend[st])
        pend = nxt
    for st in range(nstream):
        stage_b(st, nitem - 1, *pend[st])

    @pl.when(ci == past // tk - 1)
    def _():
        rn, cn = _key_query_diff(t, t)
        dn = jnp.abs((rn - cn).astype(F32))
        lam = _diff_lambda(lq_ref, lk_ref, lam_init)
        for h, cs in enumerate(slabs):
            s = _dot(rows(kn_ref, h, t), qqts[h]) - _alibi_slope(h) * dn
            carry = _online_update(s, rows(vn_ref, h, t), 0.0, (m_s[h], l_s[h], acc_s[h]))
            o_ref[0, :, cs] = _diff_finish_t(carry, t, lam, gsub_ref[...], lam_init).astype(o_ref.dtype)


def _diff_attn_sample(q, kc, vc, kn, vn, lam_q, lam_k, g_sub, tk, ts, lam_init):
    b, t, db = q.shape
    past = kc.shape[1] // H_B
    tk = min(tk, past)
    assert past % tk == 0 and tk % ts == 0
    kern = functools.partial(_diff_sample_kernel, t=t, past=past, tk=tk, ts=ts, lam_init=lam_init)
    new_spec = pl.BlockSpec((1, t, db), lambda i, j: (i, 0, 0))
    new_kv_spec = pl.BlockSpec((1, t * H_B, LANES), lambda i, j: (i, 0, 0))
    cache_spec = pl.BlockSpec((1, tk * H_B, LANES), lambda i, j: (i, j, 0))
    return pl.pallas_call(
        kern,
        grid=(b, past // tk),
        in_specs=[new_spec, cache_spec, cache_spec, new_kv_spec, new_kv_spec,
                  _const_spec(lam_q.shape), _const_spec(lam_k.shape), _const_spec(g_sub.shape)],
        out_specs=new_spec,
        out_shape=jax.ShapeDtypeStruct((b, t, db), BF16),
        scratch_shapes=[pltpu.VMEM((H_B, 1, 2 * t), F32), pltpu.VMEM((H_B, 1, 2 * t), F32),
                        pltpu.VMEM((H_B, LANES, 2 * t), F32), pltpu.VMEM((2, 2, ts, 2 * t), F32)],
        compiler_params=_params(("parallel", "arbitrary")),
        name="diff_attn_sample",
    )(q, kc, vc, kn, vn, lam_q, lam_k, g_sub)


def _mem_kernel(q_ref, mk_ref, mv_ref, o_ref):
    slabs = [slice(h * LANES, (h + 1) * LANES) for h in range(H_M)]
    nm = mk_ref.shape[1] // H_M
    rows = lambda ref, h: _head_rows(ref, h, nm, H_M, lead=(0,))[...].astype(BF16)
    scores = [_dot_nt(q_ref[0, :, cs], rows(mk_ref, h)) for h, cs in enumerate(slabs)]
    for h, cs in enumerate(slabs):
        mx = jnp.max(scores[h], axis=-1, keepdims=True)
        p = jnp.exp2(scores[h] - mx)
        l = jnp.sum(p, axis=-1, keepdims=True)
        o_ref[0, :, cs] = (_dot(p.astype(BF16), rows(mv_ref, h)) / l).astype(o_ref.dtype)


def _mem_attn(q, mk, mv, tq):
    b, seq, dm = q.shape
    kv_spec = pl.BlockSpec((1,) + mk.shape[1:], lambda i, j: (i, 0, 0))
    q_spec = pl.BlockSpec((1, tq, dm), lambda i, j: (i, j, 0))
    return pl.pallas_call(
        _mem_kernel,
        grid=(b, seq // tq),
        in_specs=[q_spec, kv_spec, kv_spec],
        out_specs=q_spec,
        out_shape=jax.ShapeDtypeStruct((b, seq, dm), BF16),
        compiler_params=_params(("parallel", "parallel")),
        name="mem_attn",
    )(q, mk, mv)


def _merge_kernel(*refs, tn, fuse_mem):
    if fuse_mem:
        (x_ref, oa_ref, ob_ref, qm_ref, mk_ref, mv_ref, g_ref, win_ref, bg_ref, wo_ref, wout_ref,
         o_ref, m_scr, om_ref) = refs
        slabs = [slice(hd * LANES, (hd + 1) * LANES) for hd in range(H_M)]
        nm = mk_ref.shape[1] // H_M
        rows = lambda ref, hd: _head_rows(ref, hd, nm, H_M, lead=(0,))[...].astype(BF16)
        scores = [_dot_nt(qm_ref[:, cs], rows(mk_ref, hd)) for hd, cs in enumerate(slabs)]
        for hd, cs in enumerate(slabs):
            mx = jnp.max(scores[hd], axis=-1, keepdims=True)
            p = jnp.exp2(scores[hd] - mx)
            l = jnp.sum(p, axis=-1, keepdims=True)
            om_ref[:, cs] = (_dot(p.astype(BF16), rows(mv_ref, hd)) / l).astype(BF16)
    else:
        x_ref, oa_ref, ob_ref, om_ref, g_ref, win_ref, bg_ref, wo_ref, wout_ref, o_ref, m_scr = refs
    x = x_ref[...]
    h = _rms(x, g_ref[...]).astype(BF16)
    o_refs = (oa_ref, ob_ref, om_ref)
    d = x.shape[1]
    for c0 in range(0, d, tn):
        cs = slice(c0, c0 + tn)
        m = None
        for br in range(N_BRANCH):
            g0 = br * d + c0
            goff = win_ref.shape[1] - N_BRANCH * d
            gate = jax.nn.sigmoid(_dot(h, win_ref[:, goff + g0:goff + g0 + tn]) + bg_ref[:, g0:g0 + tn])
            term = gate * _dot(o_refs[br][...], wo_ref[br, :, cs])
            m = term if m is None else m + term
        m_scr[:, cs] = m.astype(BF16)
    o_ref[...] = x + _dot(m_scr[...], wout_ref[...])


def _merge(x, oa, ob, om, g, wg, bg, wo, wout, tm, tn=256, mem=None):
    n, d = x.shape
    kern = functools.partial(_merge_kernel, tn=tn, fuse_mem=mem is not None)
    tok = lambda wd: pl.BlockSpec((tm, wd), lambda i: (i, 0))
    in_specs = [tok(d), tok(oa.shape[1]), tok(ob.shape[1]), tok(om.shape[1])]
    args = [x, oa, ob, om]
    scratch = [pltpu.VMEM((tm, d), BF16)]
    if mem is not None:
        mk, mv, tiles_per_seq = mem
        kv_spec = pl.BlockSpec((1,) + mk.shape[1:], lambda i: (i // tiles_per_seq, 0, 0))
        in_specs += [kv_spec, kv_spec]
        args += [mk, mv]
        scratch.append(pltpu.VMEM((tm, om.shape[1]), BF16))
    in_specs += [_const_spec(g.shape), _const_spec(wg.shape), _const_spec(bg.shape), _const_spec(wo.shape),
                 _const_spec(wout.shape)]
    args += [g, wg, bg, wo, wout]
    return pl.pallas_call(
        kern,
        grid=(n // tm,),
        in_specs=in_specs,
        out_specs=tok(d),
        out_shape=jax.ShapeDtypeStruct((n, d), F32),
        scratch_shapes=scratch,
        compiler_params=_params(("parallel",)),
        name="merge",
    )(*args)


def _ffn_kernel(x_ref, g_ref, prev_ref, wup_ref, wconv_ref, bconv_ref, wdown_ref, gfin_ref,
                y_ref, conv_ref, carry, u_scr, *, tm, nch, nseq, final_norm):
    if nseq == 1:
        @pl.when(pl.program_id(1) == 0)
        def _():
            carry[...] = prev_ref[0, 0]

    x = x_ref[0]
    h = _rms(x, g_ref[...]).astype(BF16)
    row = lax.broadcasted_iota(jnp.int32, (tm, FF_CHUNK), 0)
    seq_rows = tm // nseq
    ff = nch * FF_CHUNK
    for ch in range(nch):
        cs = slice(ch * FF_CHUNK, (ch + 1) * FF_CHUNK)
        a = _dot(h, wup_ref[:, cs])
        gate = _dot(h, wup_ref[:, ff + ch * FF_CHUNK:ff + (ch + 1) * FF_CHUNK])
        a1 = pltpu.roll(a, 1, axis=0)
        a2 = pltpu.roll(a, 2, axis=0)
        for k in range(nseq):
            before = carry if nseq == 1 else prev_ref.at[0, k]
            p2 = before[0:1, cs]
            p1 = before[1:2, cs]
            a1 = jnp.where(row == k * seq_rows, p1, a1)
            a2 = jnp.where(row == k * seq_rows, p2, jnp.where(row == k * seq_rows + 1, p1, a2))
        conv = bconv_ref[:, cs] + a2 * wconv_ref[0:1, cs]
        conv = conv + a1 * wconv_ref[1:2, cs]
        conv = conv + a * wconv_ref[2:3, cs]
        if nseq == 1:
            carry[:, cs] = a[tm - (CONV_W - 1):, :]
        else:
            for k in range(nseq):
                conv_ref[0, k, :, cs] = a[(k + 1) * seq_rows - (CONV_W - 1):(k + 1) * seq_rows, :]
        u_scr[:, cs] = (jax.nn.gelu(conv) * gate).astype(BF16)
    y = x + _dot(u_scr[...], wdown_ref[...])
    if final_norm:
        y = _rms(y, gfin_ref[...])
    y_ref[0] = y
    if nseq == 1:
        conv_ref[0, 0] = carry[...]


def _conv_ffn(x, g, prev, wup, wconv, bconv, wdown, gfin, tm, final_norm):
    b, seq, d = x.shape
    ff = wdown.shape[0]
    nch = ff // FF_CHUNK
    nseq = max(tm // seq, 1)
    groups = b // nseq
    assert (seq % tm == 0) if nseq == 1 else (tm % seq == 0 and b % nseq == 0)
    kern = functools.partial(_ffn_kernel, tm=tm, nch=nch, nseq=nseq, final_norm=final_norm)
    tok = pl.BlockSpec((1, tm, d), lambda i, j: (i, j, 0))
    state = pl.BlockSpec((1, nseq, CONV_W - 1, ff), lambda i, j: (i, 0, 0, 0))
    y, conv = pl.pallas_call(
        kern,
        grid=(groups, nseq * seq // tm),
        in_specs=[tok, _const_spec(g.shape), state, _const_spec(wup.shape), _const_spec(wconv.shape),
                  _const_spec(bconv.shape), _const_spec(wdown.shape), _const_spec(gfin.shape)],
        out_specs=[tok, state],
        out_shape=[jax.ShapeDtypeStruct((groups, nseq * seq, d), F32),
                   jax.ShapeDtypeStruct((groups, nseq, CONV_W - 1, ff), F32)],
        scratch_shapes=[pltpu.VMEM((CONV_W - 1, ff), F32), pltpu.VMEM((tm, ff), BF16)],
        compiler_params=_params(("parallel", "arbitrary")),
        name="conv_ffn",
    )(x.reshape(groups, nseq * seq, d), g, prev.reshape(groups, nseq, CONV_W - 1, ff),
      wup, wconv, bconv, wdown, gfin)
    return y.reshape(b, seq, d), conv.reshape(b, CONV_W - 1, ff)


def _token_tile(n, pref):
    return pref if n % pref == 0 else n


def _tiles(b, t, prompt):
    n = b * t
    if t % 1024 == 0:
        ffn = 1024
    elif t % 512 == 0 or (512 % t == 0 and n % 512 == 0):
        ffn = 512
    else:
        ffn = t
    return dict(
        proj=_token_tile(n, WIN_A),
        merge=_token_tile(n, 1024),
        ffn=ffn,
        band_q=128 if prompt else t,
        diff_q=256,
        mem_q=_token_tile(t, 2048),
        cache_keys=4096,
        cache_item=512,
    )


def _layer(x, w, *, lam_init, final_norm, conv_prev, mem=None, cache=None):
    b, t, d = x.shape
    n = b * t
    tile = _tiles(b, t, cache is None)
    tm = tile["proj"]
    d_a, d_b, d_m = H_A * DH_A, H_B * 2 * DH_B, H_M * DH_M
    prompt = cache is None
    assert not prompt or (tm == WIN_A and t % tm == 0)
    kv_a = [(BF16, "rows"), (F32, "last_t")] if prompt else [(F32, "rows")]
    kv_b = [(BF16, "rows"), (F32, "by_head")] if prompt else [(F32, "by_head")]
    outs = ([(0, d_a, DH_A ** -0.5 * LOG2E, BF16, "rows")]
            + [(d_a, d_a, 1.0, dt, mode) for dt, mode in kv_a]
            + [(2 * d_a, d_a, 1.0, dt, mode) for dt, mode in kv_a]
            + [(3 * d_a, d_b, DH_B ** -0.5 * LOG2E, BF16, "rows")]
            + [(3 * d_a + d_b, d_b, 1.0, dt, mode) for dt, mode in kv_b]
            + [(3 * d_a + 2 * d_b, d_b, 1.0, dt, mode) for dt, mode in kv_b]
            + [(3 * d_a + 3 * d_b, d_m, DH_M ** -0.5 * LOG2E, BF16, "rows")])
    res = _norm_proj(x.reshape(n, d), w["g_mix"], w["w_in"], outs, tm, tiles_per_seq=max(t // tm, 1))
    res = [a.reshape(b, -1, a.shape[-1]) for a in res]
    if prompt:
        qa, ka, ka_keep, va, va_keep, qb, kb_bf, kb, vb_bf, vb, qm = res
    else:
        qa, ka, va, qb, kb, vb, qm = res
        ka_keep, va_keep = ka, va

    if cache is None:
        oa = _band_attn(qa, ka, va, _band_bias_table(w["rel_bias"], tile["band_q"]), tile["band_q"])
        ob = _diff_attn_prompt(qb, kb_bf, vb_bf, w["lam_q"], w["lam_k"], w["g_sub"], tile["diff_q"], lam_init)
        nm = mem.shape[1]
        mk, mv = _norm_proj(mem.reshape(b * nm, d), w["g_mem"], w["w_mem_kv"],
                            [(0, d_m, 1.0, F32, "by_head"), (d_m, d_m, 1.0, F32, "by_head")],
                            _token_tile(b * nm, WIN_A))
        mk = mk.reshape(b, nm * H_M, DH_M)
        mv = mv.reshape(b, nm * H_M, DH_M)
        fuse_mem = t % tile["merge"] == 0
        om = qm if fuse_mem else _mem_attn(qm, mk, mv, tile["mem_q"])
    else:
        fuse_mem = False
        oa = _band_attn(qa, ka, va, _band_bias_table(w["rel_bias"], tile["band_q"]), tile["band_q"],
                        hist=(cache["a_k_t"], cache["a_v_t"]))
        ob = _diff_attn_sample(qb, cache["b_k"], cache["b_v"], kb, vb,
                               w["lam_q"], w["lam_k"], w["g_sub"], tile["cache_keys"], tile["cache_item"], lam_init)
        mk, mv = cache["m_k"], cache["m_v"]
        om = _mem_attn(qm, mk, mv, tile["mem_q"])

    flat = lambda a: a.reshape(n, a.shape[-1])
    x1 = _merge(x.reshape(n, d), flat(oa), flat(ob), flat(om), w["g_mix"], w["w_in"], w["b_gate"],
                w["w_o"], w["w_out"], tile["merge"],
                mem=(mk, mv, t // tile["merge"]) if fuse_mem else None)
    y, conv = _conv_ffn(x1.reshape(b, t, d), w["g_ffn"], conv_prev, w["w_up"], w["w_conv"], w["b_conv"],
                        w["w_down"], w["g_final"], tile["ffn"], final_norm)
    return y, dict(ka=ka_keep, va=va_keep, kb=kb, vb=vb, mk=mk, mv=mv, conv=conv)


def kernel(x_prompt, x_sample, mem_prompt, cache_a_k, cache_a_v, cache_b_k, cache_b_v, cache_mem_k, cache_mem_v, state_ffn_conv, g_mix, w_in, b_gate, rel_bias, lam_q, lam_k, g_sub, g_mem, w_mem_kv, w_oa, w_ob, w_om, w_out, g_ffn, w_up, w_conv, b_conv, w_down, g_final):
    depth, d, _ = w_in.shape
    b, s, _ = x_prompt.shape
    bd, t, _ = x_sample.shape
    past = cache_b_k.shape[2]
    a_buf = cache_a_k.shape[2]
    ff = w_down.shape[1]
    d_a, d_b, d_m = H_A * DH_A, H_B * 2 * DH_B, H_M * DH_M
    a_keep = min(WIN_A, s)
    assert a_buf == WIN_A and a_keep == WIN_A and t == CHUNK and past % CHUNK == 0 and ff % FF_CHUNK == 0
    nch = ff // FF_CHUNK

    xp, xs = x_prompt, x_sample
    outs = {k: [] for k in ("pa_k", "pa_v", "pb_k", "pb_v", "pm_k", "pm_v", "pconv",
                            "sa_k", "sa_v", "sb_k", "sb_v", "sconv")}
    for l in range(depth):
        lam_init = 0.8 - 0.6 * math.exp(-0.3 * l)
        w = dict(
            g_mix=g_mix[l][None], g_mem=g_mem[l][None], g_ffn=g_ffn[l][None], g_final=g_final[None],
            g_sub=g_sub[l][None], lam_q=lam_q[l], lam_k=lam_k[l], rel_bias=rel_bias[l],
            w_in=w_in[l].astype(BF16),
            b_gate=b_gate[l][None],
            w_mem_kv=w_mem_kv[l].astype(BF16),
            w_o=jnp.stack([w_oa[l], w_ob[l], w_om[l]]).astype(BF16),
            w_out=w_out[l].astype(BF16),
            w_up=w_up[l].astype(BF16),
            w_conv=w_conv[l], b_conv=b_conv[l][None],
            w_down=w_down[l].astype(BF16),
        )
        last = l == depth - 1
        xp, po = _layer(xp, w, lam_init=lam_init, final_norm=last,
                        conv_prev=jnp.zeros((b, CONV_W - 1, ff), F32), mem=mem_prompt)
        feat_major = lambda a: a.transpose(0, 2, 3, 1).reshape(bd, d_a, a_buf)
        cache = dict(a_k=cache_a_k[l].reshape(bd, a_buf, d_a), a_v=cache_a_v[l].reshape(bd, a_buf, d_a),
                     a_k_t=feat_major(cache_a_k[l]), a_v_t=feat_major(cache_a_v[l]),
                     b_k=cache_b_k[l].reshape(bd, past * H_B, 2 * DH_B),
                     b_v=cache_b_v[l].reshape(bd, past * H_B, 2 * DH_B),
                     m_k=cache_mem_k[l].reshape(bd, -1, DH_M), m_v=cache_mem_v[l].reshape(bd, -1, DH_M))
        xs, so = _layer(xs, w, lam_init=lam_init, final_norm=last, conv_prev=state_ffn_conv[l], cache=cache)

        tok_major = lambda a: a.reshape(b, H_A, DH_A, a_keep).transpose(0, 3, 1, 2)
        outs["pa_k"].append(tok_major(po["ka"]))
        outs["pa_v"].append(tok_major(po["va"]))
        outs["pb_k"].append(po["kb"].reshape(b, s, H_B, 2 * DH_B))
        outs["pb_v"].append(po["vb"].reshape(b, s, H_B, 2 * DH_B))
        outs["pm_k"].append(po["mk"].reshape(b, -1, H_M, DH_M))
        outs["pm_v"].append(po["mv"].reshape(b, -1, H_M, DH_M))
        outs["pconv"].append(po["conv"])
        new_k = jnp.concatenate([cache["a_k"], so["ka"]], axis=1)[:, t:]
        new_v = jnp.concatenate([cache["a_v"], so["va"]], axis=1)[:, t:]
        outs["sa_k"].append(new_k.reshape(bd, a_buf, H_A, DH_A))
        outs["sa_v"].append(new_v.reshape(bd, a_buf, H_A, DH_A))
        outs["sb_k"].append(so["kb"].reshape(bd, t, H_B, 2 * DH_B))
        outs["sb_v"].append(so["vb"].reshape(bd, t, H_B, 2 * DH_B))
        outs["sconv"].append(so["conv"])

    st = {k: jnp.stack(v) for k, v in outs.items()}
    return (xp, xs, st["pa_k"], st["pa_v"], st["pb_k"], st["pb_v"], st["pm_k"], st["pm_v"], st["pconv"],
            st["sa_k"], st["sa_v"], st["sb_k"], st["sb_v"], st["sconv"])
```
